```python
import jax
import jax.numpy as jnp
from jax import lax
import numpy as np

D_MODEL = 2048
BATCH = 2
SEQ = 16384
DEPTH = 2

F32 = jnp.float32
GRID_W = 64
CTX_LEN = 256
EPS = 1e-6
CHUNK = 64
S5_CHUNK = 128

MIX_WIDTH = D_MODEL
A_WIDTH = MIX_WIDTH // 2
S5_H = 16
S5_G = A_WIDTH // S5_H
S5_P = 64
B_WIDTH = MIX_WIDTH - A_WIDTH
B_HEAD = 128
B_HEADS = B_WIDTH // B_HEAD
C_WIDTH = MIX_WIDTH // 2
C_QK = 128
C_V = 256
C_HEADS = C_WIDTH // C_V
ROPE_BASE = 10000.0
D_WIDTH = MIX_WIDTH - C_WIDTH
D_HEAD = 128
D_HEADS = D_WIDTH // D_HEAD
CONV_K = 3
AB_COLS = A_WIDTH + 5 * B_WIDTH
CD_COLS = 2 * C_HEADS * C_QK + 2 * C_WIDTH + 4 * D_WIDTH + 4 * D_HEADS
N_EXPERTS = 64
N_GROUPS = 8
TOPK_GROUPS = 4
TOP_K = 8
D_EXPERT = D_MODEL // 4
ROUTED_SCALE = 2.5
MOE_BLOCK = 256
N_EVEN = (DEPTH + 1) // 2
N_ODD = DEPTH // 2

kernel_name = 'hybrid_s5_hgrn2_retention_gdn_moe_prefix'


def rmsnorm(x, g):
    x32 = x.astype(F32)
    y = x32 * lax.rsqrt(jnp.mean(x32 * x32, axis=-1, keepdims=True) + EPS)
    return (y * g.astype(F32)).astype(x.dtype)


def head_rms(x):
    return x * lax.rsqrt(jnp.mean(x * x, axis=-1, keepdims=True) + EPS)


def l2norm(x):
    return x * lax.rsqrt(jnp.sum(x * x, axis=-1, keepdims=True) + EPS)


def modulate(h, shift, scale):
    return h * (1 + scale) + shift


def split_cols(z, sizes):
    return jnp.split(z, [int(s) for s in np.cumsum(sizes)[:-1]], axis=-1)


def bwd_order(t, n_ctx):
    return jnp.concatenate([jnp.flip(t[:, :n_ctx], 1), jnp.flip(t[:, n_ctx:], 1)], axis=1)


def to_chunks(t):
    bsz, n = t.shape[:2]
    t = t.reshape(bsz, n // CHUNK, CHUNK, *t.shape[2:])
    return jnp.moveaxis(jnp.moveaxis(t, 1, 0), 2, 3)


def from_chunks(t):
    t = jnp.moveaxis(jnp.moveaxis(t, 3, 2), 0, 1)
    return t.reshape(t.shape[0], t.shape[1] * t.shape[2], *t.shape[3:])


def centred_conv(t, w):
    k = w.shape[0]
    p = k // 2
    n = t.shape[1]
    tp = jnp.pad(t, ((0, 0), (p, k - 1 - p), (0, 0)))
    return sum(tp[:, j:j + n] * w[j] for j in range(k))


def axial_rotary(t, n_ctx):
    n_lat = t.shape[1] - n_ctx
    rows = n_lat // GRID_W
    row = jnp.repeat(jnp.arange(rows, dtype=F32), GRID_W)
    col = jnp.tile(jnp.arange(GRID_W, dtype=F32), rows)
    n_freq = C_QK // 4
    inv = ROPE_BASE ** (-jnp.arange(n_freq, dtype=F32) / n_freq)
    ang = jnp.concatenate([row[:, None] * inv, col[:, None] * inv], axis=-1)
    ang = jnp.concatenate([jnp.zeros((n_ctx, C_QK // 2), F32), ang], axis=0)[None, :, None, :]
    cos, sin = jnp.cos(ang), jnp.sin(ang)
    t1, t2 = t[..., :C_QK // 2], t[..., C_QK // 2:]
    return jnp.concatenate([t1 * cos - t2 * sin, t2 * cos + t1 * sin], axis=-1)


def s5_direction(u, lam_re, lam_im, log_dt, b_re, b_im, c_re, c_im):
    lam_re, lam_im, b_re, b_im, c_re, c_im = (p.astype(F32) for p in (lam_re, lam_im, b_re, b_im, c_re, c_im))
    dt = jnp.exp(log_dt.astype(F32))[:, None]
    mag = jnp.exp(lam_re * dt)
    ab_re, ab_im = mag * jnp.cos(lam_im * dt), mag * jnp.sin(lam_im * dt)
    den = lam_re * lam_re + lam_im * lam_im
    fr = ((ab_re - 1) * lam_re + ab_im * lam_im) / den
    fi = (ab_im * lam_re - (ab_re - 1) * lam_im) / den
    bb_re = fr[..., None] * b_re - fi[..., None] * b_im
    bb_im = fr[..., None] * b_im + fi[..., None] * b_re
    bsz, n, g, h = u.shape
    nc = n // S5_CHUNK
    uc = jnp.moveaxis(u.reshape(bsz, nc, S5_CHUNK, g, h), 1, 0)
    a_shape = (bsz, S5_CHUNK, g, S5_P)

    def combine(e1, e2):
        a1r, a1i, s1r, s1i = e1
        a2r, a2i, s2r, s2i = e2
        return (a2r * a1r - a2i * a1i, a2r * a1i + a2i * a1r,
                a2r * s1r - a2i * s1i + s2r, a2r * s1i + a2i * s1r + s2i)

    def step(carry, u_blk):
        s0r, s0i = carry
        bur = jnp.einsum('bcgh,gph->bcgp', u_blk, bb_re)
        bui = jnp.einsum('bcgh,gph->bcgp', u_blk, bb_im)
        pr, pim, sr, si = lax.associative_scan(
            combine, (jnp.broadcast_to(ab_re, a_shape), jnp.broadcast_to(ab_im, a_shape), bur, bui), axis=1)
        sr = sr + pr * s0r[:, None] - pim * s0i[:, None]
        si = si + pr * s0i[:, None] + pim * s0r[:, None]
        y = jnp.einsum('bcgp,ghp->bcgh', sr, c_re) - jnp.einsum('bcgp,ghp->bcgh', si, c_im)
        return (sr[:, -1], si[:, -1]), y

    init = (jnp.zeros((bsz, g, S5_P), F32), jnp.zeros((bsz, g, S5_P), F32))
    _, y = lax.scan(step, init, uc)
    return jnp.moveaxis(y, 0, 1).reshape(bsz, n, g, h)


def gla_chunk_scan(q, k, v, log_f):
    bsz, n, nh, dk = q.shape
    dv = v.shape[-1]
    mask = jnp.tril(jnp.ones((CHUNK, CHUNK), bool))[:, :, None]

    def step(s, blk):
        q_, k_, v_, g_ = blk
        b = jnp.cumsum(g_, axis=2)
        decay = jnp.exp(jnp.where(mask, b[:, :, :, None] - b[:, :, None], -jnp.inf))
        att = jnp.einsum('bhtsk,bhsk->bhts', q_[:, :, :, None] * decay, k_)
        o = jnp.einsum('bhts,bhsv->bhtv', att, v_) + jnp.einsum('bhtk,bhkv->bhtv', q_ * jnp.exp(b), s)
        b_last = b[:, :, -1:]
        s = jnp.exp(b_last[:, :, 0])[..., None] * s + jnp.einsum('bhsk,bhsv->bhkv', k_ * jnp.exp(b_last - b), v_)
        return s, o

    s0 = jnp.zeros((bsz, nh, dk, dv), F32)
    _, o = lax.scan(step, s0, (to_chunks(q), to_chunks(k), to_chunks(v), to_chunks(log_f)))
    return from_chunks(o)


def retention_chunk_scan(q, k, v, log_gamma):
    bsz, n, nh, dk = q.shape
    dv = v.shape[-1]
    idx = jnp.arange(CHUNK, dtype=F32)
    diff = idx[:, None] - idx[None, :]
    dmat = jnp.where(diff >= 0, jnp.exp(jnp.maximum(diff, 0.0) * log_gamma[:, None, None]), 0.0)
    q_dec = jnp.exp((idx + 1) * log_gamma[:, None])[..., None]
    k_dec = jnp.exp((CHUNK - 1 - idx) * log_gamma[:, None])[..., None]
    c_dec = jnp.exp(CHUNK * log_gamma)[:, None, None]

    def step(s, blk):
        q_, k_, v_ = blk
        att = jnp.einsum('bhtk,bhsk->bhts', q_, k_) * dmat
        o = jnp.einsum('bhts,bhsv->bhtv', att, v_) + jnp.einsum('bhtk,bhkv->bhtv', q_, s) * q_dec
        s = c_dec * s + jnp.einsum('bhsk,bhsv->bhkv', k_ * k_dec, v_)
        return s, o

    s0 = jnp.zeros((bsz, nh, dk, dv), F32)
    _, o = lax.scan(step, s0, (to_chunks(q), to_chunks(k), to_chunks(v)))
    return from_chunks(o)


def gated_delta_chunk_scan(q, k, v, beta, log_alpha):
    bsz, n, nh, dk = q.shape
    dv = v.shape[-1]
    strict = jnp.tril(jnp.ones((CHUNK, CHUNK), bool), -1)
    incl = jnp.tril(jnp.ones((CHUNK, CHUNK), bool))
    eye = jnp.eye(CHUNK, dtype=F32)

    def step(s, blk):
        q_, k_, v_, be, g_ = blk
        b = jnp.cumsum(g_, axis=-1)
        diff = b[..., :, None] - b[..., None, :]
        gam_s = jnp.exp(jnp.where(strict, diff, -jnp.inf))
        gam_i = jnp.exp(jnp.where(incl, diff, -jnp.inf))
        a = eye + be[..., None] * jnp.einsum('bhtk,bhsk->bhts', k_, k_) * gam_s
        rhs = jnp.concatenate([be[..., None] * v_, (be * jnp.exp(b))[..., None] * k_], axis=-1)
        sol = lax.linalg.triangular_solve(a, rhs, left_side=True, lower=True)
        u = sol[..., :dv] - jnp.einsum('bhtk,bhkv->bhtv', sol[..., dv:], s)
        qk = jnp.einsum('bhtk,bhsk->bhts', q_, k_) * gam_i
        o = jnp.einsum('bhtk,bhkv->bhtv', q_ * jnp.exp(b)[..., None], s) + jnp.einsum('bhts,bhsv->bhtv', qk, u)
        b_last = b[..., -1:]
        s = jnp.exp(b_last)[..., None] * s + jnp.einsum('bhsk,bhsv->bhkv', k_ * jnp.exp(b_last - b)[..., None], u)
        return s, o

    s0 = jnp.zeros((bsz, nh, dk, dv), F32)
    xs = (to_chunks(q), to_chunks(k), to_chunks(v), to_chunks(beta), to_chunks(log_alpha))
    _, o = lax.scan(step, s0, xs)
    return from_chunks(o)


def hgrn_gates(f_logit, lb):
    f = lb + (1 - lb) * jax.nn.sigmoid(f_logit)
    return (1 - lb) * jax.nn.sigmoid(-f_logit), jnp.log(f)


def mixer_s5_hgrn(h, n_ctx, w_in, lam_re, lam_im, log_dt, b_re, b_im, c_re, c_im, d_skip, w_glu, lb):
    bsz, n, _ = h.shape
    bo = lambda t: bwd_order(t, n_ctx)
    z = (h @ w_in).astype(F32)
    u, q, f_fw, f_bw, i_in, g = split_cols(z, [A_WIDTH] + [B_WIDTH] * 5)
    u = u.reshape(bsz, n, S5_G, S5_H)
    y_f = s5_direction(u, lam_re[0], lam_im[0], log_dt[0], b_re[0], b_im[0], c_re[0], c_im[0])
    y_b = bo(s5_direction(bo(u), lam_re[1], lam_im[1], log_dt[1], b_re[1], b_im[1], c_re[1], c_im[1]))
    y = (y_f + y_b + d_skip.astype(F32).reshape(S5_G, S5_H) * u).reshape(bsz, n, A_WIDTH)
    y = jax.nn.gelu(y)
    y_a = y * jax.nn.sigmoid(y @ w_glu.astype(F32))
    heads = lambda t: t.reshape(bsz, n, B_HEADS, B_HEAD)
    q = jax.nn.silu(heads(q))
    v = heads(i_in)
    lb = lb.reshape(2, B_HEADS, B_HEAD)
    k_f, lf_f = hgrn_gates(heads(f_fw), lb[0])
    k_b, lf_b = hgrn_gates(heads(f_bw), lb[1])
    o = gla_chunk_scan(q, k_f, v, lf_f) + bo(gla_chunk_scan(bo(q), bo(k_b), bo(v), bo(lf_b)))
    y_bg = (head_rms(o) * jax.nn.silu(heads(g))).reshape(bsz, n, B_WIDTH)
    return jnp.concatenate([y_a, y_bg], axis=-1).astype(h.dtype)


def mixer_ret_gdn(h, n_ctx, w_in, ret_decay_logit, conv_w, a_log, dt_bias, norm_g):
    bsz, n, _ = h.shape
    bo = lambda t: bwd_order(t, n_ctx)
    z = h @ w_in
    qc, kc, vc, gc, qkv, gd, be_f, be_b, a_f, a_b = split_cols(
        z, [C_HEADS * C_QK, C_HEADS * C_QK, C_WIDTH, C_WIDTH, 3 * D_WIDTH, D_WIDTH,
            D_HEADS, D_HEADS, D_HEADS, D_HEADS])
    qc = axial_rotary(qc.astype(F32).reshape(bsz, n, C_HEADS, C_QK), n_ctx) * C_QK ** -0.5
    kc = axial_rotary(kc.astype(F32).reshape(bsz, n, C_HEADS, C_QK), n_ctx)
    vc = vc.astype(F32).reshape(bsz, n, C_HEADS, C_V)
    log_gamma = jax.nn.log_sigmoid(ret_decay_logit.astype(F32))
    o = retention_chunk_scan(qc, kc, vc, log_gamma[0]) + bo(retention_chunk_scan(bo(qc), bo(kc), bo(vc), log_gamma[1]))
    y_c = (head_rms(o) * jax.nn.silu(gc.astype(F32).reshape(bsz, n, C_HEADS, C_V))).reshape(bsz, n, C_WIDTH)
    qkv = jnp.concatenate([centred_conv(qkv[:, :n_ctx], conv_w), centred_conv(qkv[:, n_ctx:], conv_w)], axis=1)
    qkv = jax.nn.silu(qkv.astype(F32))
    qd, kd, vd = (t.reshape(bsz, n, D_HEADS, D_HEAD) for t in jnp.split(qkv, 3, axis=-1))
    qd = l2norm(qd) * D_HEAD ** -0.5
    kd = l2norm(kd)
    a_log = a_log.astype(F32)
    dt_bias = dt_bias.astype(F32)
    la_f = -jnp.exp(a_log[0]) * jax.nn.softplus(a_f.astype(F32) + dt_bias[0])
    la_b = -jnp.exp(a_log[1]) * jax.nn.softplus(a_b.astype(F32) + dt_bias[1])
    beta_f = jax.nn.sigmoid(be_f.astype(F32))
    beta_b = jax.nn.sigmoid(be_b.astype(F32))
    o = gated_delta_chunk_scan(qd, kd, vd, beta_f, la_f) + bo(
        gated_delta_chunk_scan(bo(qd), bo(kd), bo(vd), bo(beta_b), bo(la_b)))
    gate = jax.nn.silu(gd.astype(F32).reshape(bsz, n, D_HEADS, D_HEAD))
    y_d = (head_rms(o) * norm_g.astype(F32) * gate).reshape(bsz, n, D_WIDTH)
    return jnp.concatenate([y_c, y_d], axis=-1).astype(h.dtype)


def swiglu(t, wg, wu, wd):
    return (jax.nn.silu(t @ wg) * (t @ wu)) @ wd


def routed_experts(t, eidx, gate, w_gate, w_up, w_down):
    n_tok = t.shape[0]
    n_assign = n_tok * TOP_K
    n_blocks = -(-n_assign // MOE_BLOCK) + N_EXPERTS
    flat_e = eidx.reshape(-1)
    order = jnp.argsort(flat_e)
    sorted_e = flat_e[order]
    counts = jnp.bincount(flat_e, length=N_EXPERTS)
    padded = (counts + MOE_BLOCK - 1) // MOE_BLOCK * MOE_BLOCK
    pad_end = jnp.cumsum(padded)
    pad_start = pad_end - padded
    start = jnp.cumsum(counts) - counts
    slot = pad_start[sorted_e] + jnp.arange(n_assign) - start[sorted_e]
    slot_tok = jnp.full((n_blocks * MOE_BLOCK,), n_tok, jnp.int32).at[slot].set((order // TOP_K).astype(jnp.int32))
    slot_gate = jnp.zeros((n_blocks * MOE_BLOCK,), t.dtype).at[slot].set(gate.reshape(-1)[order].astype(t.dtype))
    block_e = jnp.minimum(jnp.searchsorted(pad_end, jnp.arange(n_blocks) * MOE_BLOCK, side='right'), N_EXPERTS - 1)
    t_pad = jnp.concatenate([t, jnp.zeros((1, t.shape[1]), t.dtype)], axis=0)

    def step(out, blk):
        tok, g, e = blk
        xb = t_pad[tok]
        yb = swiglu(xb, w_gate[e], w_up[e], w_down[e]) * g[:, None]
        return out.at[tok].add(yb), None

    xs = (slot_tok.reshape(n_blocks, MOE_BLOCK), slot_gate.reshape(n_blocks, MOE_BLOCK), block_e)
    out, _ = lax.scan(step, jnp.zeros_like(t_pad), xs)
    return out[:n_tok]


def moe(h, w_router, router_bias, w_gate, w_up, w_down, ws_gate, ws_up, ws_down):
    t = h.reshape(-1, D_MODEL)
    n_tok = t.shape[0]
    scores = jax.nn.sigmoid((t @ w_router).astype(F32))
    sel = (scores + router_bias.astype(F32)).reshape(n_tok, N_GROUPS, N_EXPERTS // N_GROUPS)
    grp_score = lax.top_k(sel, 2)[0].sum(-1)
    _, gidx = lax.top_k(grp_score, TOPK_GROUPS)
    gmask = jnp.any(gidx[..., None] == jnp.arange(N_GROUPS), axis=1)
    masked = jnp.where(gmask[..., None], sel, -jnp.inf).reshape(n_tok, N_EXPERTS)
    _, eidx = lax.top_k(masked, TOP_K)
    gate = jnp.take_along_axis(scores, eidx, axis=1)
    gate = gate / jnp.sum(gate, axis=-1, keepdims=True) * ROUTED_SCALE
    y = routed_experts(t, eidx, gate, w_gate, w_up, w_down) + swiglu(t, ws_gate, ws_up, ws_down)
    return y.reshape(h.shape)


def setup_inputs(seed: int = 0) -> dict:
    key = jax.random.key(seed)
    ks = iter(jax.random.split(key, 48))

    def nrm(shape, scale):
        return jax.random.normal(next(ks), shape, F32) * scale

    def unif(shape, lo, hi):
        return jax.random.uniform(next(ks), shape, F32, lo, hi)

    lam_im = jnp.pi * jnp.arange(S5_P, dtype=F32)
    gam = 1.0 - 2.0 ** (-5.0 - jnp.arange(C_HEADS, dtype=F32))
    ret_logit = jnp.log(gam) - jnp.log1p(-gam)
    dt = jnp.exp(unif((N_ODD, 2, D_HEADS), float(np.log(1e-3)), float(np.log(1e-1))))
    return {
        'x': nrm((BATCH, SEQ, D_MODEL), 1.0),
        'c': nrm((BATCH, D_MODEL), 1.0),
        'ctx': nrm((BATCH, CTX_LEN, D_MODEL), 1.0),
        'c_ctx': nrm((D_MODEL,), 1.0),
        'w_ada': nrm((DEPTH, D_MODEL, 6 * D_MODEL), 0.5 * D_MODEL ** -0.5),
        'b_ada': nrm((DEPTH, 6 * D_MODEL), 0.02),
        'norm1_g': 1.0 + nrm((DEPTH, D_MODEL), 0.02),
        'norm2_g': 1.0 + nrm((DEPTH, D_MODEL), 0.02),
        'w_in_ab': nrm((N_EVEN, D_MODEL, AB_COLS), D_MODEL ** -0.5),
        's5_lambda_re': -0.5 + nrm((N_EVEN, 2, S5_G, S5_P), 0.01),
        's5_lambda_im': lam_im + nrm((N_EVEN, 2, S5_G, S5_P), 0.01),
        's5_log_dt': unif((N_EVEN, 2, S5_G), float(np.log(1e-3)), float(np.log(1e-1))),
        's5_b_re': nrm((N_EVEN, 2, S5_G, S5_P, S5_H), (2 * S5_H) ** -0.5),
        's5_b_im': nrm((N_EVEN, 2, S5_G, S5_P, S5_H), (2 * S5_H) ** -0.5),
        's5_c_re': nrm((N_EVEN, 2, S5_G, S5_H, S5_P), 0.5 * (2 * S5_P) ** -0.5),
        's5_c_im': nrm((N_EVEN, 2, S5_G, S5_H, S5_P), 0.5 * (2 * S5_P) ** -0.5),
        's5_d': nrm((N_EVEN, A_WIDTH), 1.0),
        's5_w_glu': nrm((N_EVEN, A_WIDTH, A_WIDTH), A_WIDTH ** -0.5),
        'hgrn_lb': nrm((N_EVEN + 1, 2, B_WIDTH), 0.1),
        'w_in_cd': nrm((N_ODD, D_MODEL, CD_COLS), D_MODEL ** -0.5),
        'ret_decay_logit': ret_logit + nrm((N_ODD, 2, C_HEADS), 0.05),
        'gdn_conv_w': nrm((N_ODD, CONV_K, 3 * D_WIDTH), CONV_K ** -0.5),
        'gdn_a_log': jnp.log(unif((N_ODD, 2, D_HEADS), 1.0, 16.0)),
        'gdn_dt_bias': dt + jnp.log(-jnp.expm1(-dt)),
        'gdn_norm_g': 1.0 + nrm((N_ODD, D_HEAD), 0.02),
        'w_out': nrm((DEPTH, MIX_WIDTH, D_MODEL), MIX_WIDTH ** -0.5),
        'w_router': nrm((DEPTH, D_MODEL, N_EXPERTS), D_MODEL ** -0.5),
        'router_bias': nrm((DEPTH, N_EXPERTS), 0.01),
        'w_exp_gate': nrm((DEPTH, N_EXPERTS, D_MODEL, D_EXPERT), D_MODEL ** -0.5),
        'w_exp_up': nrm((DEPTH, N_EXPERTS, D_MODEL, D_EXPERT), D_MODEL ** -0.5),
        'w_exp_down': nrm((DEPTH, N_EXPERTS, D_EXPERT, D_MODEL), D_EXPERT ** -0.5),
        'w_sh_gate': nrm((DEPTH, D_MODEL, D_EXPERT), D_MODEL ** -0.5),
        'w_sh_up': nrm((DEPTH, D_MODEL, D_EXPERT), D_MODEL ** -0.5),
        'w_sh_down': nrm((DEPTH, D_EXPERT, D_MODEL), D_EXPERT ** -0.5),
        'final_norm_g': 1.0 + nrm((D_MODEL,), 0.02),
    }


def reference(x, c, ctx, c_ctx, w_ada, b_ada, norm1_g, norm2_g,
              w_in_ab, s5_lambda_re, s5_lambda_im, s5_log_dt, s5_b_re, s5_b_im, s5_c_re, s5_c_im,
              s5_d, s5_w_glu, hgrn_lb,
              w_in_cd, ret_decay_logit, gdn_conv_w, gdn_a_log, gdn_dt_bias, gdn_norm_g,
              w_out, w_router, router_bias, w_exp_gate, w_exp_up, w_exp_down,
              w_sh_gate, w_sh_up, w_sh_down, final_norm_g):
    n_ctx = ctx.shape[1]
    lb_all = jnp.cumsum(jax.nn.softmax(hgrn_lb.astype(F32), axis=0), axis=0)
    xl, xc = x, ctx
    for i in range(DEPTH):
        last = i == DEPTH - 1
        j = i // 2
        mod_l = [m[:, None, :] for m in jnp.split(jax.nn.silu(c) @ w_ada[i] + b_ada[i], 6, axis=-1)]
        mod_c = jnp.split(jax.nn.silu(c_ctx) @ w_ada[i] + b_ada[i], 6, axis=-1)
        h = jnp.concatenate([modulate(rmsnorm(xc, norm1_g[i]), mod_c[0], mod_c[1]),
                             modulate(rmsnorm(xl, norm1_g[i]), mod_l[0], mod_l[1])], axis=1)
        if i % 2 == 0:
            mix = mixer_s5_hgrn(h, n_ctx, w_in_ab[j], s5_lambda_re[j], s5_lambda_im[j], s5_log_dt[j],
                                s5_b_re[j], s5_b_im[j], s5_c_re[j], s5_c_im[j], s5_d[j], s5_w_glu[j], lb_all[j])
        else:
            mix = mixer_ret_gdn(h, n_ctx, w_in_cd[j], ret_decay_logit[j], gdn_conv_w[j],
                                gdn_a_log[j], gdn_dt_bias[j], gdn_norm_g[j])
        moe_w = (w_router[i], router_bias[i], w_exp_gate[i], w_exp_up[i], w_exp_down[i],
                 w_sh_gate[i], w_sh_up[i], w_sh_down[i])
        if last:
            xl = xl + mod_l[2] * (mix[:, n_ctx:] @ w_out[i])
            xl = xl + mod_l[5] * moe(modulate(rmsnorm(xl, norm2_g[i]), mod_l[3], mod_l[4]), *moe_w)
        else:
            y = mix @ w_out[i]
            xc = xc + mod_c[2] * y[:, :n_ctx]
            xl = xl + mod_l[2] * y[:, n_ctx:]
            h = jnp.concatenate([modulate(rmsnorm(xc, norm2_g[i]), mod_c[3], mod_c[4]),
                                 modulate(rmsnorm(xl, norm2_g[i]), mod_l[3], mod_l[4])], axis=1)
            y = moe(h, *moe_w)
            xc = xc + mod_c[5] * y[:, :n_ctx]
            xl = xl + mod_l[5] * y[:, n_ctx:]
    return rmsnorm(xl, final_norm_g)
```

```python
import functools

import numpy as np
import jax
import jax.numpy as jnp
from jax import lax
from jax.experimental import pallas as pl
from jax.experimental.pallas import tpu as pltpu

F32 = jnp.float32
BF16 = jnp.bfloat16

EPS = 1e-6
GRID_W = 64
CHUNK = 64
S5_L = 16
S5_H = 16
S5_P = 64
B_HEAD = 128
C_QK = 128
C_V = 256
D_HEAD = 128
ROPE_BASE = 10000.0
N_EXPERTS = 64
N_GROUPS = 8
TOPK_GROUPS = 4
TOP_K = 8
ROUTED_SCALE = 2.5
MOE_BLOCK = 256
ROW_TILE = 256
VMEM_LIMIT_V7X = 56 * 1024 * 1024
EXP_CLAMP = 80.0


def _cparams(*sem):
    return pltpu.CompilerParams(dimension_semantics=sem, vmem_limit_bytes=VMEM_LIMIT_V7X)


def _silu(x):
    return x * jax.nn.sigmoid(x)


def _dot(a, b):
    return jnp.dot(a.astype(BF16), b.astype(BF16), preferred_element_type=F32)


def _dot_nt(a, b):
    return lax.dot_general(a.astype(BF16), b.astype(BF16), (((1,), (1,)), ((), ())), preferred_element_type=F32)


def _dot_tn(a, b):
    return lax.dot_general(a.astype(BF16), b.astype(BF16), (((0,), (0,)), ((), ())), preferred_element_type=F32)


def _split3(x):
    x1 = x.astype(BF16)
    r = x - x1.astype(F32)
    x2 = r.astype(BF16)
    x3 = (r - x2.astype(F32)).astype(BF16)
    return x1, x2, x3


def _tri_dot(tri, x):
    return sum(jnp.dot(tri, p, preferred_element_type=F32) for p in _split3(x))


def _dot_tri(x, tri):
    return sum(jnp.dot(p, tri, preferred_element_type=F32) for p in _split3(x))


def _tri_mask(n, lower):
    r = lax.broadcasted_iota(jnp.int32, (n, n), 0)
    c = lax.broadcasted_iota(jnp.int32, (n, n), 1)
    return (r >= c) if lower else (r <= c)


def _seg_of_tile(i, tiles_per_row, n_batch):
    return jnp.where(i % tiles_per_row == 0, n_batch, i // tiles_per_row)


def _norm_mod_kernel(x_ref, g_ref, shift_ref, scale_ref, o_ref):
    x = x_ref[...]
    y = x * lax.rsqrt(jnp.mean(x * x, axis=-1, keepdims=True) + EPS) * g_ref[...]
    o_ref[...] = (y * (1.0 + scale_ref[...]) + shift_ref[...]).astype(o_ref.dtype)


def norm_mod(xs, g, mod, k_shift, k_scale, tiles_per_row, n_batch):
    t, d = xs.shape
    seg = functools.partial(_seg_of_tile, tiles_per_row=tiles_per_row, n_batch=n_batch)
    return pl.pallas_call(
        _norm_mod_kernel,
        grid=(t // ROW_TILE,),
        in_specs=[
            pl.BlockSpec((ROW_TILE, d), lambda i: (i, 0)),
            pl.BlockSpec((1, d), lambda i: (0, 0)),
            pl.BlockSpec((None, None, 1, d), lambda i: (seg(i), k_shift, 0, 0)),
            pl.BlockSpec((None, None, 1, d), lambda i: (seg(i), k_scale, 0, 0)),
        ],
        out_specs=pl.BlockSpec((ROW_TILE, d), lambda i: (i, 0)),
        out_shape=jax.ShapeDtypeStruct((t, d), BF16),
        compiler_params=_cparams("parallel"),
        name="norm_mod",
    )(xs, g.reshape(1, d), mod, mod)


def _final_norm_kernel(x_ref, g_ref, o_ref):
    x = x_ref[...]
    o_ref[...] = x * lax.rsqrt(jnp.mean(x * x, axis=-1, keepdims=True) + EPS) * g_ref[...]


def final_norm(xs, g, n_batch, tiles_per_row, ctx_tiles):
    t, d = xs.shape
    lat_tiles = tiles_per_row - ctx_tiles
    return pl.pallas_call(
        _final_norm_kernel,
        grid=(n_batch, lat_tiles),
        in_specs=[
            pl.BlockSpec((ROW_TILE, d), lambda b, i: (b * tiles_per_row + ctx_tiles + i, 0)),
            pl.BlockSpec((1, d), lambda b, i: (0, 0)),
        ],
        out_specs=pl.BlockSpec((ROW_TILE, d), lambda b, i: (b * lat_tiles + i, 0)),
        out_shape=jax.ShapeDtypeStruct((n_batch * lat_tiles * ROW_TILE, d), F32),
        compiler_params=_cparams("parallel", "parallel"),
        name="final_norm",
    )(xs, g.reshape(1, d))


def _mm_kernel(a_ref, w_ref, o_ref):
    o_ref[...] = jnp.dot(a_ref[...].astype(BF16), w_ref[...].astype(BF16),
                         preferred_element_type=F32).astype(o_ref.dtype)


def matmul(a, w, out_dtype, tm, tn, name):
    m, k = a.shape
    n = w.shape[1]
    return pl.pallas_call(
        _mm_kernel,
        grid=(n // tn, m // tm),
        in_specs=[pl.BlockSpec((tm, k), lambda j, i: (i, 0)),
                  pl.BlockSpec((k, tn), lambda j, i: (0, j))],
        out_specs=pl.BlockSpec((tm, tn), lambda j, i: (i, j)),
        out_shape=jax.ShapeDtypeStruct((m, n), out_dtype),
        compiler_params=_cparams("parallel", "parallel"),
        name=name,
    )(a, w)


def _out_proj_kernel(a1_ref, a2_ref, w1_ref, w2_ref, res_ref, gate_ref, o_ref):
    y = jnp.dot(a1_ref[...], w1_ref[...], preferred_element_type=F32)
    y += jnp.dot(a2_ref[...], w2_ref[...], preferred_element_type=F32)
    o_ref[...] = res_ref[...] + gate_ref[...] * y


def out_proj(a1, a2, w, xs, mod, k_gate, tiles_per_row, n_batch, tn=1024):
    t, d = xs.shape
    k1, k2 = a1.shape[1], a2.shape[1]
    assert k1 == k2
    seg = functools.partial(_seg_of_tile, tiles_per_row=tiles_per_row, n_batch=n_batch)
    return pl.pallas_call(
        _out_proj_kernel,
        grid=(d // tn, t // ROW_TILE),
        in_specs=[
            pl.BlockSpec((ROW_TILE, k1), lambda j, i: (i, 0)),
            pl.BlockSpec((ROW_TILE, k2), lambda j, i: (i, 0)),
            pl.BlockSpec((k1, tn), lambda j, i: (0, j)),
            pl.BlockSpec((k2, tn), lambda j, i: (1, j)),
            pl.BlockSpec((ROW_TILE, tn), lambda j, i: (i, j)),
            pl.BlockSpec((None, None, 1, tn), lambda j, i: (seg(i), k_gate, 0, j)),
        ],
        out_specs=pl.BlockSpec((ROW_TILE, tn), lambda j, i: (i, j)),
        out_shape=jax.ShapeDtypeStruct((t, d), F32),
        compiler_params=_cparams("parallel", "parallel"),
        name="out_proj",
    )(a1, a2, w, w, xs, mod)


def _post_kernel(of_ref, ob_ref, gate_ref, ng_ref, o_ref, *, head_dim):
    o = of_ref[...] + ob_ref[...]
    g = gate_ref[...].astype(F32)
    width = o.shape[1]
    for h in range(width // head_dim):
        sl = slice(h * head_dim, (h + 1) * head_dim)
        oh = o[:, sl]
        y = oh * lax.rsqrt(jnp.mean(oh * oh, axis=-1, keepdims=True) + EPS) * ng_ref[...]
        o_ref[:, sl] = (y * _silu(g[:, sl])).astype(o_ref.dtype)


def post_norm_gate(o_f, o_b, z, gate_col_block, head_dim, norm_g):
    t, width = o_f.shape
    return pl.pallas_call(
        functools.partial(_post_kernel, head_dim=head_dim),
        grid=(t // ROW_TILE,),
        in_specs=[
            pl.BlockSpec((ROW_TILE, width), lambda i: (i, 0)),
            pl.BlockSpec((ROW_TILE, width), lambda i: (i, 0)),
            pl.BlockSpec((ROW_TILE, width), lambda i: (i, gate_col_block)),
            pl.BlockSpec((1, head_dim), lambda i: (0, 0)),
        ],
        out_specs=pl.BlockSpec((ROW_TILE, width), lambda i: (i, 0)),
        out_shape=jax.ShapeDtypeStruct((t, width), BF16),
        compiler_params=_cparams("parallel"),
        name="post_norm_gate",
    )(o_f, o_b, z, norm_g.reshape(1, head_dim).astype(F32))


def _bwd_chunk(j, ctx_chunks, n_chunks):
    return jnp.where(j < ctx_chunks, ctx_chunks - 1 - j, n_chunks + ctx_chunks - 1 - j)


def _s5_direction_tables(lam_re, lam_im, log_dt, b_re, b_im, c_re, c_im, reverse):
    hi = lax.Precision.HIGHEST
    ln = S5_L
    lam_re, lam_im, b_re, b_im, c_re, c_im = (p.astype(F32) for p in (lam_re, lam_im, b_re, b_im, c_re, c_im))
    dt = jnp.exp(log_dt.astype(F32))[:, None]
    mag = jnp.exp(lam_re * dt)
    ab_re, ab_im = mag * jnp.cos(lam_im * dt), mag * jnp.sin(lam_im * dt)
    den = lam_re * lam_re + lam_im * lam_im
    fr = ((ab_re - 1) * lam_re + ab_im * lam_im) / den
    fi = (ab_im * lam_re - (ab_re - 1) * lam_im) / den
    bb_re = fr[..., None] * b_re - fi[..., None] * b_im
    bb_im = fr[..., None] * b_im + fi[..., None] * b_re
    tau = jnp.arange(ln + 1, dtype=F32)[:, None, None]
    pw = jnp.exp(tau * (lam_re * dt))
    pr, pi = pw * jnp.cos(tau * (lam_im * dt)), pw * jnp.sin(tau * (lam_im * dt))
    abr = pr[..., None] * bb_re - pi[..., None] * bb_im
    abi = pr[..., None] * bb_im + pi[..., None] * bb_re
    kern = (jnp.einsum('ghp,tgpk->tghk', c_re, abr[:ln], precision=hi)
            - jnp.einsum('ghp,tgpk->tghk', c_im, abi[:ln], precision=hi))
    pos = jnp.arange(ln)
    lag = (pos[None, :] - pos[:, None]) if not reverse else (pos[:, None] - pos[None, :])
    toe = jnp.where((lag >= 0)[:, :, None, None, None], kern[jnp.clip(lag, 0, ln - 1)], 0.0)
    g = lam_re.shape[0]
    intra = jnp.transpose(toe, (2, 0, 4, 1, 3)).reshape(g, ln * S5_H, ln * S5_H)
    pw_in = (ln - 1 - pos) if not reverse else pos
    inj = jnp.concatenate([abr[pw_in], abi[pw_in]], axis=2)
    inject = jnp.transpose(inj, (1, 0, 3, 2)).reshape(g, ln * S5_H, 2 * S5_P)
    pw_out = (pos + 1) if not reverse else (ln - pos)
    w_re = c_re[None] * pr[pw_out][:, :, None, :] - c_im[None] * pi[pw_out][:, :, None, :]
    w_im = -(c_re[None] * pi[pw_out][:, :, None, :] + c_im[None] * pr[pw_out][:, :, None, :])
    readout = jnp.transpose(jnp.concatenate([w_re, w_im], axis=3), (1, 3, 0, 2)).reshape(g, 2 * S5_P, ln * S5_H)
    decay = jnp.stack([pr[ln], pi[ln]])
    return intra, inject, readout, decay


def _s5_in_kernel(u_ref, w_ref, yi_ref, s_ref):
    r = jnp.dot(u_ref[...], w_ref[...], preferred_element_type=F32)
    n_intra = yi_ref.shape[-1]
    yi_ref[...] = r[:, :n_intra]
    s_ref[...] = r[:, n_intra:]


def _s5_scan_kernel(a_ref, sf_ref, sb_ref, of_ref, ob_ref, st_ref):
    @pl.when(pl.program_id(1) == 0)
    def _():
        st_ref[...] = jnp.zeros_like(st_ref)

    steps = sf_ref.shape[0]
    for d, (s_ref, o_ref) in enumerate(((sf_ref, of_ref), (sb_ref, ob_ref))):
        ar, ai = a_ref[2 * d], a_ref[2 * d + 1]
        sr, si = st_ref[2 * d], st_ref[2 * d + 1]
        for q in range(steps):
            r = q if d == 0 else steps - 1 - q
            o_ref[r, 0] = sr
            o_ref[r, 1] = si
            sr, si = (ar * sr - ai * si + s_ref[r, 0], ar * si + ai * sr + s_ref[r, 1])
        st_ref[2 * d] = sr
        st_ref[2 * d + 1] = si


def _s5_out_kernel(yi_ref, st_ref, w_ref, u_ref, d_ref, o_ref):
    y = yi_ref[...] + jnp.dot(st_ref[...].astype(BF16), w_ref[...], preferred_element_type=F32)
    o_ref[...] = y + d_ref[...] * u_ref[...].astype(F32)


def _glu_kernel(y_ref, w_ref, o_ref):
    y = jax.nn.gelu(y_ref[...])
    o_ref[...] = (y * jax.nn.sigmoid(jnp.dot(y.astype(BF16), w_ref[...], preferred_element_type=F32))
                  ).astype(o_ref.dtype)


def mixer_s5(z, n_batch, n_tok, n_ctx, lam_re, lam_im, log_dt, b_re, b_im, c_re, c_im, d_skip, w_glu):
    t = z.shape[0]
    g = lam_re.shape[1]
    width = g * S5_H
    lh = S5_L * S5_H
    rows = t // S5_L
    tabs = [_s5_direction_tables(lam_re[d], lam_im[d], log_dt[d], b_re[d], b_im[d], c_re[d], c_im[d], d == 1)
            for d in range(2)]
    w_in = jnp.concatenate([tabs[0][0] + tabs[1][0], tabs[0][1], tabs[1][1]], axis=2).astype(BF16)
    w_st = jnp.concatenate([tabs[0][2], tabs[1][2]], axis=1).astype(BF16)
    decay = jnp.concatenate([tabs[0][3], tabs[1][3]], axis=0).reshape(4, g * S5_P // 128, 128)
    u = z[:, :width].reshape(rows, S5_L, g, S5_H)
    u_g = jnp.transpose(u, (2, 0, 1, 3)).reshape(g, rows, lh).astype(BF16)

    n_st = 4 * S5_P
    yi, s_in = pl.pallas_call(
        _s5_in_kernel,
        grid=(g,),
        in_specs=[pl.BlockSpec((None, rows, lh), lambda i: (i, 0, 0)),
                  pl.BlockSpec((None, lh, lh + n_st), lambda i: (i, 0, 0))],
        out_specs=[pl.BlockSpec((None, rows, lh), lambda i: (i, 0, 0)),
                   pl.BlockSpec((None, rows, n_st), lambda i: (i, 0, 0))],
        out_shape=[jax.ShapeDtypeStruct((g, rows, lh), F32), jax.ShapeDtypeStruct((g, rows, n_st), F32)],
        compiler_params=_cparams("parallel"),
        name="s5_in",
    )(u_g, w_in)

    gp = g * S5_P // 128
    s_scan = jnp.transpose(s_in.reshape(g, rows, 4, S5_P), (1, 2, 0, 3)).reshape(rows, 4, gp, 128)
    blocks_per_row = n_tok // S5_L
    cb = n_ctx // S5_L
    assert blocks_per_row % cb == 0
    nblk = blocks_per_row // cb
    bwd = lambda j: jnp.where(j == 0, 0, nblk - j)
    st_f, st_b = pl.pallas_call(
        _s5_scan_kernel,
        grid=(n_batch, nblk),
        in_specs=[pl.BlockSpec((4, gp, 128), lambda b, j: (0, 0, 0)),
                  pl.BlockSpec((cb, 2, gp, 128), lambda b, j: (b * nblk + j, 0, 0, 0)),
                  pl.BlockSpec((cb, 2, gp, 128), lambda b, j: (b * nblk + bwd(j), 1, 0, 0))],
        out_specs=[pl.BlockSpec((cb, 2, gp, 128), lambda b, j: (b * nblk + j, 0, 0, 0)),
                   pl.BlockSpec((cb, 2, gp, 128), lambda b, j: (b * nblk + bwd(j), 0, 0, 0))],
        out_shape=[jax.ShapeDtypeStruct((rows, 2, gp, 128), F32)] * 2,
        scratch_shapes=[pltpu.VMEM((4, gp, 128), F32)],
        compiler_params=_cparams("parallel", "arbitrary"),
        name="s5_scan",
    )(decay, s_scan, s_scan)
    st = jnp.concatenate([st_f, st_b], axis=1).reshape(rows, 4, g, S5_P)
    st_g = jnp.transpose(st, (2, 0, 1, 3)).reshape(g, rows, n_st)

    d_vec = jnp.tile(d_skip.astype(F32).reshape(g, 1, S5_H), (1, S5_L, 1)).reshape(g, 1, lh)
    y_g = pl.pallas_call(
        _s5_out_kernel,
        grid=(g,),
        in_specs=[pl.BlockSpec((None, rows, lh), lambda i: (i, 0, 0)),
                  pl.BlockSpec((None, rows, n_st), lambda i: (i, 0, 0)),
                  pl.BlockSpec((None, n_st, lh), lambda i: (i, 0, 0)),
                  pl.BlockSpec((None, rows, lh), lambda i: (i, 0, 0)),
                  pl.BlockSpec((None, 1, lh), lambda i: (i, 0, 0))],
        out_specs=pl.BlockSpec((None, rows, lh), lambda i: (i, 0, 0)),
        out_shape=jax.ShapeDtypeStruct((g, rows, lh), F32),
        compiler_params=_cparams("parallel"),
        name="s5_out",
    )(yi, st_g, w_st, u_g, d_vec)
    y = jnp.transpose(y_g.reshape(g, rows, S5_L, S5_H), (1, 2, 0, 3)).reshape(t, width)

    return pl.pallas_call(
        _glu_kernel,
        grid=(t // ROW_TILE,),
        in_specs=[pl.BlockSpec((ROW_TILE, width), lambda i: (i, 0)),
                  pl.BlockSpec((width, width), lambda i: (0, 0))],
        out_specs=pl.BlockSpec((ROW_TILE, width), lambda i: (i, 0)),
        out_shape=jax.ShapeDtypeStruct((t, width), BF16),
        compiler_params=_cparams("parallel"),
        name="s5_glu",
    )(y, w_glu.astype(BF16))


def _gla_kernel(lb_ref, qf_ref, ff_ref, vf_ref, qb_ref, fb_ref, vb_ref, of_ref, ob_ref, st_ref, *, n_heads, hd):
    @pl.when(pl.program_id(1) == 0)
    def _():
        st_ref[...] = jnp.zeros_like(st_ref)

    c = qf_ref.shape[0]
    for d, (q_ref, f_ref, v_ref, o_ref) in enumerate(((qf_ref, ff_ref, vf_ref, of_ref),
                                                       (qb_ref, fb_ref, vb_ref, ob_ref))):
        lower = d == 0
        mask = _tri_mask(c, lower)
        tri = mask.astype(BF16)
        lb = lb_ref[d]
        sig = jax.nn.sigmoid(f_ref[...])
        kk = (1.0 - lb) * (1.0 - sig)
        logf = jnp.log(lb + (1.0 - lb) * sig)
        b = _tri_dot(tri, logf)
        b_end = b[c - 1:c, :] if lower else b[0:1, :]
        qt = _silu(q_ref[...]) * jnp.exp(b)
        kt = kk * jnp.exp(jnp.minimum(-b, EXP_CLAMP))
        kend = kk * jnp.exp(b_end - b)
        dec = jnp.exp(b_end)
        v = v_ref[...]
        for h in range(n_heads):
            sl = slice(h * hd, (h + 1) * hd)
            s_t = st_ref[d, h]
            att = jnp.where(mask, _dot_nt(qt[:, sl], kt[:, sl]), 0.0)
            o_ref[:, sl] = _dot(att, v[:, sl]) + _dot_nt(qt[:, sl], s_t)
            st_ref[d, h] = dec[:, sl] * s_t + _dot_tn(v[:, sl], kend[:, sl])


def mixer_gla(z, n_batch, n_tok, n_ctx, col0, lb, n_heads, hd):
    width = n_heads * hd
    z3 = z.reshape(n_batch, n_tok, z.shape[1])
    nc, cc = n_tok // CHUNK, n_ctx // CHUNK
    cb0 = col0 // width
    bwd = functools.partial(_bwd_chunk, ctx_chunks=cc, n_chunks=nc)
    blk = (None, CHUNK, width)
    fw = lambda k: pl.BlockSpec(blk, lambda b, j: (b, j, cb0 + k))
    bw = lambda k: pl.BlockSpec(blk, lambda b, j: (b, bwd(j), cb0 + k))
    o_f, o_b = pl.pallas_call(
        functools.partial(_gla_kernel, n_heads=n_heads, hd=hd),
        grid=(n_batch, nc),
        in_specs=[pl.BlockSpec((2, 1, width), lambda b, j: (0, 0, 0)),
                  fw(0), fw(1), fw(3), bw(0), bw(2), bw(3)],
        out_specs=[pl.BlockSpec(blk, lambda b, j: (b, j, 0)),
                   pl.BlockSpec(blk, lambda b, j: (b, bwd(j), 0))],
        out_shape=[jax.ShapeDtypeStruct((n_batch, n_tok, width), F32)] * 2,
        scratch_shapes=[pltpu.VMEM((2, n_heads, hd, hd), F32)],
        compiler_params=_cparams("parallel", "arbitrary"),
        name="gla_scan",
    )(lb.reshape(2, 1, width).astype(F32), z3, z3, z3, z3, z3, z3)
    return o_f.reshape(-1, width), o_b.reshape(-1, width)


def _rotary_tables(n_tok, n_ctx):
    n_lat = n_tok - n_ctx
    rows = n_lat // GRID_W
    row = jnp.repeat(jnp.arange(rows, dtype=F32), GRID_W)
    col = jnp.tile(jnp.arange(GRID_W, dtype=F32), rows)
    n_freq = C_QK // 4
    inv = ROPE_BASE ** (-jnp.arange(n_freq, dtype=F32) / n_freq)
    ang = jnp.concatenate([row[:, None] * inv, col[:, None] * inv], axis=-1)
    ang = jnp.concatenate([jnp.zeros((n_ctx, C_QK // 2), F32), ang], axis=0)
    cos, sin = jnp.cos(ang), jnp.sin(ang)
    return jnp.concatenate([cos, cos], axis=-1), jnp.concatenate([-sin, sin], axis=-1)


def _ret_kernel(cdec_ref, dmat_ref, qdec_ref, kdec_ref,
                qf_ref, kf_ref, vf_ref, cf_ref, sf_ref, qb_ref, kb_ref, vb_ref, cb_ref, sb_ref,
                of_ref, ob_ref, st_ref, *, n_heads):
    @pl.when(pl.program_id(1) == 0)
    def _():
        st_ref[...] = jnp.zeros_like(st_ref)

    half = C_QK // 2
    for d, (q_ref, k_ref, v_ref, cos_ref, sin_ref, o_ref) in enumerate(
            ((qf_ref, kf_ref, vf_ref, cf_ref, sf_ref, of_ref), (qb_ref, kb_ref, vb_ref, cb_ref, sb_ref, ob_ref))):
        cos, sin = cos_ref[...], sin_ref[...]
        for h in range(n_heads):
            qs = slice(h * C_QK, (h + 1) * C_QK)
            vs = slice(h * C_V, (h + 1) * C_V)
            qh, kh = q_ref[:, qs].astype(F32), k_ref[:, qs].astype(F32)
            qh = (qh * cos + pltpu.roll(qh, half, axis=1) * sin) * (C_QK ** -0.5)
            kh = kh * cos + pltpu.roll(kh, half, axis=1) * sin
            vh = v_ref[:, vs]
            s = st_ref[d, h]
            att = _dot_nt(qh, kh) * dmat_ref[d, h]
            o_ref[:, vs] = _dot(att, vh) + _dot(qh * qdec_ref[d, h], s)
            st_ref[d, h] = cdec_ref[d, h] * s + _dot_tn(kh * kdec_ref[d, h], vh)


def mixer_retention(z, n_batch, n_tok, n_ctx, decay_logit, n_heads):
    qw, vw = n_heads * C_QK, n_heads * C_V
    z3 = z.reshape(n_batch, n_tok, z.shape[1])
    nc, cc = n_tok // CHUNK, n_ctx // CHUNK
    bwd = functools.partial(_bwd_chunk, ctx_chunks=cc, n_chunks=nc)
    log_gamma = jax.nn.log_sigmoid(decay_logit.astype(F32))[:, :, None, None]
    idx = jnp.arange(CHUNK, dtype=F32)
    diff = idx[:, None] - idx[None, :]
    dmat_f = jnp.where(diff >= 0, jnp.exp(jnp.maximum(diff, 0.0) * log_gamma[0]), 0.0)
    dmat_b = jnp.where(diff <= 0, jnp.exp(jnp.maximum(-diff, 0.0) * log_gamma[1]), 0.0)
    dmat = jnp.stack([dmat_f, dmat_b])
    ones = jnp.ones((1, 1, 1, C_QK), F32)
    pos_f, pos_b = idx[None, None, :, None], (CHUNK - 1 - idx)[None, None, :, None]
    lg = log_gamma
    qdec = jnp.concatenate([jnp.exp((pos_f + 1) * lg[0:1]), jnp.exp((pos_b + 1) * lg[1:2])]) * ones
    kdec = jnp.concatenate([jnp.exp((CHUNK - 1 - pos_f) * lg[0:1]), jnp.exp((CHUNK - 1 - pos_b) * lg[1:2])]) * ones
    cdec = jnp.exp(CHUNK * log_gamma[:, :, 0, 0])
    cos2, sin2 = _rotary_tables(n_tok, n_ctx)
    full = lambda shape: pl.BlockSpec(shape, lambda b, j: (0,) * len(shape))
    tab = lambda order: pl.BlockSpec((CHUNK, C_QK), lambda b, j: (order(j), 0))
    ident = lambda j: j
    vcb = 2 * qw // vw
    assert vcb * vw == 2 * qw
    def specs(order):
        return [pl.BlockSpec((None, CHUNK, qw), lambda b, j: (b, order(j), 0)),
                pl.BlockSpec((None, CHUNK, qw), lambda b, j: (b, order(j), 1)),
                pl.BlockSpec((None, CHUNK, vw), lambda b, j: (b, order(j), vcb)),
                tab(order), tab(order)]
    o_f, o_b = pl.pallas_call(
        functools.partial(_ret_kernel, n_heads=n_heads),
        grid=(n_batch, nc),
        in_specs=[pl.BlockSpec(memory_space=pltpu.SMEM), full((2, n_heads, CHUNK, CHUNK)),
                  full((2, n_heads, CHUNK, C_QK)), full((2, n_heads, CHUNK, C_QK))] + specs(ident) + specs(bwd),
        out_specs=[pl.BlockSpec((None, CHUNK, vw), lambda b, j: (b, j, 0)),
                   pl.BlockSpec((None, CHUNK, vw), lambda b, j: (b, bwd(j), 0))],
        out_shape=[jax.ShapeDtypeStruct((n_batch, n_tok, vw), F32)] * 2,
        scratch_shapes=[pltpu.VMEM((2, n_heads, C_QK, C_V), F32)],
        compiler_params=_cparams("parallel", "arbitrary"),
        name="retention_scan",
    )(cdec, dmat, qdec, kdec, z3, z3, z3, cos2, sin2, z3, z3, z3, cos2, sin2)
    return o_f.reshape(-1, vw), o_b.reshape(-1, vw)


def _gdn_conv_kernel(prev_ref, cur_ref, next_ref, w_ref, o_ref, *, tiles_per_row, ctx_tiles, n_heads, hd):
    i = pl.program_id(0)
    r = i % tiles_per_row
    first = jnp.logical_or(r == 0, r == ctx_tiles)
    last = jnp.logical_or(r == ctx_tiles - 1, r == tiles_per_row - 1)
    x = cur_ref[...].astype(F32)
    rows = x.shape[0]
    rid = lax.broadcasted_iota(jnp.int32, x.shape, 0)
    hp = prev_ref.shape[0]
    x_prev = jnp.where(first, 0.0, prev_ref[hp - 1:hp, :].astype(F32))
    x_next = jnp.where(last, 0.0, next_ref[0:1, :].astype(F32))
    left = jnp.where(rid == 0, x_prev, pltpu.roll(x, 1, axis=0))
    right = jnp.where(rid == rows - 1, x_next, pltpu.roll(x, rows - 1, axis=0))
    w = w_ref[...]
    y = _silu(left * w[0:1, :] + x * w[1:2, :] + right * w[2:3, :])
    width = n_heads * hd
    for h in range(3 * n_heads):
        sl = slice(h * hd, (h + 1) * hd)
        yh = y[:, sl]
        if h < 2 * n_heads:
            yh = yh * lax.rsqrt(jnp.sum(yh * yh, axis=-1, keepdims=True) + EPS)
            if h < n_heads:
                yh = yh * (hd ** -0.5)
        o_ref[:, sl] = yh.astype(o_ref.dtype)


def _gdn_kernel(qf_ref, kf_ref, vf_ref, cf_ref, rf_ref, qb_ref, kb_ref, vb_ref, cb_ref, rb_ref,
                of_ref, ob_ref, st_ref, *, n_heads, hd):
    @pl.when(pl.program_id(1) == 0)
    def _():
        st_ref[...] = jnp.zeros_like(st_ref)

    c = qf_ref.shape[0]
    eye = (lax.broadcasted_iota(jnp.int32, (c, c), 0) == lax.broadcasted_iota(jnp.int32, (c, c), 1)).astype(F32)
    for d, (q_ref, k_ref, v_ref, col_ref, row_ref, o_ref) in enumerate(
            ((qf_ref, kf_ref, vf_ref, cf_ref, rf_ref, of_ref), (qb_ref, kb_ref, vb_ref, cb_ref, rb_ref, ob_ref))):
        lower = d == 0
        incl = _tri_mask(c, lower)
        strict = jnp.logical_and(incl, eye == 0.0)
        col = col_ref[...]
        row = row_ref[...]
        b_cols = _tri_dot(incl.astype(BF16), col)
        b_rows = _dot_tri(row, _tri_mask(c, not lower).astype(BF16))
        for h in range(n_heads):
            sl = slice(h * hd, (h + 1) * hd)
            ib, ig = d * n_heads + h, (2 + d) * n_heads + h
            be = col[:, ib:ib + 1]
            bc = b_cols[:, ig:ig + 1]
            br = b_rows[ig:ig + 1, :]
            b_end = bc[c - 1:c, :] if lower else bc[0:1, :]
            diff = bc - br
            gam_i = jnp.exp(jnp.where(incl, diff, -1e30))
            gam_s = jnp.where(strict, gam_i, 0.0)
            qh, kh, vh = q_ref[:, sl], k_ref[:, sl], v_ref[:, sl]
            s = st_ref[d, h]
            n = be * _dot_nt(kh, kh) * gam_s
            inv = eye - n
            p = _dot(n, n)
            for it in range(int(np.log2(c)) - 1):
                inv = inv + _dot(inv, p)
                if it < int(np.log2(c)) - 2:
                    p = _dot(p, p)
            rhs = jnp.concatenate([be * vh.astype(F32), (be * jnp.exp(bc)) * kh.astype(F32)], axis=1)
            sol = _dot(inv, rhs)
            u = sol[:, :hd] - _dot(sol[:, hd:], s)
            qk = _dot_nt(qh, kh) * gam_i
            o_ref[:, sl] = _dot(qh.astype(F32) * jnp.exp(bc), s) + _dot(qk, u)
            st_ref[d, h] = jnp.exp(b_end) * s + _dot_tn(kh.astype(F32) * jnp.exp(b_end - bc), u)


def mixer_gdn(z, n_batch, n_tok, n_ctx, col0, conv_w, a_log, dt_bias, n_heads, hd):
    t, zw = z.shape
    width = n_heads * hd
    tiles_per_row, ctx_tiles = n_tok // ROW_TILE, n_ctx // ROW_TILE
    cb0 = col0 // (3 * width)
    assert cb0 * 3 * width == col0
    halo = 8
    hpt = ROW_TILE // halo
    n_halo = t // halo
    qkv = pl.pallas_call(
        functools.partial(_gdn_conv_kernel, tiles_per_row=tiles_per_row, ctx_tiles=ctx_tiles, n_heads=n_heads, hd=hd),
        grid=(t // ROW_TILE,),
        in_specs=[pl.BlockSpec((halo, 3 * width), lambda i: (jnp.maximum(i * hpt - 1, 0), cb0)),
                  pl.BlockSpec((ROW_TILE, 3 * width), lambda i: (i, cb0)),
                  pl.BlockSpec((halo, 3 * width), lambda i: (jnp.minimum((i + 1) * hpt, n_halo - 1), cb0)),
                  pl.BlockSpec((3, 3 * width), lambda i: (0, 0))],
        out_specs=pl.BlockSpec((ROW_TILE, 3 * width), lambda i: (i, 0)),
        out_shape=jax.ShapeDtypeStruct((t, 3 * width), BF16),
        compiler_params=_cparams("parallel"),
        name="gdn_conv",
    )(z, z, z, conv_w.astype(F32))

    sc0 = col0 + 4 * width
    small = z[:, sc0:sc0 + 4 * n_heads].astype(F32)
    a_log, dt_bias = a_log.astype(F32), dt_bias.astype(F32)
    be = jax.nn.sigmoid(small[:, :2 * n_heads])
    la_f = -jnp.exp(a_log[0]) * jax.nn.softplus(small[:, 2 * n_heads:3 * n_heads] + dt_bias[0])
    la_b = -jnp.exp(a_log[1]) * jax.nn.softplus(small[:, 3 * n_heads:] + dt_bias[1])
    cols = jnp.concatenate([be, la_f, la_b], axis=1)
    nc, cc = n_tok // CHUNK, n_ctx // CHUNK
    cols3 = cols.reshape(n_batch, n_tok, 4 * n_heads)
    rows4 = jnp.transpose(cols.reshape(n_batch, nc, CHUNK, 4 * n_heads), (0, 1, 3, 2))
    qkv3 = qkv.reshape(n_batch, n_tok, 3 * width)
    bwd = functools.partial(_bwd_chunk, ctx_chunks=cc, n_chunks=nc)
    ident = lambda j: j
    def specs(order):
        return [pl.BlockSpec((None, CHUNK, width), lambda b, j: (b, order(j), 0)),
                pl.BlockSpec((None, CHUNK, width), lambda b, j: (b, order(j), 1)),
                pl.BlockSpec((None, CHUNK, width), lambda b, j: (b, order(j), 2)),
                pl.BlockSpec((None, CHUNK, 4 * n_heads), lambda b, j: (b, order(j), 0)),
                pl.BlockSpec((None, None, 4 * n_heads, CHUNK), lambda b, j: (b, order(j), 0, 0))]
    o_f, o_b = pl.pallas_call(
        functools.partial(_gdn_kernel, n_heads=n_heads, hd=hd),
        grid=(n_batch, nc),
        in_specs=specs(ident) + specs(bwd),
        out_specs=[pl.BlockSpec((None, CHUNK, width), lambda b, j: (b, j, 0)),
                   pl.BlockSpec((None, CHUNK, width), lambda b, j: (b, bwd(j), 0))],
        out_shape=[jax.ShapeDtypeStruct((n_batch, n_tok, width), F32)] * 2,
        scratch_shapes=[pltpu.VMEM((2, n_heads, hd, hd), F32)],
        compiler_params=_cparams("parallel", "arbitrary"),
        name="gdn_scan",
    )(qkv3, qkv3, qkv3, cols3, rows4, qkv3, qkv3, qkv3, cols3, rows4)
    return o_f.reshape(-1, width), o_b.reshape(-1, width)


def _expert_kernel(be_ref, nu_ref, x_ref, wg_ref, wu_ref, wd_ref, o_ref):
    i = pl.program_id(0)

    @pl.when(i < nu_ref[0])
    def _():
        x = x_ref[...]
        g = jnp.dot(x, wg_ref[...], preferred_element_type=F32)
        u = jnp.dot(x, wu_ref[...], preferred_element_type=F32)
        a = (_silu(g) * u).astype(BF16)
        o_ref[...] = jnp.dot(a, wd_ref[...], preferred_element_type=F32).astype(o_ref.dtype)

    @pl.when(i >= nu_ref[0])
    def _():
        o_ref[...] = jnp.zeros_like(o_ref)


def _shared_kernel(h_ref, r_ref, xs_ref, gate_ref, wg_ref, wu_ref, wd_ref, o_ref):
    h = h_ref[...]
    g = jnp.dot(h, wg_ref[...], preferred_element_type=F32)
    u = jnp.dot(h, wu_ref[...], preferred_element_type=F32)
    a = (_silu(g) * u).astype(BF16)
    y = jnp.dot(a, wd_ref[...], preferred_element_type=F32)
    o_ref[...] = xs_ref[...] + gate_ref[...] * (y + r_ref[...])


def _route(logits, router_bias):
    n_tok = logits.shape[0]
    scores = jax.nn.sigmoid(logits)
    sel = (scores + router_bias.astype(F32)).reshape(n_tok, N_GROUPS, N_EXPERTS // N_GROUPS)
    grp_score = lax.top_k(sel, 2)[0].sum(-1)
    _, gidx = lax.top_k(grp_score, TOPK_GROUPS)
    gmask = jnp.any(gidx[..., None] == jnp.arange(N_GROUPS), axis=1)
    masked = jnp.where(gmask[..., None], sel, -jnp.inf).reshape(n_tok, N_EXPERTS)
    _, eidx = lax.top_k(masked, TOP_K)
    gate = jnp.take_along_axis(scores, eidx, axis=1)
    gate = gate / jnp.sum(gate, axis=-1, keepdims=True) * ROUTED_SCALE
    return eidx, gate


def moe_block(h2, xs, mod, k_gate, tiles_per_row, n_batch,
              w_router, router_bias, w_gate, w_up, w_down, ws_gate, ws_up, ws_down):
    t, d = h2.shape
    de = ws_gate.shape[1]
    wr = jnp.pad(w_router.astype(BF16), ((0, 0), (0, 128 - N_EXPERTS)))
    logits = matmul(h2, wr, F32, ROW_TILE, 128, "router")[:, :N_EXPERTS]
    eidx, gate = _route(logits, router_bias)

    n_assign = t * TOP_K
    n_blocks = -(-n_assign // MOE_BLOCK) + N_EXPERTS
    flat_e = eidx.reshape(-1)
    order = jnp.argsort(flat_e)
    sorted_e = flat_e[order]
    counts = jnp.bincount(flat_e, length=N_EXPERTS)
    padded = (counts + MOE_BLOCK - 1) // MOE_BLOCK * MOE_BLOCK
    pad_end = jnp.cumsum(padded)
    pad_start = pad_end - padded
    start = jnp.cumsum(counts) - counts
    slot = (pad_start[sorted_e] + jnp.arange(n_assign) - start[sorted_e]).astype(jnp.int32)
    slot_tok = jnp.zeros((n_blocks * MOE_BLOCK,), jnp.int32).at[slot].set((order // TOP_K).astype(jnp.int32))
    slot_of_assign = jnp.zeros((n_assign,), jnp.int32).at[order].set(slot)
    block_e = jnp.minimum(jnp.searchsorted(pad_end, jnp.arange(n_blocks) * MOE_BLOCK, side='right'),
                          N_EXPERTS - 1).astype(jnp.int32)
    n_used = (pad_end[-1] // MOE_BLOCK).astype(jnp.int32).reshape(1)

    x_sorted = h2[slot_tok]
    y_sorted = pl.pallas_call(
        _expert_kernel,
        grid_spec=pltpu.PrefetchScalarGridSpec(
            num_scalar_prefetch=2,
            grid=(n_blocks,),
            in_specs=[pl.BlockSpec((MOE_BLOCK, d), lambda i, be, nu: (jnp.minimum(i, nu[0] - 1), 0)),
                      pl.BlockSpec((None, d, de), lambda i, be, nu: (be[i], 0, 0)),
                      pl.BlockSpec((None, d, de), lambda i, be, nu: (be[i], 0, 0)),
                      pl.BlockSpec((None, de, d), lambda i, be, nu: (be[i], 0, 0))],
            out_specs=pl.BlockSpec((MOE_BLOCK, d), lambda i, be, nu: (i, 0)),
        ),
        out_shape=jax.ShapeDtypeStruct((n_blocks * MOE_BLOCK, d), BF16),
        compiler_params=_cparams("arbitrary"),
        name="routed_experts",
    )(block_e, n_used, x_sorted, w_gate.astype(BF16), w_up.astype(BF16), w_down.astype(BF16))
    routed = jnp.sum(y_sorted[slot_of_assign].reshape(t, TOP_K, d).astype(F32) * gate[:, :, None], axis=1)

    seg = functools.partial(_seg_of_tile, tiles_per_row=tiles_per_row, n_batch=n_batch)
    return pl.pallas_call(
        _shared_kernel,
        grid=(t // ROW_TILE,),
        in_specs=[pl.BlockSpec((ROW_TILE, d), lambda i: (i, 0)),
                  pl.BlockSpec((ROW_TILE, d), lambda i: (i, 0)),
                  pl.BlockSpec((ROW_TILE, d), lambda i: (i, 0)),
                  pl.BlockSpec((None, None, 1, d), lambda i: (seg(i), k_gate, 0, 0)),
                  pl.BlockSpec((d, de), lambda i: (0, 0)),
                  pl.BlockSpec((d, de), lambda i: (0, 0)),
                  pl.BlockSpec((de, d), lambda i: (0, 0))],
        out_specs=pl.BlockSpec((ROW_TILE, d), lambda i: (i, 0)),
        out_shape=jax.ShapeDtypeStruct((t, d), F32),
        compiler_params=_cparams("parallel"),
        name="shared_expert",
    )(h2, routed, xs, mod, ws_gate.astype(BF16), ws_up.astype(BF16), ws_down.astype(BF16))


def kernel(x, c, ctx, c_ctx, w_ada, b_ada, norm1_g, norm2_g, w_in_ab, s5_lambda_re, s5_lambda_im, s5_log_dt, s5_b_re, s5_b_im, s5_c_re, s5_c_im, s5_d, s5_w_glu, hgrn_lb, w_in_cd, ret_decay_logit, gdn_conv_w, gdn_a_log, gdn_dt_bias, gdn_norm_g, w_out, w_router, router_bias, w_exp_gate, w_exp_up, w_exp_down, w_sh_gate, w_sh_up, w_sh_down, final_norm_g):
    n_batch, n_lat, d = x.shape
    n_ctx = ctx.shape[1]
    n_tok = n_ctx + n_lat
    depth = w_ada.shape[0]
    assert n_ctx % ROW_TILE == 0 and n_lat % ROW_TILE == 0
    tiles_per_row, ctx_tiles = n_tok // ROW_TILE, n_ctx // ROW_TILE
    assert ctx_tiles == 1
    t = n_batch * n_tok

    xs = jnp.concatenate([ctx, x], axis=1).reshape(t, d)
    lb_all = jnp.cumsum(jax.nn.softmax(hgrn_lb.astype(F32), axis=0), axis=0)
    cond = jnp.concatenate([c, c_ctx[None]], axis=0)
    cond = jnp.pad(jax.nn.silu(cond), ((0, 8 - (n_batch + 1)), (0, 0)))
    ones_g = jnp.ones((D_HEAD,), F32)

    for i in range(depth):
        j = i // 2
        mod = matmul(cond, w_ada[i], F32, 8, 1024, "ada_mod")[:n_batch + 1] + b_ada[i]
        mod = mod.reshape(n_batch + 1, 6, 1, d)
        h = norm_mod(xs, norm1_g[i], mod, 0, 1, tiles_per_row, n_batch)
        if i % 2 == 0:
            z = matmul(h, w_in_ab[j].astype(BF16), F32, 512, 1024, "in_proj")
            a_width = s5_d.shape[1]
            b_heads = (z.shape[1] - a_width) // 5 // B_HEAD
            y_a = mixer_s5(z, n_batch, n_tok, n_ctx, s5_lambda_re[j], s5_lambda_im[j], s5_log_dt[j],
                           s5_b_re[j], s5_b_im[j], s5_c_re[j], s5_c_im[j], s5_d[j], s5_w_glu[j])
            o_f, o_b = mixer_gla(z, n_batch, n_tok, n_ctx, a_width, lb_all[j], b_heads, B_HEAD)
            gcb = (a_width + 4 * b_heads * B_HEAD) // (b_heads * B_HEAD)
            y_b = post_norm_gate(o_f, o_b, z, gcb, B_HEAD, jnp.ones((B_HEAD,), F32))
            m1, m2 = y_a, y_b
        else:
            cd_cols = w_in_cd.shape[2]
            cd_pad = -(-cd_cols // 1024) * 1024
            w_cd = jnp.pad(w_in_cd[j].astype(BF16), ((0, 0), (0, cd_pad - cd_cols)))
            z = matmul(h, w_cd, F32, 512, 1024, "in_proj")
            c_heads = ret_decay_logit.shape[2]
            d_heads = gdn_a_log.shape[2]
            o_f, o_b = mixer_retention(z, n_batch, n_tok, n_ctx, ret_decay_logit[j], c_heads)
            y_c = post_norm_gate(o_f, o_b, z, (2 * c_heads * C_QK + c_heads * C_V) // (c_heads * C_V), C_V,
                                 jnp.ones((C_V,), F32))
            col0 = 2 * c_heads * C_QK + 2 * c_heads * C_V
            o_f, o_b = mixer_gdn(z, n_batch, n_tok, n_ctx, col0, gdn_conv_w[j], gdn_a_log[j], gdn_dt_bias[j],
                                 d_heads, D_HEAD)
            y_d = post_norm_gate(o_f, o_b, z, (col0 + 3 * d_heads * D_HEAD) // (d_heads * D_HEAD), D_HEAD,
                                 gdn_norm_g[j])
            m1, m2 = y_c, y_d
        xs = out_proj(m1, m2, w_out[i].astype(BF16), xs, mod, 2, tiles_per_row, n_batch)
        h2 = norm_mod(xs, norm2_g[i], mod, 3, 4, tiles_per_row, n_batch)
        xs = moe_block(h2, xs, mod, 5, tiles_per_row, n_batch, w_router[i], router_bias[i],
                       w_exp_gate[i], w_exp_up[i], w_exp_down[i], w_sh_gate[i], w_sh_up[i], w_sh_down[i])
    out = final_norm(xs, final_norm_g, n_batch, tiles_per_row, ctx_tiles)
    return out.reshape(n_batch, n_lat, d)
```

```python
import functools

import numpy as np
import jax
import jax.numpy as jnp
from jax import lax
from jax.experimental import pallas as pl
from jax.experimental.pallas import tpu as pltpu

F32 = jnp.float32
BF16 = jnp.bfloat16

EPS = 1e-6
GRID_W = 64
CHUNK = 64
S5_L = 16
S5_H = 16
S5_P = 64
B_HEAD = 128
C_QK = 128
C_V = 256
D_HEAD = 128
ROPE_BASE = 10000.0
N_EXPERTS = 64
N_GROUPS = 8
TOPK_GROUPS = 4
TOP_K = 8
ROUTED_SCALE = 2.5
MOE_BLOCK = 256
ROW_TILE = 256
VMEM_LIMIT_V7X = 56 * 1024 * 1024
EXP_CLAMP = 80.0


def _cparams(*sem):
    return pltpu.CompilerParams(dimension_semantics=sem, vmem_limit_bytes=VMEM_LIMIT_V7X)


def _silu(x):
    return x * jax.nn.sigmoid(x)


def _dot(a, b):
    return jnp.dot(a.astype(BF16), b.astype(BF16), preferred_element_type=F32)


def _dot_nt(a, b):
    return lax.dot_general(a.astype(BF16), b.astype(BF16), (((1,), (1,)), ((), ())), preferred_element_type=F32)


def _dot_tn(a, b):
    return lax.dot_general(a.astype(BF16), b.astype(BF16), (((0,), (0,)), ((), ())), preferred_element_type=F32)


def _split3(x):
    x1 = x.astype(BF16)
    r = x - x1.astype(F32)
    x2 = r.astype(BF16)
    x3 = (r - x2.astype(F32)).astype(BF16)
    return x1, x2, x3


def _tri_dot(tri, x):
    return sum(jnp.dot(tri, p, preferred_element_type=F32) for p in _split3(x))


def _dot_tri(x, tri):
    return sum(jnp.dot(p, tri, preferred_element_type=F32) for p in _split3(x))


def _tri_mask(n, lower):
    r = lax.broadcasted_iota(jnp.int32, (n, n), 0)
    c = lax.broadcasted_iota(jnp.int32, (n, n), 1)
    return (r >= c) if lower else (r <= c)


def _seg_of_tile(i, tiles_per_row, n_batch):
    return jnp.where(i % tiles_per_row == 0, n_batch, i // tiles_per_row)


def _pack_halves(y):
    half = y.shape[1] // 2
    lo = lax.bitcast_convert_type(y[:, :half].astype(BF16).astype(F32), jnp.uint32) >> 16
    hi = lax.bitcast_convert_type(y[:, half:].astype(BF16).astype(F32), jnp.uint32) & jnp.uint32(0xFFFF0000)
    return hi | lo


def _unpack_halves(w):
    lo = lax.bitcast_convert_type(w << 16, F32)
    hi = lax.bitcast_convert_type(w & jnp.uint32(0xFFFF0000), F32)
    return lo, hi


LANES = 128


def _store_rows(ref, packed):
    for s in range(ref.shape[1]):
        ref[:, s, :] = packed[:, s * LANES:(s + 1) * LANES]


def _load_rows(ref, row0, rows):
    return jnp.concatenate([ref[row0:row0 + rows, s, :] for s in range(ref.shape[1])], axis=1)


def _norm_mod_kernel(x_ref, g_ref, shift_ref, scale_ref, o_ref, *packed_ref):
    x = x_ref[...]
    y = x * lax.rsqrt(jnp.mean(x * x, axis=-1, keepdims=True) + EPS) * g_ref[...]
    y = y * (1.0 + scale_ref[...]) + shift_ref[...]
    o_ref[...] = y.astype(o_ref.dtype)
    if packed_ref:
        _store_rows(packed_ref[0], _pack_halves(y))


def norm_mod(xs, g, mod, k_shift, k_scale, tiles_per_row, n_batch, packed=False):
    t, d = xs.shape
    seg = functools.partial(_seg_of_tile, tiles_per_row=tiles_per_row, n_batch=n_batch)
    out_specs = [pl.BlockSpec((ROW_TILE, d), lambda i: (i, 0))]
    out_shape = [jax.ShapeDtypeStruct((t, d), BF16)]
    if packed:
        out_specs.append(pl.BlockSpec((ROW_TILE, d // 2 // LANES, LANES), lambda i: (i, 0, 0)))
        out_shape.append(jax.ShapeDtypeStruct((t, d // 2 // LANES, LANES), jnp.uint32))
    res = pl.pallas_call(
        _norm_mod_kernel,
        grid=(t // ROW_TILE,),
        in_specs=[
            pl.BlockSpec((ROW_TILE, d), lambda i: (i, 0)),
            pl.BlockSpec((1, d), lambda i: (0, 0)),
            pl.BlockSpec((None, None, 1, d), lambda i: (seg(i), k_shift, 0, 0)),
            pl.BlockSpec((None, None, 1, d), lambda i: (seg(i), k_scale, 0, 0)),
        ],
        out_specs=out_specs,
        out_shape=out_shape,
        compiler_params=_cparams("parallel"),
        name="norm_mod",
    )(xs, g.reshape(1, d), mod, mod)
    return res if packed else res[0]


def _final_norm_kernel(x_ref, g_ref, o_ref):
    x = x_ref[...]
    o_ref[...] = x * lax.rsqrt(jnp.mean(x * x, axis=-1, keepdims=True) + EPS) * g_ref[...]


def final_norm(xs, g, n_batch, tiles_per_row, ctx_tiles):
    t, d = xs.shape
    lat_tiles = tiles_per_row - ctx_tiles
    return pl.pallas_call(
        _final_norm_kernel,
        grid=(n_batch, lat_tiles),
        in_specs=[
            pl.BlockSpec((ROW_TILE, d), lambda b, i: (b * tiles_per_row + ctx_tiles + i, 0)),
            pl.BlockSpec((1, d), lambda b, i: (0, 0)),
        ],
        out_specs=pl.BlockSpec((ROW_TILE, d), lambda b, i: (b * lat_tiles + i, 0)),
        out_shape=jax.ShapeDtypeStruct((n_batch * lat_tiles * ROW_TILE, d), F32),
        compiler_params=_cparams("parallel", "parallel"),
        name="final_norm",
    )(xs, g.reshape(1, d))


def _mm_kernel(a_ref, w_ref, o_ref):
    o_ref[...] = jnp.dot(a_ref[...].astype(BF16), w_ref[...].astype(BF16),
                         preferred_element_type=F32).astype(o_ref.dtype)


def matmul(a, w, out_dtype, tm, tn, name):
    m, k = a.shape
    n = w.shape[1]
    return pl.pallas_call(
        _mm_kernel,
        grid=(n // tn, m // tm),
        in_specs=[pl.BlockSpec((tm, k), lambda j, i: (i, 0)),
                  pl.BlockSpec((k, tn), lambda j, i: (0, j))],
        out_specs=pl.BlockSpec((tm, tn), lambda j, i: (i, j)),
        out_shape=jax.ShapeDtypeStruct((m, n), out_dtype),
        compiler_params=_cparams("parallel", "parallel"),
        name=name,
    )(a, w)


def _out_proj_kernel(a1_ref, a2_ref, w1_ref, w2_ref, res_ref, gate_ref, o_ref):
    y = jnp.dot(a1_ref[...], w1_ref[...], preferred_element_type=F32)
    y += jnp.dot(a2_ref[...], w2_ref[...], preferred_element_type=F32)
    o_ref[...] = res_ref[...] + gate_ref[...] * y


def out_proj(a1, a2, w, xs, mod, k_gate, tiles_per_row, n_batch, tn=1024):
    t, d = xs.shape
    k1, k2 = a1.shape[1], a2.shape[1]
    assert k1 == k2
    seg = functools.partial(_seg_of_tile, tiles_per_row=tiles_per_row, n_batch=n_batch)
    return pl.pallas_call(
        _out_proj_kernel,
        grid=(d // tn, t // ROW_TILE),
        in_specs=[
            pl.BlockSpec((ROW_TILE, k1), lambda j, i: (i, 0)),
            pl.BlockSpec((ROW_TILE, k2), lambda j, i: (i, 0)),
            pl.BlockSpec((k1, tn), lambda j, i: (0, j)),
            pl.BlockSpec((k2, tn), lambda j, i: (1, j)),
            pl.BlockSpec((ROW_TILE, tn), lambda j, i: (i, j)),
            pl.BlockSpec((None, None, 1, tn), lambda j, i: (seg(i), k_gate, 0, j)),
        ],
        out_specs=pl.BlockSpec((ROW_TILE, tn), lambda j, i: (i, j)),
        out_shape=jax.ShapeDtypeStruct((t, d), F32),
        compiler_params=_cparams("parallel", "parallel"),
        name="out_proj",
    )(a1, a2, w, w, xs, mod)


def _post_kernel(of_ref, ob_ref, gate_ref, ng_ref, o_ref, *, head_dim):
    o = of_ref[...] + ob_ref[...]
    g = gate_ref[...].astype(F32)
    width = o.shape[1]
    for h in range(width // head_dim):
        sl = slice(h * head_dim, (h + 1) * head_dim)
        oh = o[:, sl]
        y = oh * lax.rsqrt(jnp.mean(oh * oh, axis=-1, keepdims=True) + EPS) * ng_ref[...]
        o_ref[:, sl] = (y * _silu(g[:, sl])).astype(o_ref.dtype)


def post_norm_gate(o_f, o_b, z, gate_col_block, head_dim, norm_g):
    t, width = o_f.shape
    return pl.pallas_call(
        functools.partial(_post_kernel, head_dim=head_dim),
        grid=(t // ROW_TILE,),
        in_specs=[
            pl.BlockSpec((ROW_TILE, width), lambda i: (i, 0)),
            pl.BlockSpec((ROW_TILE, width), lambda i: (i, 0)),
            pl.BlockSpec((ROW_TILE, width), lambda i: (i, gate_col_block)),
            pl.BlockSpec((1, head_dim), lambda i: (0, 0)),
        ],
        out_specs=pl.BlockSpec((ROW_TILE, width), lambda i: (i, 0)),
        out_shape=jax.ShapeDtypeStruct((t, width), BF16),
        compiler_params=_cparams("parallel"),
        name="post_norm_gate",
    )(o_f, o_b, z, norm_g.reshape(1, head_dim).astype(F32))


def _bwd_chunk(j, ctx_chunks, n_chunks):
    return jnp.where(j < ctx_chunks, ctx_chunks - 1 - j, n_chunks + ctx_chunks - 1 - j)


def _s5_direction_tables(lam_re, lam_im, log_dt, b_re, b_im, c_re, c_im, reverse):
    hi = lax.Precision.HIGHEST
    ln = S5_L
    lam_re, lam_im, b_re, b_im, c_re, c_im = (p.astype(F32) for p in (lam_re, lam_im, b_re, b_im, c_re, c_im))
    dt = jnp.exp(log_dt.astype(F32))[:, None]
    mag = jnp.exp(lam_re * dt)
    ab_re, ab_im = mag * jnp.cos(lam_im * dt), mag * jnp.sin(lam_im * dt)
    den = lam_re * lam_re + lam_im * lam_im
    fr = ((ab_re - 1) * lam_re + ab_im * lam_im) / den
    fi = (ab_im * lam_re - (ab_re - 1) * lam_im) / den
    bb_re = fr[..., None] * b_re - fi[..., None] * b_im
    bb_im = fr[..., None] * b_im + fi[..., None] * b_re
    tau = jnp.arange(ln + 1, dtype=F32)[:, None, None]
    pw = jnp.exp(tau * (lam_re * dt))
    pr, pi = pw * jnp.cos(tau * (lam_im * dt)), pw * jnp.sin(tau * (lam_im * dt))
    abr = pr[..., None] * bb_re - pi[..., None] * bb_im
    abi = pr[..., None] * bb_im + pi[..., None] * bb_re
    kern = (jnp.einsum('ghp,tgpk->tghk', c_re, abr[:ln], precision=hi)
            - jnp.einsum('ghp,tgpk->tghk', c_im, abi[:ln], precision=hi))
    pos = jnp.arange(ln)
    lag = (pos[None, :] - pos[:, None]) if not reverse else (pos[:, None] - pos[None, :])
    toe = jnp.where((lag >= 0)[:, :, None, None, None], kern[jnp.clip(lag, 0, ln - 1)], 0.0)
    g = lam_re.shape[0]
    intra = jnp.transpose(toe, (2, 0, 4, 1, 3)).reshape(g, ln * S5_H, ln * S5_H)
    pw_in = (ln - 1 - pos) if not reverse else pos
    inj = jnp.concatenate([abr[pw_in], abi[pw_in]], axis=2)
    inject = jnp.transpose(inj, (1, 0, 3, 2)).reshape(g, ln * S5_H, 2 * S5_P)
    pw_out = (pos + 1) if not reverse else (ln - pos)
    w_re = c_re[None] * pr[pw_out][:, :, None, :] - c_im[None] * pi[pw_out][:, :, None, :]
    w_im = -(c_re[None] * pi[pw_out][:, :, None, :] + c_im[None] * pr[pw_out][:, :, None, :])
    readout = jnp.transpose(jnp.concatenate([w_re, w_im], axis=3), (1, 3, 0, 2)).reshape(g, 2 * S5_P, ln * S5_H)
    decay = jnp.stack([pr[ln], pi[ln]])
    return intra, inject, readout, decay


def _s5_in_kernel(u_ref, w_ref, yi_ref, s_ref):
    r = jnp.dot(u_ref[...], w_ref[...], preferred_element_type=F32)
    n_intra = yi_ref.shape[-1]
    yi_ref[...] = r[:, :n_intra]
    s_ref[...] = r[:, n_intra:]


def _s5_scan_kernel(a_ref, sf_ref, sb_ref, of_ref, ob_ref, st_ref):
    @pl.when(pl.program_id(1) == 0)
    def _():
        st_ref[...] = jnp.zeros_like(st_ref)

    steps = sf_ref.shape[0]
    for d, (s_ref, o_ref) in enumerate(((sf_ref, of_ref), (sb_ref, ob_ref))):
        ar, ai = a_ref[2 * d], a_ref[2 * d + 1]
        sr, si = st_ref[2 * d], st_ref[2 * d + 1]
        for q in range(steps):
            r = q if d == 0 else steps - 1 - q
            o_ref[r, 0] = sr
            o_ref[r, 1] = si
            sr, si = (ar * sr - ai * si + s_ref[r, 0], ar * si + ai * sr + s_ref[r, 1])
        st_ref[2 * d] = sr
        st_ref[2 * d + 1] = si


def _s5_out_kernel(yi_ref, st_ref, w_ref, u_ref, d_ref, o_ref):
    y = yi_ref[...] + jnp.dot(st_ref[...].astype(BF16), w_ref[...], preferred_element_type=F32)
    o_ref[...] = y + d_ref[...] * u_ref[...].astype(F32)


def _glu_kernel(y_ref, w_ref, o_ref):
    y = jax.nn.gelu(y_ref[...])
    o_ref[...] = (y * jax.nn.sigmoid(jnp.dot(y.astype(BF16), w_ref[...], preferred_element_type=F32))
                  ).astype(o_ref.dtype)


def mixer_s5(z, n_batch, n_tok, n_ctx, lam_re, lam_im, log_dt, b_re, b_im, c_re, c_im, d_skip, w_glu):
    t = z.shape[0]
    g = lam_re.shape[1]
    width = g * S5_H
    lh = S5_L * S5_H
    rows = t // S5_L
    tabs = [_s5_direction_tables(lam_re[d], lam_im[d], log_dt[d], b_re[d], b_im[d], c_re[d], c_im[d], d == 1)
            for d in range(2)]
    w_in = jnp.concatenate([tabs[0][0] + tabs[1][0], tabs[0][1], tabs[1][1]], axis=2).astype(BF16)
    w_st = jnp.concatenate([tabs[0][2], tabs[1][2]], axis=1).astype(BF16)
    decay = jnp.concatenate([tabs[0][3], tabs[1][3]], axis=0).reshape(4, g * S5_P // 128, 128)
    u = z[:, :width].reshape(rows, S5_L, g, S5_H)
    u_g = jnp.transpose(u, (2, 0, 1, 3)).reshape(g, rows, lh).astype(BF16)

    n_st = 4 * S5_P
    yi, s_in = pl.pallas_call(
        _s5_in_kernel,
        grid=(g,),
        in_specs=[pl.BlockSpec((None, rows, lh), lambda i: (i, 0, 0)),
                  pl.BlockSpec((None, lh, lh + n_st), lambda i: (i, 0, 0))],
        out_specs=[pl.BlockSpec((None, rows, lh), lambda i: (i, 0, 0)),
                   pl.BlockSpec((None, rows, n_st), lambda i: (i, 0, 0))],
        out_shape=[jax.ShapeDtypeStruct((g, rows, lh), F32), jax.ShapeDtypeStruct((g, rows, n_st), F32)],
        compiler_params=_cparams("parallel"),
        name="s5_in",
    )(u_g, w_in)

    gp = g * S5_P // 128
    s_scan = jnp.transpose(s_in.reshape(g, rows, 4, S5_P), (1, 2, 0, 3)).reshape(rows, 4, gp, 128)
    blocks_per_row = n_tok // S5_L
    cb = n_ctx // S5_L
    assert blocks_per_row % cb == 0
    nblk = blocks_per_row // cb
    bwd = lambda j: jnp.where(j == 0, 0, nblk - j)
    st_f, st_b = pl.pallas_call(
        _s5_scan_kernel,
        grid=(n_batch, nblk),
        in_specs=[pl.BlockSpec((4, gp, 128), lambda b, j: (0, 0, 0)),
                  pl.BlockSpec((cb, 2, gp, 128), lambda b, j: (b * nblk + j, 0, 0, 0)),
                  pl.BlockSpec((cb, 2, gp, 128), lambda b, j: (b * nblk + bwd(j), 1, 0, 0))],
        out_specs=[pl.BlockSpec((cb, 2, gp, 128), lambda b, j: (b * nblk + j, 0, 0, 0)),
                   pl.BlockSpec((cb, 2, gp, 128), lambda b, j: (b * nblk + bwd(j), 0, 0, 0))],
        out_shape=[jax.ShapeDtypeStruct((rows, 2, gp, 128), F32)] * 2,
        scratch_shapes=[pltpu.VMEM((4, gp, 128), F32)],
        compiler_params=_cparams("parallel", "arbitrary"),
        name="s5_scan",
    )(decay, s_scan, s_scan)
    st = jnp.concatenate([st_f, st_b], axis=1).reshape(rows, 4, g, S5_P)
    st_g = jnp.transpose(st, (2, 0, 1, 3)).reshape(g, rows, n_st)

    d_vec = jnp.tile(d_skip.astype(F32).reshape(g, 1, S5_H), (1, S5_L, 1)).reshape(g, 1, lh)
    y_g = pl.pallas_call(
        _s5_out_kernel,
        grid=(g,),
        in_specs=[pl.BlockSpec((None, rows, lh), lambda i: (i, 0, 0)),
                  pl.BlockSpec((None, rows, n_st), lambda i: (i, 0, 0)),
                  pl.BlockSpec((None, n_st, lh), lambda i: (i, 0, 0)),
                  pl.BlockSpec((None, rows, lh), lambda i: (i, 0, 0)),
                  pl.BlockSpec((None, 1, lh), lambda i: (i, 0, 0))],
        out_specs=pl.BlockSpec((None, rows, lh), lambda i: (i, 0, 0)),
        out_shape=jax.ShapeDtypeStruct((g, rows, lh), F32),
        compiler_params=_cparams("parallel"),
        name="s5_out",
    )(yi, st_g, w_st, u_g, d_vec)
    y = jnp.transpose(y_g.reshape(g, rows, S5_L, S5_H), (1, 2, 0, 3)).reshape(t, width)

    return pl.pallas_call(
        _glu_kernel,
        grid=(t // ROW_TILE,),
        in_specs=[pl.BlockSpec((ROW_TILE, width), lambda i: (i, 0)),
                  pl.BlockSpec((width, width), lambda i: (0, 0))],
        out_specs=pl.BlockSpec((ROW_TILE, width), lambda i: (i, 0)),
        out_shape=jax.ShapeDtypeStruct((t, width), BF16),
        compiler_params=_cparams("parallel"),
        name="s5_glu",
    )(y, w_glu.astype(BF16))


def _gla_kernel(lb_ref, qf_ref, ff_ref, vf_ref, qb_ref, fb_ref, vb_ref, of_ref, ob_ref, st_ref, *, n_heads, hd):
    @pl.when(pl.program_id(1) == 0)
    def _():
        st_ref[...] = jnp.zeros_like(st_ref)

    c = qf_ref.shape[0]
    items = []
    for d, (q_ref, f_ref, v_ref, o_ref) in enumerate(((qf_ref, ff_ref, vf_ref, of_ref),
                                                       (qb_ref, fb_ref, vb_ref, ob_ref))):
        lower = d == 0
        mask = _tri_mask(c, lower)
        tri = mask.astype(BF16)
        lb = lb_ref[d]
        sig = jax.nn.sigmoid(f_ref[...])
        kk = (1.0 - lb) * (1.0 - sig)
        logf = jnp.log(lb + (1.0 - lb) * sig)
        b = _tri_dot(tri, logf)
        b_end = b[c - 1:c, :] if lower else b[0:1, :]
        qt = (_silu(q_ref[...]) * jnp.exp(b)).astype(BF16)
        kt = (kk * jnp.exp(jnp.minimum(-b, EXP_CLAMP))).astype(BF16)
        kend = (kk * jnp.exp(b_end - b)).astype(BF16)
        dec = jnp.exp(b_end)
        v = v_ref[...].astype(BF16)
        for h in range(n_heads):
            sl = slice(h * hd, (h + 1) * hd)
            items.append((d, h, sl, o_ref, mask, qt[:, sl], kt[:, sl], kend[:, sl], v[:, sl], dec[:, sl], st_ref[d, h]))
    att = [_dot_nt(it[5], it[6]) for it in items]
    inter = [_dot_nt(it[5], it[10]) for it in items]
    upd = [_dot_tn(it[8], it[7]) for it in items]
    intra = [_dot(jnp.where(it[4], a, 0.0), it[8]) for it, a in zip(items, att)]
    for it, oi, os, up in zip(items, intra, inter, upd):
        d, h, sl, o_ref = it[:4]
        o_ref[:, sl] = oi + os
        st_ref[d, h] = it[9] * it[10] + up


def mixer_gla(z, n_batch, n_tok, n_ctx, col0, lb, n_heads, hd):
    width = n_heads * hd
    z3 = z.reshape(n_batch, n_tok, z.shape[1])
    nc, cc = n_tok // CHUNK, n_ctx // CHUNK
    cb0 = col0 // width
    bwd = functools.partial(_bwd_chunk, ctx_chunks=cc, n_chunks=nc)
    blk = (None, CHUNK, width)
    fw = lambda k: pl.BlockSpec(blk, lambda b, j: (b, j, cb0 + k))
    bw = lambda k: pl.BlockSpec(blk, lambda b, j: (b, bwd(j), cb0 + k))
    o_f, o_b = pl.pallas_call(
        functools.partial(_gla_kernel, n_heads=n_heads, hd=hd),
        grid=(n_batch, nc),
        in_specs=[pl.BlockSpec((2, 1, width), lambda b, j: (0, 0, 0)),
                  fw(0), fw(1), fw(3), bw(0), bw(2), bw(3)],
        out_specs=[pl.BlockSpec(blk, lambda b, j: (b, j, 0)),
                   pl.BlockSpec(blk, lambda b, j: (b, bwd(j), 0))],
        out_shape=[jax.ShapeDtypeStruct((n_batch, n_tok, width), F32)] * 2,
        scratch_shapes=[pltpu.VMEM((2, n_heads, hd, hd), F32)],
        compiler_params=_cparams("parallel", "arbitrary"),
        name="gla_scan",
    )(lb.reshape(2, 1, width).astype(F32), z3, z3, z3, z3, z3, z3)
    return o_f.reshape(-1, width), o_b.reshape(-1, width)


def _rotary_tables(n_tok, n_ctx):
    n_lat = n_tok - n_ctx
    rows = n_lat // GRID_W
    row = jnp.repeat(jnp.arange(rows, dtype=F32), GRID_W)
    col = jnp.tile(jnp.arange(GRID_W, dtype=F32), rows)
    n_freq = C_QK // 4
    inv = ROPE_BASE ** (-jnp.arange(n_freq, dtype=F32) / n_freq)
    ang = jnp.concatenate([row[:, None] * inv, col[:, None] * inv], axis=-1)
    ang = jnp.concatenate([jnp.zeros((n_ctx, C_QK // 2), F32), ang], axis=0)
    cos, sin = jnp.cos(ang), jnp.sin(ang)
    return jnp.concatenate([cos, cos], axis=-1), jnp.concatenate([-sin, sin], axis=-1)


def _ret_kernel(cdec_ref, dmat_ref, qdec_ref, kdec_ref,
                qf_ref, kf_ref, vf_ref, cf_ref, sf_ref, qb_ref, kb_ref, vb_ref, cb_ref, sb_ref,
                of_ref, ob_ref, st_ref, *, n_heads):
    @pl.when(pl.program_id(1) == 0)
    def _():
        st_ref[...] = jnp.zeros_like(st_ref)

    half = C_QK // 2
    items = []
    for d, (q_ref, k_ref, v_ref, cos_ref, sin_ref, o_ref) in enumerate(
            ((qf_ref, kf_ref, vf_ref, cf_ref, sf_ref, of_ref), (qb_ref, kb_ref, vb_ref, cb_ref, sb_ref, ob_ref))):
        cos, sin = cos_ref[...], sin_ref[...]
        for h in range(n_heads):
            qs = slice(h * C_QK, (h + 1) * C_QK)
            vs = slice(h * C_V, (h + 1) * C_V)
            qh, kh = q_ref[:, qs].astype(F32), k_ref[:, qs].astype(F32)
            qh = (qh * cos + pltpu.roll(qh, half, axis=1) * sin) * (C_QK ** -0.5)
            kh = kh * cos + pltpu.roll(kh, half, axis=1) * sin
            items.append((d, h, vs, o_ref, qh.astype(BF16), kh.astype(BF16), (qh * qdec_ref[d, h]).astype(BF16),
                          (kh * kdec_ref[d, h]).astype(BF16), v_ref[:, vs].astype(BF16), st_ref[d, h]))
    att = [_dot_nt(it[4], it[5]) for it in items]
    inter = [_dot(it[6], it[9]) for it in items]
    upd = [_dot_tn(it[7], it[8]) for it in items]
    intra = [_dot(a * dmat_ref[it[0], it[1]], it[8]) for it, a in zip(items, att)]
    for it, oi, os, up in zip(items, intra, inter, upd):
        d, h, vs, o_ref = it[:4]
        o_ref[:, vs] = oi + os
        st_ref[d, h] = cdec_ref[d, h] * it[9] + up


def mixer_retention(z, n_batch, n_tok, n_ctx, decay_logit, n_heads):
    qw, vw = n_heads * C_QK, n_heads * C_V
    z3 = z.reshape(n_batch, n_tok, z.shape[1])
    nc, cc = n_tok // CHUNK, n_ctx // CHUNK
    bwd = functools.partial(_bwd_chunk, ctx_chunks=cc, n_chunks=nc)
    log_gamma = jax.nn.log_sigmoid(decay_logit.astype(F32))[:, :, None, None]
    idx = jnp.arange(CHUNK, dtype=F32)
    diff = idx[:, None] - idx[None, :]
    dmat_f = jnp.where(diff >= 0, jnp.exp(jnp.maximum(diff, 0.0) * log_gamma[0]), 0.0)
    dmat_b = jnp.where(diff <= 0, jnp.exp(jnp.maximum(-diff, 0.0) * log_gamma[1]), 0.0)
    dmat = jnp.stack([dmat_f, dmat_b])
    ones = jnp.ones((1, 1, 1, C_QK), F32)
    pos_f, pos_b = idx[None, None, :, None], (CHUNK - 1 - idx)[None, None, :, None]
    lg = log_gamma
    qdec = jnp.concatenate([jnp.exp((pos_f + 1) * lg[0:1]), jnp.exp((pos_b + 1) * lg[1:2])]) * ones
    kdec = jnp.concatenate([jnp.exp((CHUNK - 1 - pos_f) * lg[0:1]), jnp.exp((CHUNK - 1 - pos_b) * lg[1:2])]) * ones
    cdec = jnp.exp(CHUNK * log_gamma[:, :, 0, 0])
    cos2, sin2 = _rotary_tables(n_tok, n_ctx)
    full = lambda shape: pl.BlockSpec(shape, lambda b, j: (0,) * len(shape))
    tab = lambda order: pl.BlockSpec((CHUNK, C_QK), lambda b, j: (order(j), 0))
    ident = lambda j: j
    vcb = 2 * qw // vw
    assert vcb * vw == 2 * qw
    def specs(order):
        return [pl.BlockSpec((None, CHUNK, qw), lambda b, j: (b, order(j), 0)),
                pl.BlockSpec((None, CHUNK, qw), lambda b, j: (b, order(j), 1)),
                pl.BlockSpec((None, CHUNK, vw), lambda b, j: (b, order(j), vcb)),
                tab(order), tab(order)]
    o_f, o_b = pl.pallas_call(
        functools.partial(_ret_kernel, n_heads=n_heads),
        grid=(n_batch, nc),
        in_specs=[pl.BlockSpec(memory_space=pltpu.SMEM), full((2, n_heads, CHUNK, CHUNK)),
                  full((2, n_heads, CHUNK, C_QK)), full((2, n_heads, CHUNK, C_QK))] + specs(ident) + specs(bwd),
        out_specs=[pl.BlockSpec((None, CHUNK, vw), lambda b, j: (b, j, 0)),
                   pl.BlockSpec((None, CHUNK, vw), lambda b, j: (b, bwd(j), 0))],
        out_shape=[jax.ShapeDtypeStruct((n_batch, n_tok, vw), F32)] * 2,
        scratch_shapes=[pltpu.VMEM((2, n_heads, C_QK, C_V), F32)],
        compiler_params=_cparams("parallel", "arbitrary"),
        name="retention_scan",
    )(cdec, dmat, qdec, kdec, z3, z3, z3, cos2, sin2, z3, z3, z3, cos2, sin2)
    return o_f.reshape(-1, vw), o_b.reshape(-1, vw)


def _gdn_conv_kernel(prev_ref, cur_ref, next_ref, w_ref, o_ref, *, tiles_per_row, ctx_tiles, n_heads, hd):
    i = pl.program_id(0)
    r = i % tiles_per_row
    first = jnp.logical_or(r == 0, r == ctx_tiles)
    last = jnp.logical_or(r == ctx_tiles - 1, r == tiles_per_row - 1)
    x = cur_ref[...].astype(F32)
    rows = x.shape[0]
    rid = lax.broadcasted_iota(jnp.int32, x.shape, 0)
    hp = prev_ref.shape[0]
    x_prev = jnp.where(first, 0.0, prev_ref[hp - 1:hp, :].astype(F32))
    x_next = jnp.where(last, 0.0, next_ref[0:1, :].astype(F32))
    left = jnp.where(rid == 0, x_prev, pltpu.roll(x, 1, axis=0))
    right = jnp.where(rid == rows - 1, x_next, pltpu.roll(x, rows - 1, axis=0))
    w = w_ref[...]
    y = _silu(left * w[0:1, :] + x * w[1:2, :] + right * w[2:3, :])
    width = n_heads * hd
    for h in range(3 * n_heads):
        sl = slice(h * hd, (h + 1) * hd)
        yh = y[:, sl]
        if h < 2 * n_heads:
            yh = yh * lax.rsqrt(jnp.sum(yh * yh, axis=-1, keepdims=True) + EPS)
            if h < n_heads:
                yh = yh * (hd ** -0.5)
        o_ref[:, sl] = yh.astype(o_ref.dtype)


def _gdn_kernel(qf_ref, kf_ref, vf_ref, cf_ref, rf_ref, qb_ref, kb_ref, vb_ref, cb_ref, rb_ref,
                of_ref, ob_ref, st_ref, *, n_heads, hd):
    @pl.when(pl.program_id(1) == 0)
    def _():
        st_ref[...] = jnp.zeros_like(st_ref)

    c = qf_ref.shape[0]
    eye = (lax.broadcasted_iota(jnp.int32, (c, c), 0) == lax.broadcasted_iota(jnp.int32, (c, c), 1)).astype(F32)
    items = []
    for d, (q_ref, k_ref, v_ref, col_ref, row_ref, o_ref) in enumerate(
            ((qf_ref, kf_ref, vf_ref, cf_ref, rf_ref, of_ref), (qb_ref, kb_ref, vb_ref, cb_ref, rb_ref, ob_ref))):
        lower = d == 0
        incl = _tri_mask(c, lower)
        strict = jnp.logical_and(incl, eye == 0.0)
        col = col_ref[...]
        row = row_ref[...]
        b_cols = _tri_dot(incl.astype(BF16), col)
        b_rows = _dot_tri(row, _tri_mask(c, not lower).astype(BF16))
        for h in range(n_heads):
            sl = slice(h * hd, (h + 1) * hd)
            ib, ig = d * n_heads + h, (2 + d) * n_heads + h
            be = col[:, ib:ib + 1]
            bc = b_cols[:, ig:ig + 1]
            br = b_rows[ig:ig + 1, :]
            b_end = bc[c - 1:c, :] if lower else bc[0:1, :]
            gam_i = jnp.exp(jnp.where(incl, bc - br, -1e30))
            gam_s = jnp.where(strict, gam_i, 0.0)
            qh, kh, vh = q_ref[:, sl], k_ref[:, sl], v_ref[:, sl]
            eb = jnp.exp(bc)
            rhs = jnp.concatenate([be * vh.astype(F32), (be * eb) * kh.astype(F32)], axis=1).astype(BF16)
            items.append(dict(d=d, h=h, sl=sl, o_ref=o_ref, be=be, gam_i=gam_i, gam_s=gam_s, q=qh, k=kh, rhs=rhs,
                              qe=(qh.astype(F32) * eb).astype(BF16), dec=jnp.exp(b_end),
                              ke=(kh.astype(F32) * jnp.exp(b_end - bc)).astype(BF16), s=st_ref[d, h]))
    kk = [_dot_nt(it["k"], it["k"]) for it in items]
    qk = [_dot_nt(it["q"], it["k"]) for it in items]
    qs = [_dot(it["qe"], it["s"]) for it in items]
    ns = [it["be"] * a * it["gam_s"] for it, a in zip(items, kk)]
    invs = [eye - n for n in ns]
    ps = [_dot(n, n) for n in ns]
    levels = int(np.log2(c)) - 1
    for lvl in range(levels):
        prods = [_dot(inv, p) for inv, p in zip(invs, ps)]
        if lvl < levels - 1:
            ps = [_dot(p, p) for p in ps]
        invs = [inv + pr for inv, pr in zip(invs, prods)]
    sols = [_dot(inv, it["rhs"]) for inv, it in zip(invs, items)]
    sks = [_dot(sol[:, hd:], it["s"]) for sol, it in zip(sols, items)]
    us = [sol[:, :hd] - sk for sol, sk in zip(sols, sks)]
    intra = [_dot(a * it["gam_i"], u) for a, it, u in zip(qk, items, us)]
    upd = [_dot_tn(it["ke"], u) for it, u in zip(items, us)]
    for it, oi, os, up in zip(items, intra, qs, upd):
        it["o_ref"][:, it["sl"]] = oi + os
        st_ref[it["d"], it["h"]] = it["dec"] * it["s"] + up


def mixer_gdn(z, n_batch, n_tok, n_ctx, col0, conv_w, a_log, dt_bias, n_heads, hd):
    t, zw = z.shape
    width = n_heads * hd
    tiles_per_row, ctx_tiles = n_tok // ROW_TILE, n_ctx // ROW_TILE
    cb0 = col0 // (3 * width)
    assert cb0 * 3 * width == col0
    halo = 8
    hpt = ROW_TILE // halo
    n_halo = t // halo
    qkv = pl.pallas_call(
        functools.partial(_gdn_conv_kernel, tiles_per_row=tiles_per_row, ctx_tiles=ctx_tiles, n_heads=n_heads, hd=hd),
        grid=(t // ROW_TILE,),
        in_specs=[pl.BlockSpec((halo, 3 * width), lambda i: (jnp.maximum(i * hpt - 1, 0), cb0)),
                  pl.BlockSpec((ROW_TILE, 3 * width), lambda i: (i, cb0)),
                  pl.BlockSpec((halo, 3 * width), lambda i: (jnp.minimum((i + 1) * hpt, n_halo - 1), cb0)),
                  pl.BlockSpec((3, 3 * width), lambda i: (0, 0))],
        out_specs=pl.BlockSpec((ROW_TILE, 3 * width), lambda i: (i, 0)),
        out_shape=jax.ShapeDtypeStruct((t, 3 * width), BF16),
        compiler_params=_cparams("parallel"),
        name="gdn_conv",
    )(z, z, z, conv_w.astype(F32))

    sc0 = col0 + 4 * width
    small = z[:, sc0:sc0 + 4 * n_heads].astype(F32)
    a_log, dt_bias = a_log.astype(F32), dt_bias.astype(F32)
    be = jax.nn.sigmoid(small[:, :2 * n_heads])
    la_f = -jnp.exp(a_log[0]) * jax.nn.softplus(small[:, 2 * n_heads:3 * n_heads] + dt_bias[0])
    la_b = -jnp.exp(a_log[1]) * jax.nn.softplus(small[:, 3 * n_heads:] + dt_bias[1])
    cols = jnp.concatenate([be, la_f, la_b], axis=1)
    nc, cc = n_tok // CHUNK, n_ctx // CHUNK
    cols3 = cols.reshape(n_batch, n_tok, 4 * n_heads)
    rows4 = jnp.transpose(cols.reshape(n_batch, nc, CHUNK, 4 * n_heads), (0, 1, 3, 2))
    qkv3 = qkv.reshape(n_batch, n_tok, 3 * width)
    bwd = functools.partial(_bwd_chunk, ctx_chunks=cc, n_chunks=nc)
    ident = lambda j: j
    def specs(order):
        return [pl.BlockSpec((None, CHUNK, width), lambda b, j: (b, order(j), 0)),
                pl.BlockSpec((None, CHUNK, width), lambda b, j: (b, order(j), 1)),
                pl.BlockSpec((None, CHUNK, width), lambda b, j: (b, order(j), 2)),
                pl.BlockSpec((None, CHUNK, 4 * n_heads), lambda b, j: (b, order(j), 0)),
                pl.BlockSpec((None, None, 4 * n_heads, CHUNK), lambda b, j: (b, order(j), 0, 0))]
    o_f, o_b = pl.pallas_call(
        functools.partial(_gdn_kernel, n_heads=n_heads, hd=hd),
        grid=(n_batch, nc),
        in_specs=specs(ident) + specs(bwd),
        out_specs=[pl.BlockSpec((None, CHUNK, width), lambda b, j: (b, j, 0)),
                   pl.BlockSpec((None, CHUNK, width), lambda b, j: (b, bwd(j), 0))],
        out_shape=[jax.ShapeDtypeStruct((n_batch, n_tok, width), F32)] * 2,
        scratch_shapes=[pltpu.VMEM((2, n_heads, hd, hd), F32)],
        compiler_params=_cparams("parallel", "arbitrary"),
        name="gdn_scan",
    )(qkv3, qkv3, qkv3, cols3, rows4, qkv3, qkv3, qkv3, cols3, rows4)
    return o_f.reshape(-1, width), o_b.reshape(-1, width)


def _rank_kernel(e_ref, rank_ref, cnt_ref, carry_ref):
    @pl.when(pl.program_id(0) == 0)
    def _():
        carry_ref[...] = jnp.zeros_like(carry_ref)

    e = e_ref[...]
    rows = e.shape[0]
    lane = lax.broadcasted_iota(jnp.int32, (rows, 128), 1)
    m = jnp.zeros((rows, 128), F32)
    for k in range(TOP_K):
        m = m + (lane == e[:, k:k + 1]).astype(F32)
    r = lax.broadcasted_iota(jnp.int32, (rows, rows), 0)
    c = lax.broadcasted_iota(jnp.int32, (rows, rows), 1)
    before = jnp.dot((r > c).astype(BF16), m.astype(BF16), preferred_element_type=F32)
    rank_ref[...] = before + carry_ref[...]
    carry_ref[...] += jnp.sum(m, axis=0, keepdims=True)
    cnt_ref[...] = carry_ref[...]


def _slot_fetch(slots_hbm, idx_smem, sem_idx, tile, buf):
    return pltpu.make_async_copy(slots_hbm.at[tile], idx_smem.at[buf], sem_idx.at[buf])


def _dispatch_kernel(padded_ref, pend_ref, nu_ref, slots_hbm, hp_ref, zeros_hbm, x_hbm, idx_smem, sem_idx, sem_rows):
    i = pl.program_id(0)
    rows = hp_ref.shape[0]

    @pl.when(i == 0)
    def _():
        _slot_fetch(slots_hbm, idx_smem, sem_idx, 0, 0).start()

        def zero_tail(e):
            return pltpu.make_async_copy(zeros_hbm, x_hbm.at[pl.ds(pend_ref[e] - MOE_BLOCK, MOE_BLOCK)], sem_rows)

        def zstart(e, carry):
            @pl.when(padded_ref[e] > 0)
            def _():
                zero_tail(e).start()
            return carry

        def zwait(e, carry):
            @pl.when(padded_ref[e] > 0)
            def _():
                zero_tail(e).wait()
            return carry

        lax.fori_loop(0, N_EXPERTS, zstart, 0)
        lax.fori_loop(0, N_EXPERTS, zwait, 0)

        def unused_block(j):
            return pltpu.make_async_copy(zeros_hbm, x_hbm.at[pl.ds(j * MOE_BLOCK, MOE_BLOCK)], sem_rows)

        n_blocks = x_hbm.shape[0] // MOE_BLOCK
        lax.fori_loop(nu_ref[0], n_blocks, lambda j, carry: (unused_block(j).start(), carry)[1], 0)
        lax.fori_loop(nu_ref[0], n_blocks, lambda j, carry: (unused_block(j).wait(), carry)[1], 0)

    buf = i % 2
    _slot_fetch(slots_hbm, idx_smem, sem_idx, i, buf).wait()

    @pl.when(i + 1 < pl.num_programs(0))
    def _():
        _slot_fetch(slots_hbm, idx_smem, sem_idx, i + 1, 1 - buf).start()

    def body(r, carry):
        for k in range(TOP_K):
            pltpu.make_async_copy(hp_ref.at[pl.ds(r, 1)], x_hbm.at[pl.ds(idx_smem[buf, k, r], 1)],
                                  sem_rows).start(priority=k % 2)
        return carry

    lax.fori_loop(0, rows, body, 0, unroll=8)
    span = x_hbm.at[pl.ds(0, rows * TOP_K)]
    pltpu.make_async_copy(span, span, sem_rows).wait()


def _expert_kernel(be_ref, nu_ref, x_ref, wg_ref, wu_ref, wd_ref, o_ref, wg_s, wu_s, wd_s):
    i = pl.program_id(0)
    live = i < nu_ref[0]
    new_expert = jnp.logical_or(i == 0, be_ref[i] != be_ref[jnp.maximum(i - 1, 0)])

    @pl.when(jnp.logical_and(live, new_expert))
    def _():
        wg_s[...] = wg_ref[...].astype(BF16)
        wu_s[...] = wu_ref[...].astype(BF16)
        wd_s[...] = wd_ref[...].astype(BF16)

    @pl.when(live)
    def _():
        lo, hi = _unpack_halves(_load_rows(x_ref, 0, x_ref.shape[0]))
        lo, hi = lo.astype(BF16), hi.astype(BF16)
        half = lo.shape[1]
        g = (jnp.dot(lo, wg_s[:half, :], preferred_element_type=F32)
             + jnp.dot(hi, wg_s[half:, :], preferred_element_type=F32))
        u = (jnp.dot(lo, wu_s[:half, :], preferred_element_type=F32)
             + jnp.dot(hi, wu_s[half:, :], preferred_element_type=F32))
        a = (_silu(g) * u).astype(BF16)
        _store_rows(o_ref, _pack_halves(jnp.dot(a, wd_s[...], preferred_element_type=F32)))

    @pl.when(jnp.logical_not(live))
    def _():
        o_ref[...] = jnp.zeros_like(o_ref)


def _combine_kernel(slots_hbm, y_hbm, h_ref, gate_ref, xs_ref, mg_ref, wg_ref, wu_ref, wd_ref, o_ref,
                    idx_smem, rows_buf, sem_idx, sem_rows):
    i = pl.program_id(0)
    rows = h_ref.shape[0]

    @pl.when(i == 0)
    def _():
        _slot_fetch(slots_hbm, idx_smem, sem_idx, 0, 0).start()

    buf = i % 2
    _slot_fetch(slots_hbm, idx_smem, sem_idx, i, buf).wait()

    @pl.when(i + 1 < pl.num_programs(0))
    def _():
        _slot_fetch(slots_hbm, idx_smem, sem_idx, i + 1, 1 - buf).start()

    def body(r, carry):
        for k in range(TOP_K):
            pltpu.make_async_copy(y_hbm.at[pl.ds(idx_smem[buf, k, r], 1)], rows_buf.at[pl.ds(k * rows + r, 1)],
                                  sem_rows).start(priority=k % 2)
        return carry

    lax.fori_loop(0, rows, body, 0, unroll=8)

    h = h_ref[...]
    g = jnp.dot(h, wg_ref[...], preferred_element_type=F32)
    u = jnp.dot(h, wu_ref[...], preferred_element_type=F32)
    a = (_silu(g) * u).astype(BF16)
    y = jnp.dot(a, wd_ref[...], preferred_element_type=F32)

    pltpu.make_async_copy(y_hbm.at[pl.ds(0, rows * TOP_K)], rows_buf, sem_rows).wait()
    half = y.shape[1] // 2
    acc_lo, acc_hi = y[:, :half], y[:, half:]
    gate = gate_ref[...]
    for k in range(TOP_K):
        lo, hi = _unpack_halves(_load_rows(rows_buf, k * rows, rows))
        acc_lo = acc_lo + gate[:, k:k + 1] * lo
        acc_hi = acc_hi + gate[:, k:k + 1] * hi
    mg = mg_ref[...]
    o_ref[:, :half] = xs_ref[:, :half] + mg[:, :half] * acc_lo
    o_ref[:, half:] = xs_ref[:, half:] + mg[:, half:] * acc_hi


def _route(logits, router_bias):
    n_tok = logits.shape[0]
    scores = jax.nn.sigmoid(logits)
    sel = (scores + router_bias.astype(F32)).reshape(n_tok, N_GROUPS, N_EXPERTS // N_GROUPS)
    grp_score = lax.top_k(sel, 2)[0].sum(-1)
    _, gidx = lax.top_k(grp_score, TOPK_GROUPS)
    gmask = jnp.any(gidx[..., None] == jnp.arange(N_GROUPS), axis=1)
    masked = jnp.where(gmask[..., None], sel, -jnp.inf).reshape(n_tok, N_EXPERTS)
    _, eidx = lax.top_k(masked, TOP_K)
    gate = jnp.take_along_axis(scores, eidx, axis=1)
    gate = gate / jnp.sum(gate, axis=-1, keepdims=True) * ROUTED_SCALE
    return eidx, gate


def moe_block(h2, h2_packed, xs, mod, k_gate, tiles_per_row, n_batch,
              w_router, router_bias, w_gate, w_up, w_down, ws_gate, ws_up, ws_down):
    t, d = h2.shape
    row_shape = h2_packed.shape[1:]
    de = ws_gate.shape[1]
    n_tiles = t // ROW_TILE
    wr = jnp.pad(w_router.astype(BF16), ((0, 0), (0, 128 - N_EXPERTS)))
    logits = matmul(h2, wr, F32, ROW_TILE, 128, "router")[:, :N_EXPERTS]
    eidx, gate = _route(logits, router_bias)
    eidx = eidx.astype(jnp.int32)

    rank_mat, cnt = pl.pallas_call(
        _rank_kernel,
        grid=(n_tiles,),
        in_specs=[pl.BlockSpec((ROW_TILE, TOP_K), lambda i: (i, 0))],
        out_specs=[pl.BlockSpec((ROW_TILE, 128), lambda i: (i, 0)), pl.BlockSpec((1, 128), lambda i: (0, 0))],
        out_shape=[jax.ShapeDtypeStruct((t, 128), F32), jax.ShapeDtypeStruct((1, 128), F32)],
        scratch_shapes=[pltpu.VMEM((1, 128), F32)],
        compiler_params=_cparams("arbitrary"),
        name="expert_rank",
    )(eidx)
    n_blocks = -(-t * TOP_K // MOE_BLOCK) + N_EXPERTS
    counts = cnt[0, :N_EXPERTS].astype(jnp.int32)
    padded = (counts + MOE_BLOCK - 1) // MOE_BLOCK * MOE_BLOCK
    pad_end = jnp.cumsum(padded).astype(jnp.int32)
    pad_start = pad_end - padded
    slot = pad_start[eidx] + jnp.take_along_axis(rank_mat, eidx, axis=1).astype(jnp.int32)
    slots = jnp.transpose(slot.reshape(n_tiles, ROW_TILE, TOP_K), (0, 2, 1))
    block_start = jnp.arange(n_blocks, dtype=jnp.int32) * MOE_BLOCK
    block_e = jnp.minimum(jnp.sum(pad_end[None, :] <= block_start[:, None], axis=1), N_EXPERTS - 1).astype(jnp.int32)
    n_used = (pad_end[-1] // MOE_BLOCK).astype(jnp.int32).reshape(1)

    idx_scratch = [pltpu.SMEM((2, TOP_K, ROW_TILE), jnp.int32)]
    x_sorted = pl.pallas_call(
        _dispatch_kernel,
        grid_spec=pltpu.PrefetchScalarGridSpec(
            num_scalar_prefetch=3,
            grid=(n_tiles,),
            in_specs=[pl.BlockSpec(memory_space=pl.ANY),
                      pl.BlockSpec((ROW_TILE,) + row_shape, lambda i, pd, pe, nu: (i, 0, 0)),
                      pl.BlockSpec(memory_space=pl.ANY)],
            out_specs=pl.BlockSpec(memory_space=pl.ANY),
            scratch_shapes=idx_scratch + [pltpu.SemaphoreType.DMA((2,)), pltpu.SemaphoreType.DMA(())],
        ),
        out_shape=jax.ShapeDtypeStruct((n_blocks * MOE_BLOCK,) + row_shape, jnp.uint32),
        compiler_params=_cparams("arbitrary"),
        name="moe_dispatch",
    )(padded, pad_end, n_used, slots, h2_packed, jnp.zeros((MOE_BLOCK,) + row_shape, jnp.uint32))

    y_sorted = pl.pallas_call(
        _expert_kernel,
        grid_spec=pltpu.PrefetchScalarGridSpec(
            num_scalar_prefetch=2,
            grid=(n_blocks,),
            in_specs=[pl.BlockSpec((MOE_BLOCK,) + row_shape, lambda i, be, nu: (jnp.minimum(i, nu[0] - 1), 0, 0)),
                      pl.BlockSpec((None, d, de), lambda i, be, nu: (be[i], 0, 0)),
                      pl.BlockSpec((None, d, de), lambda i, be, nu: (be[i], 0, 0)),
                      pl.BlockSpec((None, de, d), lambda i, be, nu: (be[i], 0, 0))],
            out_specs=pl.BlockSpec((MOE_BLOCK,) + row_shape, lambda i, be, nu: (i, 0, 0)),
            scratch_shapes=[pltpu.VMEM((d, de), BF16), pltpu.VMEM((d, de), BF16), pltpu.VMEM((de, d), BF16)],
        ),
        out_shape=jax.ShapeDtypeStruct((n_blocks * MOE_BLOCK,) + row_shape, jnp.uint32),
        compiler_params=_cparams("arbitrary"),
        name="routed_experts",
    )(block_e, n_used, x_sorted, w_gate, w_up, w_down)

    seg = functools.partial(_seg_of_tile, tiles_per_row=tiles_per_row, n_batch=n_batch)
    return pl.pallas_call(
        _combine_kernel,
        grid=(n_tiles,),
        in_specs=[pl.BlockSpec(memory_space=pl.ANY),
                  pl.BlockSpec(memory_space=pl.ANY),
                  pl.BlockSpec((ROW_TILE, d), lambda i: (i, 0)),
                  pl.BlockSpec((ROW_TILE, TOP_K), lambda i: (i, 0)),
                  pl.BlockSpec((ROW_TILE, d), lambda i: (i, 0)),
                  pl.BlockSpec((None, None, 1, d), lambda i: (seg(i), k_gate, 0, 0)),
                  pl.BlockSpec((d, de), lambda i: (0, 0)),
                  pl.BlockSpec((d, de), lambda i: (0, 0)),
                  pl.BlockSpec((de, d), lambda i: (0, 0))],
        out_specs=pl.BlockSpec((ROW_TILE, d), lambda i: (i, 0)),
        out_shape=jax.ShapeDtypeStruct((t, d), F32),
        scratch_shapes=idx_scratch + [pltpu.VMEM((TOP_K * ROW_TILE,) + row_shape, jnp.uint32),
                                      pltpu.SemaphoreType.DMA((2,)), pltpu.SemaphoreType.DMA(())],
        compiler_params=_cparams("arbitrary"),
        name="moe_combine",
    )(slots, y_sorted, h2, gate, xs, mod, ws_gate.astype(BF16), ws_up.astype(BF16), ws_down.astype(BF16))


def kernel(x, c, ctx, c_ctx, w_ada, b_ada, norm1_g, norm2_g, w_in_ab, s5_lambda_re, s5_lambda_im, s5_log_dt, s5_b_re, s5_b_im, s5_c_re, s5_c_im, s5_d, s5_w_glu, hgrn_lb, w_in_cd, ret_decay_logit, gdn_conv_w, gdn_a_log, gdn_dt_bias, gdn_norm_g, w_out, w_router, router_bias, w_exp_gate, w_exp_up, w_exp_down, w_sh_gate, w_sh_up, w_sh_down, final_norm_g):
    n_batch, n_lat, d = x.shape
    n_ctx = ctx.shape[1]
    n_tok = n_ctx + n_lat
    depth = w_ada.shape[0]
    assert n_ctx % ROW_TILE == 0 and n_lat % ROW_TILE == 0
    tiles_per_row, ctx_tiles = n_tok // ROW_TILE, n_ctx // ROW_TILE
    assert ctx_tiles == 1
    t = n_batch * n_tok

    xs = jnp.concatenate([ctx, x], axis=1).reshape(t, d)
    lb_all = jnp.cumsum(jax.nn.softmax(hgrn_lb.astype(F32), axis=0), axis=0)
    cond = jnp.concatenate([c, c_ctx[None]], axis=0)
    cond = jnp.pad(jax.nn.silu(cond), ((0, 8 - (n_batch + 1)), (0, 0)))
    ones_g = jnp.ones((D_HEAD,), F32)

    for i in range(depth):
        j = i // 2
        mod = matmul(cond, w_ada[i], F32, 8, 1024, "ada_mod")[:n_batch + 1] + b_ada[i]
        mod = mod.reshape(n_batch + 1, 6, 1, d)
        h = norm_mod(xs, norm1_g[i], mod, 0, 1, tiles_per_row, n_batch)
        if i % 2 == 0:
            z = matmul(h, w_in_ab[j].astype(BF16), F32, 512, 1024, "in_proj")
            a_width = s5_d.shape[1]
            b_heads = (z.shape[1] - a_width) // 5 // B_HEAD
            y_a = mixer_s5(z, n_batch, n_tok, n_ctx, s5_lambda_re[j], s5_lambda_im[j], s5_log_dt[j],
                           s5_b_re[j], s5_b_im[j], s5_c_re[j], s5_c_im[j], s5_d[j], s5_w_glu[j])
            o_f, o_b = mixer_gla(z, n_batch, n_tok, n_ctx, a_width, lb_all[j], b_heads, B_HEAD)
            gcb = (a_width + 4 * b_heads * B_HEAD) // (b_heads * B_HEAD)
            y_b = post_norm_gate(o_f, o_b, z, gcb, B_HEAD, jnp.ones((B_HEAD,), F32))
            m1, m2 = y_a, y_b
        else:
            cd_cols = w_in_cd.shape[2]
            cd_pad = -(-cd_cols // 1024) * 1024
            w_cd = jnp.pad(w_in_cd[j].astype(BF16), ((0, 0), (0, cd_pad - cd_cols)))
            z = matmul(h, w_cd, F32, 512, 1024, "in_proj")
            c_heads = ret_decay_logit.shape[2]
            d_heads = gdn_a_log.shape[2]
            o_f, o_b = mixer_retention(z, n_batch, n_tok, n_ctx, ret_decay_logit[j], c_heads)
            y_c = post_norm_gate(o_f, o_b, z, (2 * c_heads * C_QK + c_heads * C_V) // (c_heads * C_V), C_V,
                                 jnp.ones((C_V,), F32))
            col0 = 2 * c_heads * C_QK + 2 * c_heads * C_V
            o_f, o_b = mixer_gdn(z, n_batch, n_tok, n_ctx, col0, gdn_conv_w[j], gdn_a_log[j], gdn_dt_bias[j],
                                 d_heads, D_HEAD)
            y_d = post_norm_gate(o_f, o_b, z, (col0 + 3 * d_heads * D_HEAD) // (d_heads * D_HEAD), D_HEAD,
                                 gdn_norm_g[j])
            m1, m2 = y_c, y_d
        xs = out_proj(m1, m2, w_out[i].astype(BF16), xs, mod, 2, tiles_per_row, n_batch)
        h2, h2_packed = norm_mod(xs, norm2_g[i], mod, 3, 4, tiles_per_row, n_batch, packed=True)
        xs = moe_block(h2, h2_packed, xs, mod, 5, tiles_per_row, n_batch, w_router[i], router_bias[i],
                       w_exp_gate[i], w_exp_up[i], w_exp_down[i], w_sh_gate[i], w_sh_up[i], w_sh_down[i])
    out = final_norm(xs, final_norm_g, n_batch, tiles_per_row, ctx_tiles)
    return out.reshape(n_batch, n_lat, d)
```

```python
import functools

import numpy as np
import jax
import jax.numpy as jnp
from jax import lax
from jax.experimental import pallas as pl
from jax.experimental.pallas import tpu as pltpu

F32 = jnp.float32
BF16 = jnp.bfloat16

EPS = 1e-6
GRID_W = 64
CHUNK = 64
S5_L = 16
S5_H = 16
S5_P = 64
B_HEAD = 128
C_QK = 128
C_V = 256
D_HEAD = 128
ROPE_BASE = 10000.0
N_EXPERTS = 64
N_GROUPS = 8
TOPK_GROUPS = 4
TOP_K = 8
ROUTED_SCALE = 2.5
MOE_BLOCK = 256
ROW_TILE = 256
VMEM_LIMIT_V7X = 56 * 1024 * 1024
EXP_CLAMP = 80.0


def _cparams(*sem):
    return pltpu.CompilerParams(dimension_semantics=sem, vmem_limit_bytes=VMEM_LIMIT_V7X)


def _silu(x):
    return x * jax.nn.sigmoid(x)


def _dot(a, b):
    return jnp.dot(a.astype(BF16), b.astype(BF16), preferred_element_type=F32)


def _dot_nt(a, b):
    return lax.dot_general(a.astype(BF16), b.astype(BF16), (((1,), (1,)), ((), ())), preferred_element_type=F32)


def _dot_tn(a, b):
    return lax.dot_general(a.astype(BF16), b.astype(BF16), (((0,), (0,)), ((), ())), preferred_element_type=F32)


def _split3(x):
    x1 = x.astype(BF16)
    r = x - x1.astype(F32)
    x2 = r.astype(BF16)
    x3 = (r - x2.astype(F32)).astype(BF16)
    return x1, x2, x3


def _tri_dot(tri, x):
    return sum(jnp.dot(tri, p, preferred_element_type=F32) for p in _split3(x))


def _dot_tri(x, tri):
    return sum(jnp.dot(p, tri, preferred_element_type=F32) for p in _split3(x))


def _tri_mask(n, lower):
    r = lax.broadcasted_iota(jnp.int32, (n, n), 0)
    c = lax.broadcasted_iota(jnp.int32, (n, n), 1)
    return (r >= c) if lower else (r <= c)


def _seg_of_tile(i, tiles_per_row, n_batch):
    return jnp.where(i % tiles_per_row == 0, n_batch, i // tiles_per_row)


def _pack_halves(y):
    half = y.shape[1] // 2
    lo = lax.bitcast_convert_type(y[:, :half].astype(BF16).astype(F32), jnp.uint32) >> 16
    hi = lax.bitcast_convert_type(y[:, half:].astype(BF16).astype(F32), jnp.uint32) & jnp.uint32(0xFFFF0000)
    return hi | lo


def _unpack_halves(w):
    lo = lax.bitcast_convert_type(w << 16, F32)
    hi = lax.bitcast_convert_type(w & jnp.uint32(0xFFFF0000), F32)
    return lo, hi


LANES = 128


def _store_rows(ref, packed):
    for s in range(ref.shape[1]):
        ref[:, s, :] = packed[:, s * LANES:(s + 1) * LANES]


def _load_rows(ref, row0, rows):
    return jnp.concatenate([ref[row0:row0 + rows, s, :] for s in range(ref.shape[1])], axis=1)


def _norm_mod_kernel(x_ref, g_ref, shift_ref, scale_ref, o_ref, *packed_ref):
    x = x_ref[...]
    y = x * lax.rsqrt(jnp.mean(x * x, axis=-1, keepdims=True) + EPS) * g_ref[...]
    y = y * (1.0 + scale_ref[...]) + shift_ref[...]
    o_ref[...] = y.astype(o_ref.dtype)
    if packed_ref:
        _store_rows(packed_ref[0], _pack_halves(y))


def norm_mod(xs, g, mod, k_shift, k_scale, tiles_per_row, n_batch, packed=False):
    t, d = xs.shape
    seg = functools.partial(_seg_of_tile, tiles_per_row=tiles_per_row, n_batch=n_batch)
    out_specs = [pl.BlockSpec((ROW_TILE, d), lambda i: (i, 0))]
    out_shape = [jax.ShapeDtypeStruct((t, d), BF16)]
    if packed:
        out_specs.append(pl.BlockSpec((ROW_TILE, d // 2 // LANES, LANES), lambda i: (i, 0, 0)))
        out_shape.append(jax.ShapeDtypeStruct((t, d // 2 // LANES, LANES), jnp.uint32))
    res = pl.pallas_call(
        _norm_mod_kernel,
        grid=(t // ROW_TILE,),
        in_specs=[
            pl.BlockSpec((ROW_TILE, d), lambda i: (i, 0)),
            pl.BlockSpec((1, d), lambda i: (0, 0)),
            pl.BlockSpec((None, None, 1, d), lambda i: (seg(i), k_shift, 0, 0)),
            pl.BlockSpec((None, None, 1, d), lambda i: (seg(i), k_scale, 0, 0)),
        ],
        out_specs=out_specs,
        out_shape=out_shape,
        compiler_params=_cparams("parallel"),
        name="norm_mod",
    )(xs, g.reshape(1, d), mod, mod)
    return res if packed else res[0]


def _final_norm_kernel(x_ref, g_ref, o_ref):
    x = x_ref[...]
    o_ref[...] = x * lax.rsqrt(jnp.mean(x * x, axis=-1, keepdims=True) + EPS) * g_ref[...]


def final_norm(xs, g, n_batch, tiles_per_row, ctx_tiles):
    t, d = xs.shape
    lat_tiles = tiles_per_row - ctx_tiles
    return pl.pallas_call(
        _final_norm_kernel,
        grid=(n_batch, lat_tiles),
        in_specs=[
            pl.BlockSpec((ROW_TILE, d), lambda b, i: (b * tiles_per_row + ctx_tiles + i, 0)),
            pl.BlockSpec((1, d), lambda b, i: (0, 0)),
        ],
        out_specs=pl.BlockSpec((ROW_TILE, d), lambda b, i: (b * lat_tiles + i, 0)),
        out_shape=jax.ShapeDtypeStruct((n_batch * lat_tiles * ROW_TILE, d), F32),
        compiler_params=_cparams("parallel", "parallel"),
        name="final_norm",
    )(xs, g.reshape(1, d))


def _mm_kernel(a_ref, w_ref, o_ref):
    o_ref[...] = jnp.dot(a_ref[...].astype(BF16), w_ref[...].astype(BF16),
                         preferred_element_type=F32).astype(o_ref.dtype)


def matmul(a, w, out_dtype, tm, tn, name):
    m, k = a.shape
    n = w.shape[1]
    return pl.pallas_call(
        _mm_kernel,
        grid=(n // tn, m // tm),
        in_specs=[pl.BlockSpec((tm, k), lambda j, i: (i, 0)),
                  pl.BlockSpec((k, tn), lambda j, i: (0, j))],
        out_specs=pl.BlockSpec((tm, tn), lambda j, i: (i, j)),
        out_shape=jax.ShapeDtypeStruct((m, n), out_dtype),
        compiler_params=_cparams("parallel", "parallel"),
        name=name,
    )(a, w)


def _out_proj_kernel(a1_ref, a2_ref, w1_ref, w2_ref, res_ref, gate_ref, o_ref):
    y = jnp.dot(a1_ref[...], w1_ref[...], preferred_element_type=F32)
    y += jnp.dot(a2_ref[...], w2_ref[...], preferred_element_type=F32)
    o_ref[...] = res_ref[...] + gate_ref[...] * y


def out_proj(a1, a2, w, xs, mod, k_gate, tiles_per_row, n_batch, tn=1024):
    t, d = xs.shape
    k1, k2 = a1.shape[1], a2.shape[1]
    assert k1 == k2
    seg = functools.partial(_seg_of_tile, tiles_per_row=tiles_per_row, n_batch=n_batch)
    return pl.pallas_call(
        _out_proj_kernel,
        grid=(d // tn, t // ROW_TILE),
        in_specs=[
            pl.BlockSpec((ROW_TILE, k1), lambda j, i: (i, 0)),
            pl.BlockSpec((ROW_TILE, k2), lambda j, i: (i, 0)),
            pl.BlockSpec((k1, tn), lambda j, i: (0, j)),
            pl.BlockSpec((k2, tn), lambda j, i: (1, j)),
            pl.BlockSpec((ROW_TILE, tn), lambda j, i: (i, j)),
            pl.BlockSpec((None, None, 1, tn), lambda j, i: (seg(i), k_gate, 0, j)),
        ],
        out_specs=pl.BlockSpec((ROW_TILE, tn), lambda j, i: (i, j)),
        out_shape=jax.ShapeDtypeStruct((t, d), F32),
        compiler_params=_cparams("parallel", "parallel"),
        name="out_proj",
    )(a1, a2, w, w, xs, mod)


def _post_kernel(of_ref, ob_ref, gate_ref, ng_ref, o_ref, *, head_dim):
    o = of_ref[...] + ob_ref[...]
    g = gate_ref[...].astype(F32)
    width = o.shape[1]
    for h in range(width // head_dim):
        sl = slice(h * head_dim, (h + 1) * head_dim)
        oh = o[:, sl]
        y = oh * lax.rsqrt(jnp.mean(oh * oh, axis=-1, keepdims=True) + EPS) * ng_ref[...]
        o_ref[:, sl] = (y * _silu(g[:, sl])).astype(o_ref.dtype)


def post_norm_gate(o_f, o_b, z, gate_col_block, head_dim, norm_g):
    t, width = o_f.shape
    return pl.pallas_call(
        functools.partial(_post_kernel, head_dim=head_dim),
        grid=(t // ROW_TILE,),
        in_specs=[
            pl.BlockSpec((ROW_TILE, width), lambda i: (i, 0)),
            pl.BlockSpec((ROW_TILE, width), lambda i: (i, 0)),
            pl.BlockSpec((ROW_TILE, width), lambda i: (i, gate_col_block)),
            pl.BlockSpec((1, head_dim), lambda i: (0, 0)),
        ],
        out_specs=pl.BlockSpec((ROW_TILE, width), lambda i: (i, 0)),
        out_shape=jax.ShapeDtypeStruct((t, width), BF16),
        compiler_params=_cparams("parallel"),
        name="post_norm_gate",
    )(o_f, o_b, z, norm_g.reshape(1, head_dim).astype(F32))


def _bwd_chunk(j, ctx_chunks, n_chunks):
    return jnp.where(j < ctx_chunks, ctx_chunks - 1 - j, n_chunks + ctx_chunks - 1 - j)


def _s5_direction_tables(lam_re, lam_im, log_dt, b_re, b_im, c_re, c_im, reverse):
    hi = lax.Precision.HIGHEST
    ln = S5_L
    lam_re, lam_im, b_re, b_im, c_re, c_im = (p.astype(F32) for p in (lam_re, lam_im, b_re, b_im, c_re, c_im))
    dt = jnp.exp(log_dt.astype(F32))[:, None]
    mag = jnp.exp(lam_re * dt)
    ab_re, ab_im = mag * jnp.cos(lam_im * dt), mag * jnp.sin(lam_im * dt)
    den = lam_re * lam_re + lam_im * lam_im
    fr = ((ab_re - 1) * lam_re + ab_im * lam_im) / den
    fi = (ab_im * lam_re - (ab_re - 1) * lam_im) / den
    bb_re = fr[..., None] * b_re - fi[..., None] * b_im
    bb_im = fr[..., None] * b_im + fi[..., None] * b_re
    tau = jnp.arange(ln + 1, dtype=F32)[:, None, None]
    pw = jnp.exp(tau * (lam_re * dt))
    pr, pi = pw * jnp.cos(tau * (lam_im * dt)), pw * jnp.sin(tau * (lam_im * dt))
    abr = pr[..., None] * bb_re - pi[..., None] * bb_im
    abi = pr[..., None] * bb_im + pi[..., None] * bb_re
    kern = (jnp.einsum('ghp,tgpk->tghk', c_re, abr[:ln], precision=hi)
            - jnp.einsum('ghp,tgpk->tghk', c_im, abi[:ln], precision=hi))
    pos = jnp.arange(ln)
    lag = (pos[None, :] - pos[:, None]) if not reverse else (pos[:, None] - pos[None, :])
    toe = jnp.where((lag >= 0)[:, :, None, None, None], kern[jnp.clip(lag, 0, ln - 1)], 0.0)
    g = lam_re.shape[0]
    intra = jnp.transpose(toe, (2, 0, 4, 1, 3)).reshape(g, ln * S5_H, ln * S5_H)
    pw_in = (ln - 1 - pos) if not reverse else pos
    inj = jnp.concatenate([abr[pw_in], abi[pw_in]], axis=2)
    inject = jnp.transpose(inj, (1, 0, 3, 2)).reshape(g, ln * S5_H, 2 * S5_P)
    pw_out = (pos + 1) if not reverse else (ln - pos)
    w_re = c_re[None] * pr[pw_out][:, :, None, :] - c_im[None] * pi[pw_out][:, :, None, :]
    w_im = -(c_re[None] * pi[pw_out][:, :, None, :] + c_im[None] * pr[pw_out][:, :, None, :])
    readout = jnp.transpose(jnp.concatenate([w_re, w_im], axis=3), (1, 3, 0, 2)).reshape(g, 2 * S5_P, ln * S5_H)
    decay = jnp.stack([pr[ln], pi[ln]])
    return intra, inject, readout, decay


def _s5_in_kernel(u_ref, w_ref, yi_ref, s_ref):
    r = jnp.dot(u_ref[...], w_ref[...], preferred_element_type=F32)
    n_intra = yi_ref.shape[-1]
    yi_ref[...] = r[:, :n_intra]
    s_ref[...] = r[:, n_intra:]


def _s5_scan_kernel(a_ref, sf_ref, sb_ref, of_ref, ob_ref, st_ref):
    @pl.when(pl.program_id(1) == 0)
    def _():
        st_ref[...] = jnp.zeros_like(st_ref)

    steps = sf_ref.shape[0]
    for d, (s_ref, o_ref) in enumerate(((sf_ref, of_ref), (sb_ref, ob_ref))):
        ar, ai = a_ref[2 * d], a_ref[2 * d + 1]
        sr, si = st_ref[2 * d], st_ref[2 * d + 1]
        for q in range(steps):
            r = q if d == 0 else steps - 1 - q
            o_ref[r, 0] = sr
            o_ref[r, 1] = si
            sr, si = (ar * sr - ai * si + s_ref[r, 0], ar * si + ai * sr + s_ref[r, 1])
        st_ref[2 * d] = sr
        st_ref[2 * d + 1] = si


def _s5_out_kernel(yi_ref, st_ref, w_ref, u_ref, d_ref, o_ref):
    y = yi_ref[...] + jnp.dot(st_ref[...].astype(BF16), w_ref[...], preferred_element_type=F32)
    o_ref[...] = y + d_ref[...] * u_ref[...].astype(F32)


def _glu_kernel(y_ref, w_ref, o_ref):
    y = jax.nn.gelu(y_ref[...])
    o_ref[...] = (y * jax.nn.sigmoid(jnp.dot(y.astype(BF16), w_ref[...], preferred_element_type=F32))
                  ).astype(o_ref.dtype)


def mixer_s5(z, n_batch, n_tok, n_ctx, lam_re, lam_im, log_dt, b_re, b_im, c_re, c_im, d_skip, w_glu):
    t = z.shape[0]
    g = lam_re.shape[1]
    width = g * S5_H
    lh = S5_L * S5_H
    rows = t // S5_L
    tabs = [_s5_direction_tables(lam_re[d], lam_im[d], log_dt[d], b_re[d], b_im[d], c_re[d], c_im[d], d == 1)
            for d in range(2)]
    w_in = jnp.concatenate([tabs[0][0] + tabs[1][0], tabs[0][1], tabs[1][1]], axis=2).astype(BF16)
    w_st = jnp.concatenate([tabs[0][2], tabs[1][2]], axis=1).astype(BF16)
    decay = jnp.concatenate([tabs[0][3], tabs[1][3]], axis=0).reshape(4, g * S5_P // 128, 128)
    u = z[:, :width].reshape(rows, S5_L, g, S5_H)
    u_g = jnp.transpose(u, (2, 0, 1, 3)).reshape(g, rows, lh).astype(BF16)

    n_st = 4 * S5_P
    yi, s_in = pl.pallas_call(
        _s5_in_kernel,
        grid=(g,),
        in_specs=[pl.BlockSpec((None, rows, lh), lambda i: (i, 0, 0)),
                  pl.BlockSpec((None, lh, lh + n_st), lambda i: (i, 0, 0))],
        out_specs=[pl.BlockSpec((None, rows, lh), lambda i: (i, 0, 0)),
                   pl.BlockSpec((None, rows, n_st), lambda i: (i, 0, 0))],
        out_shape=[jax.ShapeDtypeStruct((g, rows, lh), F32), jax.ShapeDtypeStruct((g, rows, n_st), F32)],
        compiler_params=_cparams("parallel"),
        name="s5_in",
    )(u_g, w_in)

    gp = g * S5_P // 128
    s_scan = jnp.transpose(s_in.reshape(g, rows, 4, S5_P), (1, 2, 0, 3)).reshape(rows, 4, gp, 128)
    blocks_per_row = n_tok // S5_L
    cb = n_ctx // S5_L
    assert blocks_per_row % cb == 0
    nblk = blocks_per_row // cb
    bwd = lambda j: jnp.where(j == 0, 0, nblk - j)
    st_f, st_b = pl.pallas_call(
        _s5_scan_kernel,
        grid=(n_batch, nblk),
        in_specs=[pl.BlockSpec((4, gp, 128), lambda b, j: (0, 0, 0)),
                  pl.BlockSpec((cb, 2, gp, 128), lambda b, j: (b * nblk + j, 0, 0, 0)),
                  pl.BlockSpec((cb, 2, gp, 128), lambda b, j: (b * nblk + bwd(j), 1, 0, 0))],
        out_specs=[pl.BlockSpec((cb, 2, gp, 128), lambda b, j: (b * nblk + j, 0, 0, 0)),
                   pl.BlockSpec((cb, 2, gp, 128), lambda b, j: (b * nblk + bwd(j), 0, 0, 0))],
        out_shape=[jax.ShapeDtypeStruct((rows, 2, gp, 128), F32)] * 2,
        scratch_shapes=[pltpu.VMEM((4, gp, 128), F32)],
        compiler_params=_cparams("parallel", "arbitrary"),
        name="s5_scan",
    )(decay, s_scan, s_scan)
    st = jnp.concatenate([st_f, st_b], axis=1).reshape(rows, 4, g, S5_P)
    st_g = jnp.transpose(st, (2, 0, 1, 3)).reshape(g, rows, n_st)

    d_vec = jnp.tile(d_skip.astype(F32).reshape(g, 1, S5_H), (1, S5_L, 1)).reshape(g, 1, lh)
    y_g = pl.pallas_call(
        _s5_out_kernel,
        grid=(g,),
        in_specs=[pl.BlockSpec((None, rows, lh), lambda i: (i, 0, 0)),
                  pl.BlockSpec((None, rows, n_st), lambda i: (i, 0, 0)),
                  pl.BlockSpec((None, n_st, lh), lambda i: (i, 0, 0)),
                  pl.BlockSpec((None, rows, lh), lambda i: (i, 0, 0)),
                  pl.BlockSpec((None, 1, lh), lambda i: (i, 0, 0))],
        out_specs=pl.BlockSpec((None, rows, lh), lambda i: (i, 0, 0)),
        out_shape=jax.ShapeDtypeStruct((g, rows, lh), F32),
        compiler_params=_cparams("parallel"),
        name="s5_out",
    )(yi, st_g, w_st, u_g, d_vec)
    y = jnp.transpose(y_g.reshape(g, rows, S5_L, S5_H), (1, 2, 0, 3)).reshape(t, width)

    return pl.pallas_call(
        _glu_kernel,
        grid=(t // ROW_TILE,),
        in_specs=[pl.BlockSpec((ROW_TILE, width), lambda i: (i, 0)),
                  pl.BlockSpec((width, width), lambda i: (0, 0))],
        out_specs=pl.BlockSpec((ROW_TILE, width), lambda i: (i, 0)),
        out_shape=jax.ShapeDtypeStruct((t, width), BF16),
        compiler_params=_cparams("parallel"),
        name="s5_glu",
    )(y, w_glu.astype(BF16))


def _gla_kernel(lb_ref, qf_ref, ff_ref, vf_ref, qb_ref, fb_ref, vb_ref, of_ref, ob_ref, st_ref, *, n_heads, hd):
    @pl.when(pl.program_id(1) == 0)
    def _():
        st_ref[...] = jnp.zeros_like(st_ref)

    c = qf_ref.shape[0]
    items = []
    for d, (q_ref, f_ref, v_ref, o_ref) in enumerate(((qf_ref, ff_ref, vf_ref, of_ref),
                                                       (qb_ref, fb_ref, vb_ref, ob_ref))):
        lower = d == 0
        mask = _tri_mask(c, lower)
        tri = mask.astype(BF16)
        lb = lb_ref[d]
        sig = jax.nn.sigmoid(f_ref[...])
        kk = (1.0 - lb) * (1.0 - sig)
        logf = jnp.log(lb + (1.0 - lb) * sig)
        b = _tri_dot(tri, logf)
        b_end = b[c - 1:c, :] if lower else b[0:1, :]
        qt = (_silu(q_ref[...]) * jnp.exp(b)).astype(BF16)
        kt = (kk * jnp.exp(jnp.minimum(-b, EXP_CLAMP))).astype(BF16)
        kend = (kk * jnp.exp(b_end - b)).astype(BF16)
        dec = jnp.exp(b_end)
        v = v_ref[...].astype(BF16)
        for h in range(n_heads):
            sl = slice(h * hd, (h + 1) * hd)
            items.append((d, h, sl, o_ref, mask, qt[:, sl], kt[:, sl], kend[:, sl], v[:, sl], dec[:, sl], st_ref[d, h]))
    att = [_dot_nt(it[5], it[6]) for it in items]
    inter = [_dot_nt(it[5], it[10]) for it in items]
    upd = [_dot_tn(it[8], it[7]) for it in items]
    intra = [_dot(jnp.where(it[4], a, 0.0), it[8]) for it, a in zip(items, att)]
    for it, oi, os, up in zip(items, intra, inter, upd):
        d, h, sl, o_ref = it[:4]
        o_ref[:, sl] = oi + os
        st_ref[d, h] = it[9] * it[10] + up


def mixer_gla(z, n_batch, n_tok, n_ctx, col0, lb, n_heads, hd):
    width = n_heads * hd
    z3 = z.reshape(n_batch, n_tok, z.shape[1])
    nc, cc = n_tok // CHUNK, n_ctx // CHUNK
    cb0 = col0 // width
    bwd = functools.partial(_bwd_chunk, ctx_chunks=cc, n_chunks=nc)
    blk = (None, CHUNK, width)
    fw = lambda k: pl.BlockSpec(blk, lambda b, j: (b, j, cb0 + k))
    bw = lambda k: pl.BlockSpec(blk, lambda b, j: (b, bwd(j), cb0 + k))
    o_f, o_b = pl.pallas_call(
        functools.partial(_gla_kernel, n_heads=n_heads, hd=hd),
        grid=(n_batch, nc),
        in_specs=[pl.BlockSpec((2, 1, width), lambda b, j: (0, 0, 0)),
                  fw(0), fw(1), fw(3), bw(0), bw(2), bw(3)],
        out_specs=[pl.BlockSpec(blk, lambda b, j: (b, j, 0)),
                   pl.BlockSpec(blk, lambda b, j: (b, bwd(j), 0))],
        out_shape=[jax.ShapeDtypeStruct((n_batch, n_tok, width), F32)] * 2,
        scratch_shapes=[pltpu.VMEM((2, n_heads, hd, hd), F32)],
        compiler_params=_cparams("parallel", "arbitrary"),
        name="gla_scan",
    )(lb.reshape(2, 1, width).astype(F32), z3, z3, z3, z3, z3, z3)
    return o_f.reshape(-1, width), o_b.reshape(-1, width)


def _rotary_tables(n_tok, n_ctx):
    n_lat = n_tok - n_ctx
    rows = n_lat // GRID_W
    row = jnp.repeat(jnp.arange(rows, dtype=F32), GRID_W)
    col = jnp.tile(jnp.arange(GRID_W, dtype=F32), rows)
    n_freq = C_QK // 4
    inv = ROPE_BASE ** (-jnp.arange(n_freq, dtype=F32) / n_freq)
    ang = jnp.concatenate([row[:, None] * inv, col[:, None] * inv], axis=-1)
    ang = jnp.concatenate([jnp.zeros((n_ctx, C_QK // 2), F32), ang], axis=0)
    cos, sin = jnp.cos(ang), jnp.sin(ang)
    return jnp.concatenate([cos, cos], axis=-1), jnp.concatenate([-sin, sin], axis=-1)


def _ret_kernel(cdec_ref, dmat_ref, qdec_ref, kdec_ref,
                qf_ref, kf_ref, vf_ref, cf_ref, sf_ref, qb_ref, kb_ref, vb_ref, cb_ref, sb_ref,
                of_ref, ob_ref, st_ref, *, n_heads):
    @pl.when(pl.program_id(1) == 0)
    def _():
        st_ref[...] = jnp.zeros_like(st_ref)

    half = C_QK // 2
    items = []
    for d, (q_ref, k_ref, v_ref, cos_ref, sin_ref, o_ref) in enumerate(
            ((qf_ref, kf_ref, vf_ref, cf_ref, sf_ref, of_ref), (qb_ref, kb_ref, vb_ref, cb_ref, sb_ref, ob_ref))):
        cos, sin = cos_ref[...], sin_ref[...]
        for h in range(n_heads):
            qs = slice(h * C_QK, (h + 1) * C_QK)
            vs = slice(h * C_V, (h + 1) * C_V)
            qh, kh = q_ref[:, qs].astype(F32), k_ref[:, qs].astype(F32)
            qh = (qh * cos + pltpu.roll(qh, half, axis=1) * sin) * (C_QK ** -0.5)
            kh = kh * cos + pltpu.roll(kh, half, axis=1) * sin
            items.append((d, h, vs, o_ref, qh.astype(BF16), kh.astype(BF16), (qh * qdec_ref[d, h]).astype(BF16),
                          (kh * kdec_ref[d, h]).astype(BF16), v_ref[:, vs].astype(BF16), st_ref[d, h]))
    att = [_dot_nt(it[4], it[5]) for it in items]
    inter = [_dot(it[6], it[9]) for it in items]
    upd = [_dot_tn(it[7], it[8]) for it in items]
    intra = [_dot(a * dmat_ref[it[0], it[1]], it[8]) for it, a in zip(items, att)]
    for it, oi, os, up in zip(items, intra, inter, upd):
        d, h, vs, o_ref = it[:4]
        o_ref[:, vs] = oi + os
        st_ref[d, h] = cdec_ref[d, h] * it[9] + up


def mixer_retention(z, n_batch, n_tok, n_ctx, decay_logit, n_heads):
    qw, vw = n_heads * C_QK, n_heads * C_V
    z3 = z.reshape(n_batch, n_tok, z.shape[1])
    nc, cc = n_tok // CHUNK, n_ctx // CHUNK
    bwd = functools.partial(_bwd_chunk, ctx_chunks=cc, n_chunks=nc)
    log_gamma = jax.nn.log_sigmoid(decay_logit.astype(F32))[:, :, None, None]
    idx = jnp.arange(CHUNK, dtype=F32)
    diff = idx[:, None] - idx[None, :]
    dmat_f = jnp.where(diff >= 0, jnp.exp(jnp.maximum(diff, 0.0) * log_gamma[0]), 0.0)
    dmat_b = jnp.where(diff <= 0, jnp.exp(jnp.maximum(-diff, 0.0) * log_gamma[1]), 0.0)
    dmat = jnp.stack([dmat_f, dmat_b])
    ones = jnp.ones((1, 1, 1, C_QK), F32)
    pos_f, pos_b = idx[None, None, :, None], (CHUNK - 1 - idx)[None, None, :, None]
    lg = log_gamma
    qdec = jnp.concatenate([jnp.exp((pos_f + 1) * lg[0:1]), jnp.exp((pos_b + 1) * lg[1:2])]) * ones
    kdec = jnp.concatenate([jnp.exp((CHUNK - 1 - pos_f) * lg[0:1]), jnp.exp((CHUNK - 1 - pos_b) * lg[1:2])]) * ones
    cdec = jnp.exp(CHUNK * log_gamma[:, :, 0, 0])
    cos2, sin2 = _rotary_tables(n_tok, n_ctx)
    full = lambda shape: pl.BlockSpec(shape, lambda b, j: (0,) * len(shape))
    tab = lambda order: pl.BlockSpec((CHUNK, C_QK), lambda b, j: (order(j), 0))
    ident = lambda j: j
    vcb = 2 * qw // vw
    assert vcb * vw == 2 * qw
    def specs(order):
        return [pl.BlockSpec((None, CHUNK, qw), lambda b, j: (b, order(j), 0)),
                pl.BlockSpec((None, CHUNK, qw), lambda b, j: (b, order(j), 1)),
                pl.BlockSpec((None, CHUNK, vw), lambda b, j: (b, order(j), vcb)),
                tab(order), tab(order)]
    o_f, o_b = pl.pallas_call(
        functools.partial(_ret_kernel, n_heads=n_heads),
        grid=(n_batch, nc),
        in_specs=[pl.BlockSpec(memory_space=pltpu.SMEM), full((2, n_heads, CHUNK, CHUNK)),
                  full((2, n_heads, CHUNK, C_QK)), full((2, n_heads, CHUNK, C_QK))] + specs(ident) + specs(bwd),
        out_specs=[pl.BlockSpec((None, CHUNK, vw), lambda b, j: (b, j, 0)),
                   pl.BlockSpec((None, CHUNK, vw), lambda b, j: (b, bwd(j), 0))],
        out_shape=[jax.ShapeDtypeStruct((n_batch, n_tok, vw), F32)] * 2,
        scratch_shapes=[pltpu.VMEM((2, n_heads, C_QK, C_V), F32)],
        compiler_params=_cparams("parallel", "arbitrary"),
        name="retention_scan",
    )(cdec, dmat, qdec, kdec, z3, z3, z3, cos2, sin2, z3, z3, z3, cos2, sin2)
    return o_f.reshape(-1, vw), o_b.reshape(-1, vw)


def _gdn_conv_kernel(prev_ref, cur_ref, next_ref, w_ref, o_ref, *, tiles_per_row, ctx_tiles, n_heads, hd):
    i = pl.program_id(0)
    r = i % tiles_per_row
    first = jnp.logical_or(r == 0, r == ctx_tiles)
    last = jnp.logical_or(r == ctx_tiles - 1, r == tiles_per_row - 1)
    x = cur_ref[...].astype(F32)
    rows = x.shape[0]
    rid = lax.broadcasted_iota(jnp.int32, x.shape, 0)
    hp = prev_ref.shape[0]
    x_prev = jnp.where(first, 0.0, prev_ref[hp - 1:hp, :].astype(F32))
    x_next = jnp.where(last, 0.0, next_ref[0:1, :].astype(F32))
    left = jnp.where(rid == 0, x_prev, pltpu.roll(x, 1, axis=0))
    right = jnp.where(rid == rows - 1, x_next, pltpu.roll(x, rows - 1, axis=0))
    w = w_ref[...]
    y = _silu(left * w[0:1, :] + x * w[1:2, :] + right * w[2:3, :])
    width = n_heads * hd
    for h in range(3 * n_heads):
        sl = slice(h * hd, (h + 1) * hd)
        yh = y[:, sl]
        if h < 2 * n_heads:
            yh = yh * lax.rsqrt(jnp.sum(yh * yh, axis=-1, keepdims=True) + EPS)
            if h < n_heads:
                yh = yh * (hd ** -0.5)
        o_ref[:, sl] = yh.astype(o_ref.dtype)


def _gdn_kernel(qf_ref, kf_ref, vf_ref, cf_ref, rf_ref, qb_ref, kb_ref, vb_ref, cb_ref, rb_ref,
                of_ref, ob_ref, st_ref, *, n_heads, hd):
    @pl.when(pl.program_id(1) == 0)
    def _():
        st_ref[...] = jnp.zeros_like(st_ref)

    c = qf_ref.shape[0]
    eye = (lax.broadcasted_iota(jnp.int32, (c, c), 0) == lax.broadcasted_iota(jnp.int32, (c, c), 1)).astype(F32)
    items = []
    for d, (q_ref, k_ref, v_ref, col_ref, row_ref, o_ref) in enumerate(
            ((qf_ref, kf_ref, vf_ref, cf_ref, rf_ref, of_ref), (qb_ref, kb_ref, vb_ref, cb_ref, rb_ref, ob_ref))):
        lower = d == 0
        incl = _tri_mask(c, lower)
        strict = jnp.logical_and(incl, eye == 0.0)
        col = col_ref[...]
        row = row_ref[...]
        b_cols = _tri_dot(incl.astype(BF16), col)
        b_rows = _dot_tri(row, _tri_mask(c, not lower).astype(BF16))
        for h in range(n_heads):
            sl = slice(h * hd, (h + 1) * hd)
            ib, ig = d * n_heads + h, (2 + d) * n_heads + h
            be = col[:, ib:ib + 1]
            bc = b_cols[:, ig:ig + 1]
            br = b_rows[ig:ig + 1, :]
            b_end = bc[c - 1:c, :] if lower else bc[0:1, :]
            gam_i = jnp.exp(jnp.where(incl, bc - br, -1e30))
            gam_s = jnp.where(strict, gam_i, 0.0)
            qh, kh, vh = q_ref[:, sl], k_ref[:, sl], v_ref[:, sl]
            eb = jnp.exp(bc)
            rhs = jnp.concatenate([be * vh.astype(F32), (be * eb) * kh.astype(F32)], axis=1).astype(BF16)
            items.append(dict(d=d, h=h, sl=sl, o_ref=o_ref, be=be, gam_i=gam_i, gam_s=gam_s, q=qh, k=kh, rhs=rhs,
                              qe=(qh.astype(F32) * eb).astype(BF16), dec=jnp.exp(b_end),
                              ke=(kh.astype(F32) * jnp.exp(b_end - bc)).astype(BF16), s=st_ref[d, h]))
    kk = [_dot_nt(it["k"], it["k"]) for it in items]
    qk = [_dot_nt(it["q"], it["k"]) for it in items]
    qs = [_dot(it["qe"], it["s"]) for it in items]
    ns = [it["be"] * a * it["gam_s"] for it, a in zip(items, kk)]
    invs = [eye - n for n in ns]
    ps = [_dot(n, n) for n in ns]
    levels = int(np.log2(c)) - 1
    for lvl in range(levels):
        prods = [_dot(inv, p) for inv, p in zip(invs, ps)]
        if lvl < levels - 1:
            ps = [_dot(p, p) for p in ps]
        invs = [inv + pr for inv, pr in zip(invs, prods)]
    sols = [_dot(inv, it["rhs"]) for inv, it in zip(invs, items)]
    sks = [_dot(sol[:, hd:], it["s"]) for sol, it in zip(sols, items)]
    us = [sol[:, :hd] - sk for sol, sk in zip(sols, sks)]
    intra = [_dot(a * it["gam_i"], u) for a, it, u in zip(qk, items, us)]
    upd = [_dot_tn(it["ke"], u) for it, u in zip(items, us)]
    for it, oi, os, up in zip(items, intra, qs, upd):
        it["o_ref"][:, it["sl"]] = oi + os
        st_ref[it["d"], it["h"]] = it["dec"] * it["s"] + up


def mixer_gdn(z, n_batch, n_tok, n_ctx, col0, conv_w, a_log, dt_bias, n_heads, hd):
    t, zw = z.shape
    width = n_heads * hd
    tiles_per_row, ctx_tiles = n_tok // ROW_TILE, n_ctx // ROW_TILE
    cb0 = col0 // (3 * width)
    assert cb0 * 3 * width == col0
    halo = 8
    hpt = ROW_TILE // halo
    n_halo = t // halo
    qkv = pl.pallas_call(
        functools.partial(_gdn_conv_kernel, tiles_per_row=tiles_per_row, ctx_tiles=ctx_tiles, n_heads=n_heads, hd=hd),
        grid=(t // ROW_TILE,),
        in_specs=[pl.BlockSpec((halo, 3 * width), lambda i: (jnp.maximum(i * hpt - 1, 0), cb0)),
                  pl.BlockSpec((ROW_TILE, 3 * width), lambda i: (i, cb0)),
                  pl.BlockSpec((halo, 3 * width), lambda i: (jnp.minimum((i + 1) * hpt, n_halo - 1), cb0)),
                  pl.BlockSpec((3, 3 * width), lambda i: (0, 0))],
        out_specs=pl.BlockSpec((ROW_TILE, 3 * width), lambda i: (i, 0)),
        out_shape=jax.ShapeDtypeStruct((t, 3 * width), BF16),
        compiler_params=_cparams("parallel"),
        name="gdn_conv",
    )(z, z, z, conv_w.astype(F32))

    sc0 = col0 + 4 * width
    small = z[:, sc0:sc0 + 4 * n_heads].astype(F32)
    a_log, dt_bias = a_log.astype(F32), dt_bias.astype(F32)
    be = jax.nn.sigmoid(small[:, :2 * n_heads])
    la_f = -jnp.exp(a_log[0]) * jax.nn.softplus(small[:, 2 * n_heads:3 * n_heads] + dt_bias[0])
    la_b = -jnp.exp(a_log[1]) * jax.nn.softplus(small[:, 3 * n_heads:] + dt_bias[1])
    cols = jnp.concatenate([be, la_f, la_b], axis=1)
    nc, cc = n_tok // CHUNK, n_ctx // CHUNK
    cols3 = cols.reshape(n_batch, n_tok, 4 * n_heads)
    rows4 = jnp.transpose(cols.reshape(n_batch, nc, CHUNK, 4 * n_heads), (0, 1, 3, 2))
    qkv3 = qkv.reshape(n_batch, n_tok, 3 * width)
    bwd = functools.partial(_bwd_chunk, ctx_chunks=cc, n_chunks=nc)
    ident = lambda j: j
    def specs(order):
        return [pl.BlockSpec((None, CHUNK, width), lambda b, j: (b, order(j), 0)),
                pl.BlockSpec((None, CHUNK, width), lambda b, j: (b, order(j), 1)),
                pl.BlockSpec((None, CHUNK, width), lambda b, j: (b, order(j), 2)),
                pl.BlockSpec((None, CHUNK, 4 * n_heads), lambda b, j: (b, order(j), 0)),
                pl.BlockSpec((None, None, 4 * n_heads, CHUNK), lambda b, j: (b, order(j), 0, 0))]
    o_f, o_b = pl.pallas_call(
        functools.partial(_gdn_kernel, n_heads=n_heads, hd=hd),
        grid=(n_batch, nc),
        in_specs=specs(ident) + specs(bwd),
        out_specs=[pl.BlockSpec((None, CHUNK, width), lambda b, j: (b, j, 0)),
                   pl.BlockSpec((None, CHUNK, width), lambda b, j: (b, bwd(j), 0))],
        out_shape=[jax.ShapeDtypeStruct((n_batch, n_tok, width), F32)] * 2,
        scratch_shapes=[pltpu.VMEM((2, n_heads, hd, hd), F32)],
        compiler_params=_cparams("parallel", "arbitrary"),
        name="gdn_scan",
    )(qkv3, qkv3, qkv3, cols3, rows4, qkv3, qkv3, qkv3, cols3, rows4)
    return o_f.reshape(-1, width), o_b.reshape(-1, width)


def _first_max(x, idx, sentinel):
    m = jnp.max(x, axis=0, keepdims=True)
    return m, jnp.min(jnp.where(x == m, idx, sentinel), axis=0, keepdims=True)


def _router_kernel(h_ref, wt_ref, bias_ref, e_ref, g_ref, rank_ref, cnt_ref, carry_ref):
    @pl.when(pl.program_id(0) == 0)
    def _():
        carry_ref[...] = jnp.zeros_like(carry_ref)

    cols = h_ref.shape[0]
    gsz = N_EXPERTS // N_GROUPS
    neg = -jnp.inf
    logits = lax.dot_general(wt_ref[...], h_ref[...], (((1,), (1,)), ((), ())), preferred_element_type=F32)
    scores = jax.nn.sigmoid(logits)
    sel = scores + bias_ref[...]
    i_g = lax.broadcasted_iota(jnp.int32, (gsz, cols), 0)
    blocks, g_scores = [], []
    for g in range(N_GROUPS):
        blk = sel[g * gsz:(g + 1) * gsz, :]
        m1, first = _first_max(blk, i_g, gsz)
        m2 = jnp.max(jnp.where(i_g == first, neg, blk), axis=0, keepdims=True)
        blocks.append(blk)
        g_scores.append(m1 + m2)
    keep = [jnp.zeros((1, cols), jnp.bool_) for _ in range(N_GROUPS)]
    for _ in range(TOPK_GROUPS):
        best = functools.reduce(jnp.maximum, g_scores)
        found = jnp.zeros((1, cols), jnp.bool_)
        for g in range(N_GROUPS):
            pick = jnp.logical_and(g_scores[g] == best, jnp.logical_not(found))
            found = jnp.logical_or(found, pick)
            keep[g] = jnp.logical_or(keep[g], pick)
            g_scores[g] = jnp.where(pick, neg, g_scores[g])
    cur = jnp.concatenate([jnp.where(keep[g], blocks[g], neg) for g in range(N_GROUPS)], axis=0)
    i_e = lax.broadcasted_iota(jnp.int32, (N_EXPERTS, cols), 0)
    picks, gates = [], []
    for _ in range(TOP_K):
        _, idx = _first_max(cur, i_e, N_EXPERTS)
        pick = i_e == idx
        picks.append((idx, pick))
        gates.append(jnp.sum(jnp.where(pick, scores, 0.0), axis=0, keepdims=True))
        cur = jnp.where(pick, neg, cur)
    total = functools.reduce(jnp.add, gates)
    chosen = functools.reduce(jnp.logical_or, [p for _, p in picks]).astype(F32)
    r = lax.broadcasted_iota(jnp.int32, (cols, cols), 0)
    c = lax.broadcasted_iota(jnp.int32, (cols, cols), 1)
    before = jnp.dot(chosen.astype(BF16), (r < c).astype(BF16), preferred_element_type=F32) + carry_ref[...]
    for k, (idx, pick) in enumerate(picks):
        e_ref[k:k + 1, :] = idx
        g_ref[k:k + 1, :] = gates[k] / total * ROUTED_SCALE
        rank_ref[k:k + 1, :] = jnp.sum(jnp.where(pick, before, 0.0), axis=0, keepdims=True).astype(jnp.int32)
    carry_ref[...] += jnp.sum(chosen, axis=1, keepdims=True)
    cnt_ref[...] = carry_ref[...]


def _slot_fetch(slots_hbm, idx_smem, sem_idx, tile, buf):
    return pltpu.make_async_copy(slots_hbm.at[tile], idx_smem.at[buf], sem_idx.at[buf])


def _dispatch_kernel(padded_ref, pend_ref, nu_ref, slots_hbm, hp_ref, zeros_hbm, x_hbm, idx_smem, sem_idx, sem_rows):
    i = pl.program_id(0)
    rows = hp_ref.shape[0]

    @pl.when(i == 0)
    def _():
        _slot_fetch(slots_hbm, idx_smem, sem_idx, 0, 0).start()

        def zero_tail(e):
            return pltpu.make_async_copy(zeros_hbm, x_hbm.at[pl.ds(pend_ref[e] - MOE_BLOCK, MOE_BLOCK)], sem_rows)

        def zstart(e, carry):
            @pl.when(padded_ref[e] > 0)
            def _():
                zero_tail(e).start()
            return carry

        def zwait(e, carry):
            @pl.when(padded_ref[e] > 0)
            def _():
                zero_tail(e).wait()
            return carry

        lax.fori_loop(0, N_EXPERTS, zstart, 0)
        lax.fori_loop(0, N_EXPERTS, zwait, 0)

        def unused_block(j):
            return pltpu.make_async_copy(zeros_hbm, x_hbm.at[pl.ds(j * MOE_BLOCK, MOE_BLOCK)], sem_rows)

        n_blocks = x_hbm.shape[0] // MOE_BLOCK
        lax.fori_loop(nu_ref[0], n_blocks, lambda j, carry: (unused_block(j).start(), carry)[1], 0)
        lax.fori_loop(nu_ref[0], n_blocks, lambda j, carry: (unused_block(j).wait(), carry)[1], 0)

    buf = i % 2
    _slot_fetch(slots_hbm, idx_smem, sem_idx, i, buf).wait()

    @pl.when(i + 1 < pl.num_programs(0))
    def _():
        _slot_fetch(slots_hbm, idx_smem, sem_idx, i + 1, 1 - buf).start()

    def body(r, carry):
        for k in range(TOP_K):
            pltpu.make_async_copy(hp_ref.at[pl.ds(r, 1)], x_hbm.at[pl.ds(idx_smem[buf, k, r], 1)],
                                  sem_rows).start(priority=k % 2)
        return carry

    lax.fori_loop(0, rows, body, 0, unroll=8)
    span = x_hbm.at[pl.ds(0, rows * TOP_K)]
    pltpu.make_async_copy(span, span, sem_rows).wait()


def _expert_kernel(be_ref, nu_ref, x_ref, wg_ref, wu_ref, wd_ref, o_ref, wg_s, wu_s, wd_s):
    i = pl.program_id(0)
    live = i < nu_ref[0]
    new_expert = jnp.logical_or(i == 0, be_ref[i] != be_ref[jnp.maximum(i - 1, 0)])

    @pl.when(jnp.logical_and(live, new_expert))
    def _():
        wg_s[...] = wg_ref[...].astype(BF16)
        wu_s[...] = wu_ref[...].astype(BF16)
        wd_s[...] = wd_ref[...].astype(BF16)

    @pl.when(live)
    def _():
        lo, hi = _unpack_halves(_load_rows(x_ref, 0, x_ref.shape[0]))
        lo, hi = lo.astype(BF16), hi.astype(BF16)
        half = lo.shape[1]
        g = (jnp.dot(lo, wg_s[:half, :], preferred_element_type=F32)
             + jnp.dot(hi, wg_s[half:, :], preferred_element_type=F32))
        u = (jnp.dot(lo, wu_s[:half, :], preferred_element_type=F32)
             + jnp.dot(hi, wu_s[half:, :], preferred_element_type=F32))
        a = (_silu(g) * u).astype(BF16)
        _store_rows(o_ref, _pack_halves(jnp.dot(a, wd_s[...], preferred_element_type=F32)))

    @pl.when(jnp.logical_not(live))
    def _():
        o_ref[...] = jnp.zeros_like(o_ref)


def _combine_kernel(slots_hbm, y_hbm, h_ref, gate_ref, xs_ref, mg_ref, wg_ref, wu_ref, wd_ref, o_ref,
                    idx_smem, rows_buf, sem_idx, sem_rows):
    i = pl.program_id(0)
    rows = h_ref.shape[0]

    @pl.when(i == 0)
    def _():
        _slot_fetch(slots_hbm, idx_smem, sem_idx, 0, 0).start()

    buf = i % 2
    _slot_fetch(slots_hbm, idx_smem, sem_idx, i, buf).wait()

    @pl.when(i + 1 < pl.num_programs(0))
    def _():
        _slot_fetch(slots_hbm, idx_smem, sem_idx, i + 1, 1 - buf).start()

    def body(r, carry):
        for k in range(TOP_K):
            pltpu.make_async_copy(y_hbm.at[pl.ds(idx_smem[buf, k, r], 1)], rows_buf.at[pl.ds(k * rows + r, 1)],
                                  sem_rows).start(priority=k % 2)
        return carry

    lax.fori_loop(0, rows, body, 0, unroll=8)

    h = h_ref[...]
    g = jnp.dot(h, wg_ref[...], preferred_element_type=F32)
    u = jnp.dot(h, wu_ref[...], preferred_element_type=F32)
    a = (_silu(g) * u).astype(BF16)
    y = jnp.dot(a, wd_ref[...], preferred_element_type=F32)

    pltpu.make_async_copy(y_hbm.at[pl.ds(0, rows * TOP_K)], rows_buf, sem_rows).wait()
    half = y.shape[1] // 2
    acc_lo, acc_hi = y[:, :half], y[:, half:]
    gate = gate_ref[...]
    for k in range(TOP_K):
        lo, hi = _unpack_halves(_load_rows(rows_buf, k * rows, rows))
        acc_lo = acc_lo + gate[:, k:k + 1] * lo
        acc_hi = acc_hi + gate[:, k:k + 1] * hi
    mg = mg_ref[...]
    o_ref[:, :half] = xs_ref[:, :half] + mg[:, :half] * acc_lo
    o_ref[:, half:] = xs_ref[:, half:] + mg[:, half:] * acc_hi


def route(h2, w_router, router_bias):
    t, d = h2.shape
    n_tiles = t // ROW_TILE
    n_exp = w_router.shape[1]
    return pl.pallas_call(
        _router_kernel,
        grid=(n_tiles,),
        in_specs=[pl.BlockSpec((ROW_TILE, d), lambda i: (i, 0)),
                  pl.BlockSpec((n_exp, d), lambda i: (0, 0)),
                  pl.BlockSpec((n_exp, 1), lambda i: (0, 0))],
        out_specs=[pl.BlockSpec((TOP_K, ROW_TILE), lambda i: (0, i)),
                   pl.BlockSpec((TOP_K, ROW_TILE), lambda i: (0, i)),
                   pl.BlockSpec((TOP_K, ROW_TILE), lambda i: (0, i)),
                   pl.BlockSpec((n_exp, 1), lambda i: (0, 0))],
        out_shape=[jax.ShapeDtypeStruct((TOP_K, t), jnp.int32), jax.ShapeDtypeStruct((TOP_K, t), F32),
                   jax.ShapeDtypeStruct((TOP_K, t), jnp.int32), jax.ShapeDtypeStruct((n_exp, 1), F32)],
        scratch_shapes=[pltpu.VMEM((n_exp, 1), F32)],
        compiler_params=_cparams("arbitrary"),
        name="router_topk",
    )(h2, jnp.transpose(w_router).astype(BF16), router_bias.astype(F32).reshape(n_exp, 1))


def moe_block(h2, h2_packed, xs, mod, k_gate, tiles_per_row, n_batch,
              w_router, router_bias, w_gate, w_up, w_down, ws_gate, ws_up, ws_down):
    t, d = h2.shape
    row_shape = h2_packed.shape[1:]
    de = ws_gate.shape[1]
    n_tiles = t // ROW_TILE
    eidx, gate_t, rank, cnt = route(h2, w_router, router_bias)
    n_blocks = -(-t * TOP_K // MOE_BLOCK) + N_EXPERTS
    counts = cnt[:, 0].astype(jnp.int32)
    padded = (counts + MOE_BLOCK - 1) // MOE_BLOCK * MOE_BLOCK
    pad_end = jnp.cumsum(padded).astype(jnp.int32)
    pad_start = pad_end - padded
    slot = pad_start[eidx] + rank
    slots = jnp.transpose(slot.reshape(TOP_K, n_tiles, ROW_TILE), (1, 0, 2))
    gate = jnp.transpose(gate_t)
    block_start = jnp.arange(n_blocks, dtype=jnp.int32) * MOE_BLOCK
    block_e = jnp.minimum(jnp.sum(pad_end[None, :] <= block_start[:, None], axis=1), N_EXPERTS - 1).astype(jnp.int32)
    n_used = (pad_end[-1] // MOE_BLOCK).astype(jnp.int32).reshape(1)

    idx_scratch = [pltpu.SMEM((2, TOP_K, ROW_TILE), jnp.int32)]
    x_sorted = pl.pallas_call(
        _dispatch_kernel,
        grid_spec=pltpu.PrefetchScalarGridSpec(
            num_scalar_prefetch=3,
            grid=(n_tiles,),
            in_specs=[pl.BlockSpec(memory_space=pl.ANY),
                      pl.BlockSpec((ROW_TILE,) + row_shape, lambda i, pd, pe, nu: (i, 0, 0)),
                      pl.BlockSpec(memory_space=pl.ANY)],
            out_specs=pl.BlockSpec(memory_space=pl.ANY),
            scratch_shapes=idx_scratch + [pltpu.SemaphoreType.DMA((2,)), pltpu.SemaphoreType.DMA(())],
        ),
        out_shape=jax.ShapeDtypeStruct((n_blocks * MOE_BLOCK,) + row_shape, jnp.uint32),
        compiler_params=_cparams("arbitrary"),
        name="moe_dispatch",
    )(padded, pad_end, n_used, slots, h2_packed, jnp.zeros((MOE_BLOCK,) + row_shape, jnp.uint32))

    y_sorted = pl.pallas_call(
        _expert_kernel,
        grid_spec=pltpu.PrefetchScalarGridSpec(
            num_scalar_prefetch=2,
            grid=(n_blocks,),
            in_specs=[pl.BlockSpec((MOE_BLOCK,) + row_shape, lambda i, be, nu: (jnp.minimum(i, nu[0] - 1), 0, 0)),
                      pl.BlockSpec((None, d, de), lambda i, be, nu: (be[i], 0, 0)),
                      pl.BlockSpec((None, d, de), lambda i, be, nu: (be[i], 0, 0)),
                      pl.BlockSpec((None, de, d), lambda i, be, nu: (be[i], 0, 0))],
            out_specs=pl.BlockSpec((MOE_BLOCK,) + row_shape, lambda i, be, nu: (i, 0, 0)),
            scratch_shapes=[pltpu.VMEM((d, de), BF16), pltpu.VMEM((d, de), BF16), pltpu.VMEM((de, d), BF16)],
        ),
        out_shape=jax.ShapeDtypeStruct((n_blocks * MOE_BLOCK,) + row_shape, jnp.uint32),
        compiler_params=_cparams("arbitrary"),
        name="routed_experts",
    )(block_e, n_used, x_sorted, w_gate, w_up, w_down)

    seg = functools.partial(_seg_of_tile, tiles_per_row=tiles_per_row, n_batch=n_batch)
    return pl.pallas_call(
        _combine_kernel,
        grid=(n_tiles,),
        in_specs=[pl.BlockSpec(memory_space=pl.ANY),
                  pl.BlockSpec(memory_space=pl.ANY),
                  pl.BlockSpec((ROW_TILE, d), lambda i: (i, 0)),
                  pl.BlockSpec((ROW_TILE, TOP_K), lambda i: (i, 0)),
                  pl.BlockSpec((ROW_TILE, d), lambda i: (i, 0)),
                  pl.BlockSpec((None, None, 1, d), lambda i: (seg(i), k_gate, 0, 0)),
                  pl.BlockSpec((d, de), lambda i: (0, 0)),
                  pl.BlockSpec((d, de), lambda i: (0, 0)),
                  pl.BlockSpec((de, d), lambda i: (0, 0))],
        out_specs=pl.BlockSpec((ROW_TILE, d), lambda i: (i, 0)),
        out_shape=jax.ShapeDtypeStruct((t, d), F32),
        scratch_shapes=idx_scratch + [pltpu.VMEM((TOP_K * ROW_TILE,) + row_shape, jnp.uint32),
                                      pltpu.SemaphoreType.DMA((2,)), pltpu.SemaphoreType.DMA(())],
        compiler_params=_cparams("arbitrary"),
        name="moe_combine",
    )(slots, y_sorted, h2, gate, xs, mod, ws_gate.astype(BF16), ws_up.astype(BF16), ws_down.astype(BF16))


def kernel(x, c, ctx, c_ctx, w_ada, b_ada, norm1_g, norm2_g, w_in_ab, s5_lambda_re, s5_lambda_im, s5_log_dt, s5_b_re, s5_b_im, s5_c_re, s5_c_im, s5_d, s5_w_glu, hgrn_lb, w_in_cd, ret_decay_logit, gdn_conv_w, gdn_a_log, gdn_dt_bias, gdn_norm_g, w_out, w_router, router_bias, w_exp_gate, w_exp_up, w_exp_down, w_sh_gate, w_sh_up, w_sh_down, final_norm_g):
    n_batch, n_lat, d = x.shape
    n_ctx = ctx.shape[1]
    n_tok = n_ctx + n_lat
    depth = w_ada.shape[0]
    assert n_ctx % ROW_TILE == 0 and n_lat % ROW_TILE == 0
    tiles_per_row, ctx_tiles = n_tok // ROW_TILE, n_ctx // ROW_TILE
    assert ctx_tiles == 1
    t = n_batch * n_tok

    xs = jnp.concatenate([ctx, x], axis=1).reshape(t, d)
    lb_all = jnp.cumsum(jax.nn.softmax(hgrn_lb.astype(F32), axis=0), axis=0)
    cond = jnp.concatenate([c, c_ctx[None]], axis=0)
    cond = jnp.pad(jax.nn.silu(cond), ((0, 8 - (n_batch + 1)), (0, 0)))
    ones_g = jnp.ones((D_HEAD,), F32)

    for i in range(depth):
        j = i // 2
        mod = matmul(cond, w_ada[i], F32, 8, 1024, "ada_mod")[:n_batch + 1] + b_ada[i]
        mod = mod.reshape(n_batch + 1, 6, 1, d)
        h = norm_mod(xs, norm1_g[i], mod, 0, 1, tiles_per_row, n_batch)
        if i % 2 == 0:
            z = matmul(h, w_in_ab[j].astype(BF16), F32, 512, 1024, "in_proj")
            a_width = s5_d.shape[1]
            b_heads = (z.shape[1] - a_width) // 5 // B_HEAD
            y_a = mixer_s5(z, n_batch, n_tok, n_ctx, s5_lambda_re[j], s5_lambda_im[j], s5_log_dt[j],
                           s5_b_re[j], s5_b_im[j], s5_c_re[j], s5_c_im[j], s5_d[j], s5_w_glu[j])
            o_f, o_b = mixer_gla(z, n_batch, n_tok, n_ctx, a_width, lb_all[j], b_heads, B_HEAD)
            gcb = (a_width + 4 * b_heads * B_HEAD) // (b_heads * B_HEAD)
            y_b = post_norm_gate(o_f, o_b, z, gcb, B_HEAD, jnp.ones((B_HEAD,), F32))
            m1, m2 = y_a, y_b
        else:
            cd_cols = w_in_cd.shape[2]
            cd_pad = -(-cd_cols // 1024) * 1024
            w_cd = jnp.pad(w_in_cd[j].astype(BF16), ((0, 0), (0, cd_pad - cd_cols)))
            z = matmul(h, w_cd, F32, 512, 1024, "in_proj")
            c_heads = ret_decay_logit.shape[2]
            d_heads = gdn_a_log.shape[2]
            o_f, o_b = mixer_retention(z, n_batch, n_tok, n_ctx, ret_decay_logit[j], c_heads)
            y_c = post_norm_gate(o_f, o_b, z, (2 * c_heads * C_QK + c_heads * C_V) // (c_heads * C_V), C_V,
                                 jnp.ones((C_V,), F32))
            col0 = 2 * c_heads * C_QK + 2 * c_heads * C_V
            o_f, o_b = mixer_gdn(z, n_batch, n_tok, n_ctx, col0, gdn_conv_w[j], gdn_a_log[j], gdn_dt_bias[j],
                                 d_heads, D_HEAD)
            y_d = post_norm_gate(o_f, o_b, z, (col0 + 3 * d_heads * D_HEAD) // (d_heads * D_HEAD), D_HEAD,
                                 gdn_norm_g[j])
            m1, m2 = y_c, y_d
        xs = out_proj(m1, m2, w_out[i].astype(BF16), xs, mod, 2, tiles_per_row, n_batch)
        h2, h2_packed = norm_mod(xs, norm2_g[i], mod, 3, 4, tiles_per_row, n_batch, packed=True)
        xs = moe_block(h2, h2_packed, xs, mod, 5, tiles_per_row, n_batch, w_router[i], router_bias[i],
                       w_exp_gate[i], w_exp_up[i], w_exp_down[i], w_sh_gate[i], w_sh_up[i], w_sh_down[i])
    out = final_norm(xs, final_norm_g, n_batch, tiles_per_row, ctx_tiles)
    return out.reshape(n_batch, n_lat, d)
```

```python
import functools

import numpy as np
import jax
import jax.numpy as jnp
from jax import lax
from jax.experimental import pallas as pl
from jax.experimental.pallas import tpu as pltpu

F32 = jnp.float32
BF16 = jnp.bfloat16

EPS = 1e-6
GRID_W = 64
CHUNK = 64
S5_L = 32
S5_H = 16
S5_P = 64
B_HEAD = 128
C_QK = 128
C_V = 256
D_HEAD = 128
ROPE_BASE = 10000.0
N_EXPERTS = 64
N_GROUPS = 8
TOPK_GROUPS = 4
TOP_K = 8
ROUTED_SCALE = 2.5
MOE_BLOCK = 256
ROW_TILE = 256
VMEM_LIMIT_V7X = 56 * 1024 * 1024
EXP_CLAMP = 80.0


def _cparams(*sem):
    return pltpu.CompilerParams(dimension_semantics=sem, vmem_limit_bytes=VMEM_LIMIT_V7X)


def _silu(x):
    return x * jax.nn.sigmoid(x)


def _dot(a, b):
    return jnp.dot(a.astype(BF16), b.astype(BF16), preferred_element_type=F32)


def _dot_nt(a, b):
    return lax.dot_general(a.astype(BF16), b.astype(BF16), (((1,), (1,)), ((), ())), preferred_element_type=F32)


def _dot_tn(a, b):
    return lax.dot_general(a.astype(BF16), b.astype(BF16), (((0,), (0,)), ((), ())), preferred_element_type=F32)


def _split3(x):
    x1 = x.astype(BF16)
    r = x - x1.astype(F32)
    x2 = r.astype(BF16)
    x3 = (r - x2.astype(F32)).astype(BF16)
    return x1, x2, x3


def _tri_dot(tri, x):
    return sum(jnp.dot(tri, p, preferred_element_type=F32) for p in _split3(x))


def _dot_tri(x, tri):
    return sum(jnp.dot(p, tri, preferred_element_type=F32) for p in _split3(x))


def _tri_mask(n, lower):
    r = lax.broadcasted_iota(jnp.int32, (n, n), 0)
    c = lax.broadcasted_iota(jnp.int32, (n, n), 1)
    return (r >= c) if lower else (r <= c)


def _seg_of_tile(i, tiles_per_row, n_batch):
    return jnp.where(i % tiles_per_row == 0, n_batch, i // tiles_per_row)


def _pack_halves(y):
    half = y.shape[1] // 2
    lo = lax.bitcast_convert_type(y[:, :half].astype(BF16).astype(F32), jnp.uint32) >> 16
    hi = lax.bitcast_convert_type(y[:, half:].astype(BF16).astype(F32), jnp.uint32) & jnp.uint32(0xFFFF0000)
    return hi | lo


def _unpack_halves(w):
    lo = lax.bitcast_convert_type(w << 16, F32)
    hi = lax.bitcast_convert_type(w & jnp.uint32(0xFFFF0000), F32)
    return lo, hi


LANES = 128


def _store_rows(ref, packed):
    pieces = jnp.stack([packed[:, s * LANES:(s + 1) * LANES] for s in range(ref.shape[1])], axis=0)
    ref[...] = pltpu.einshape("stl->tsl", pieces)


def _load_rows(ref, row0, rows):
    x = pltpu.einshape("tsl->stl", ref[row0:row0 + rows])
    return jnp.concatenate([x[s] for s in range(ref.shape[1])], axis=1)


def _norm_mod_kernel(x_ref, g_ref, shift_ref, scale_ref, o_ref, *packed_ref):
    x = x_ref[...]
    y = x * lax.rsqrt(jnp.mean(x * x, axis=-1, keepdims=True) + EPS) * g_ref[...]
    y = y * (1.0 + scale_ref[...]) + shift_ref[...]
    o_ref[...] = y.astype(o_ref.dtype)
    if packed_ref:
        _store_rows(packed_ref[0], _pack_halves(y))


def norm_mod(xs, g, mod, k_shift, k_scale, tiles_per_row, n_batch, packed=False):
    t, d = xs.shape
    seg = functools.partial(_seg_of_tile, tiles_per_row=tiles_per_row, n_batch=n_batch)
    out_specs = [pl.BlockSpec((ROW_TILE, d), lambda i: (i, 0))]
    out_shape = [jax.ShapeDtypeStruct((t, d), BF16)]
    if packed:
        out_specs.append(pl.BlockSpec((ROW_TILE, d // 2 // LANES, LANES), lambda i: (i, 0, 0)))
        out_shape.append(jax.ShapeDtypeStruct((t, d // 2 // LANES, LANES), jnp.uint32))
    res = pl.pallas_call(
        _norm_mod_kernel,
        grid=(t // ROW_TILE,),
        in_specs=[
            pl.BlockSpec((ROW_TILE, d), lambda i: (i, 0)),
            pl.BlockSpec((1, d), lambda i: (0, 0)),
            pl.BlockSpec((None, None, 1, d), lambda i: (seg(i), k_shift, 0, 0)),
            pl.BlockSpec((None, None, 1, d), lambda i: (seg(i), k_scale, 0, 0)),
        ],
        out_specs=out_specs,
        out_shape=out_shape,
        compiler_params=_cparams("parallel"),
        name="norm_mod",
    )(xs, g.reshape(1, d), mod, mod)
    return res if packed else res[0]


def _final_norm_kernel(x_ref, g_ref, o_ref):
    x = x_ref[...]
    o_ref[...] = x * lax.rsqrt(jnp.mean(x * x, axis=-1, keepdims=True) + EPS) * g_ref[...]


def final_norm(xs, g, n_batch, tiles_per_row, ctx_tiles):
    t, d = xs.shape
    lat_tiles = tiles_per_row - ctx_tiles
    return pl.pallas_call(
        _final_norm_kernel,
        grid=(n_batch, lat_tiles),
        in_specs=[
            pl.BlockSpec((ROW_TILE, d), lambda b, i: (b * tiles_per_row + ctx_tiles + i, 0)),
            pl.BlockSpec((1, d), lambda b, i: (0, 0)),
        ],
        out_specs=pl.BlockSpec((ROW_TILE, d), lambda b, i: (b * lat_tiles + i, 0)),
        out_shape=jax.ShapeDtypeStruct((n_batch * lat_tiles * ROW_TILE, d), F32),
        compiler_params=_cparams("parallel", "parallel"),
        name="final_norm",
    )(xs, g.reshape(1, d))


def _mm_kernel(a_ref, w_ref, o_ref):
    o_ref[...] = jnp.dot(a_ref[...].astype(BF16), w_ref[...].astype(BF16),
                         preferred_element_type=F32).astype(o_ref.dtype)


def matmul(a, w, out_dtype, tm, tn, name):
    m, k = a.shape
    n = w.shape[1]
    return pl.pallas_call(
        _mm_kernel,
        grid=(n // tn, m // tm),
        in_specs=[pl.BlockSpec((tm, k), lambda j, i: (i, 0)),
                  pl.BlockSpec((k, tn), lambda j, i: (0, j))],
        out_specs=pl.BlockSpec((tm, tn), lambda j, i: (i, j)),
        out_shape=jax.ShapeDtypeStruct((m, n), out_dtype),
        compiler_params=_cparams("parallel", "parallel"),
        name=name,
    )(a, w)


def _out_proj_kernel(a1_ref, a2_ref, w1_ref, w2_ref, res_ref, gate_ref, o_ref):
    y = jnp.dot(a1_ref[...], w1_ref[...], preferred_element_type=F32)
    y += jnp.dot(a2_ref[...], w2_ref[...], preferred_element_type=F32)
    o_ref[...] = res_ref[...] + gate_ref[...] * y


def out_proj(a1, a2, w, xs, mod, k_gate, tiles_per_row, n_batch, tn=1024):
    t, d = xs.shape
    k1, k2 = a1.shape[1], a2.shape[1]
    assert k1 == k2
    seg = functools.partial(_seg_of_tile, tiles_per_row=tiles_per_row, n_batch=n_batch)
    return pl.pallas_call(
        _out_proj_kernel,
        grid=(d // tn, t // ROW_TILE),
        in_specs=[
            pl.BlockSpec((ROW_TILE, k1), lambda j, i: (i, 0)),
            pl.BlockSpec((ROW_TILE, k2), lambda j, i: (i, 0)),
            pl.BlockSpec((k1, tn), lambda j, i: (0, j)),
            pl.BlockSpec((k2, tn), lambda j, i: (1, j)),
            pl.BlockSpec((ROW_TILE, tn), lambda j, i: (i, j)),
            pl.BlockSpec((None, None, 1, tn), lambda j, i: (seg(i), k_gate, 0, j)),
        ],
        out_specs=pl.BlockSpec((ROW_TILE, tn), lambda j, i: (i, j)),
        out_shape=jax.ShapeDtypeStruct((t, d), F32),
        compiler_params=_cparams("parallel", "parallel"),
        name="out_proj",
    )(a1, a2, w, w, xs, mod)


def _post_kernel(of_ref, ob_ref, gate_ref, ng_ref, o_ref, *, head_dim):
    o = of_ref[...] + ob_ref[...]
    g = gate_ref[...].astype(F32)
    width = o.shape[1]
    for h in range(width // head_dim):
        sl = slice(h * head_dim, (h + 1) * head_dim)
        oh = o[:, sl]
        y = oh * lax.rsqrt(jnp.mean(oh * oh, axis=-1, keepdims=True) + EPS) * ng_ref[...]
        o_ref[:, sl] = (y * _silu(g[:, sl])).astype(o_ref.dtype)


def post_norm_gate(o_f, o_b, z, gate_col_block, head_dim, norm_g):
    t, width = o_f.shape
    return pl.pallas_call(
        functools.partial(_post_kernel, head_dim=head_dim),
        grid=(t // ROW_TILE,),
        in_specs=[
            pl.BlockSpec((ROW_TILE, width), lambda i: (i, 0)),
            pl.BlockSpec((ROW_TILE, width), lambda i: (i, 0)),
            pl.BlockSpec((ROW_TILE, width), lambda i: (i, gate_col_block)),
            pl.BlockSpec((1, head_dim), lambda i: (0, 0)),
        ],
        out_specs=pl.BlockSpec((ROW_TILE, width), lambda i: (i, 0)),
        out_shape=jax.ShapeDtypeStruct((t, width), BF16),
        compiler_params=_cparams("parallel"),
        name="post_norm_gate",
    )(o_f, o_b, z, norm_g.reshape(1, head_dim).astype(F32))


def _bwd_chunk(j, ctx_chunks, n_chunks):
    return jnp.where(j < ctx_chunks, ctx_chunks - 1 - j, n_chunks + ctx_chunks - 1 - j)


def _s5_direction_tables(lam_re, lam_im, log_dt, b_re, b_im, c_re, c_im, reverse):
    hi = lax.Precision.HIGHEST
    ln = S5_L
    lam_re, lam_im, b_re, b_im, c_re, c_im = (p.astype(F32) for p in (lam_re, lam_im, b_re, b_im, c_re, c_im))
    dt = jnp.exp(log_dt.astype(F32))[:, None]
    mag = jnp.exp(lam_re * dt)
    ab_re, ab_im = mag * jnp.cos(lam_im * dt), mag * jnp.sin(lam_im * dt)
    den = lam_re * lam_re + lam_im * lam_im
    fr = ((ab_re - 1) * lam_re + ab_im * lam_im) / den
    fi = (ab_im * lam_re - (ab_re - 1) * lam_im) / den
    bb_re = fr[..., None] * b_re - fi[..., None] * b_im
    bb_im = fr[..., None] * b_im + fi[..., None] * b_re
    tau = jnp.arange(ln + 1, dtype=F32)[:, None, None]
    pw = jnp.exp(tau * (lam_re * dt))
    pr, pi = pw * jnp.cos(tau * (lam_im * dt)), pw * jnp.sin(tau * (lam_im * dt))
    abr = pr[..., None] * bb_re - pi[..., None] * bb_im
    abi = pr[..., None] * bb_im + pi[..., None] * bb_re
    kern = (jnp.einsum('ghp,tgpk->tghk', c_re, abr[:ln], precision=hi)
            - jnp.einsum('ghp,tgpk->tghk', c_im, abi[:ln], precision=hi))
    pos = jnp.arange(ln)
    lag = (pos[None, :] - pos[:, None]) if not reverse else (pos[:, None] - pos[None, :])
    toe = jnp.where((lag >= 0)[:, :, None, None, None], kern[jnp.clip(lag, 0, ln - 1)], 0.0)
    g = lam_re.shape[0]
    intra = jnp.transpose(toe, (2, 0, 4, 1, 3)).reshape(g, ln * S5_H, ln * S5_H)
    pw_in = (ln - 1 - pos) if not reverse else pos
    inj = jnp.concatenate([abr[pw_in], abi[pw_in]], axis=2)
    inject = jnp.transpose(inj, (1, 0, 3, 2)).reshape(g, ln * S5_H, 2 * S5_P)
    pw_out = (pos + 1) if not reverse else (ln - pos)
    w_re = c_re[None] * pr[pw_out][:, :, None, :] - c_im[None] * pi[pw_out][:, :, None, :]
    w_im = -(c_re[None] * pi[pw_out][:, :, None, :] + c_im[None] * pr[pw_out][:, :, None, :])
    readout = jnp.transpose(jnp.concatenate([w_re, w_im], axis=3), (1, 3, 0, 2)).reshape(g, 2 * S5_P, ln * S5_H)
    decay = jnp.stack([pr[ln], pi[ln]])
    return intra, inject, readout, decay


def _s5_in_kernel(u_ref, w_ref, yi_ref, s_ref):
    r = jnp.dot(u_ref[...], w_ref[...], preferred_element_type=F32)
    n_intra = yi_ref.shape[-1]
    yi_ref[...] = r[:, :n_intra]
    s_ref[...] = r[:, n_intra:]


def _s5_scan_kernel(a_ref, sf_ref, sb_ref, of_ref, ob_ref, st_ref):
    @pl.when(pl.program_id(1) == 0)
    def _():
        st_ref[...] = jnp.zeros_like(st_ref)

    steps = sf_ref.shape[0]
    for d, (s_ref, o_ref) in enumerate(((sf_ref, of_ref), (sb_ref, ob_ref))):
        ar, ai = a_ref[2 * d], a_ref[2 * d + 1]
        sr, si = st_ref[2 * d], st_ref[2 * d + 1]
        for q in range(steps):
            r = q if d == 0 else steps - 1 - q
            o_ref[r, 0] = sr
            o_ref[r, 1] = si
            sr, si = (ar * sr - ai * si + s_ref[r, 0], ar * si + ai * sr + s_ref[r, 1])
        st_ref[2 * d] = sr
        st_ref[2 * d + 1] = si


def _s5_out_kernel(yi_ref, st_ref, w_ref, u_ref, d_ref, o_ref):
    y = yi_ref[...] + jnp.dot(st_ref[...].astype(BF16), w_ref[...], preferred_element_type=F32)
    o_ref[...] = (y + d_ref[...] * u_ref[...].astype(F32)).astype(o_ref.dtype)


def _glu_kernel(y_ref, w_ref, o_ref):
    y = jax.nn.gelu(y_ref[...].astype(F32))
    o_ref[...] = (y * jax.nn.sigmoid(jnp.dot(y.astype(BF16), w_ref[...], preferred_element_type=F32))
                  ).astype(o_ref.dtype)


def mixer_s5(z, n_batch, n_tok, n_ctx, lam_re, lam_im, log_dt, b_re, b_im, c_re, c_im, d_skip, w_glu):
    t = z.shape[0]
    g = lam_re.shape[1]
    width = g * S5_H
    lh = S5_L * S5_H
    rows = t // S5_L
    tabs = [_s5_direction_tables(lam_re[d], lam_im[d], log_dt[d], b_re[d], b_im[d], c_re[d], c_im[d], d == 1)
            for d in range(2)]
    w_in = jnp.concatenate([tabs[0][0] + tabs[1][0], tabs[0][1], tabs[1][1]], axis=2).astype(BF16)
    w_st = jnp.concatenate([tabs[0][2], tabs[1][2]], axis=1).astype(BF16)
    decay = jnp.concatenate([tabs[0][3], tabs[1][3]], axis=0).reshape(4, g * S5_P // 128, 128)
    u = z[:, :width].reshape(rows, S5_L, g, S5_H)
    u_g = jnp.transpose(u, (2, 0, 1, 3)).reshape(g, rows, lh).astype(BF16)

    n_st = 4 * S5_P
    yi, s_in = pl.pallas_call(
        _s5_in_kernel,
        grid=(g,),
        in_specs=[pl.BlockSpec((None, rows, lh), lambda i: (i, 0, 0)),
                  pl.BlockSpec((None, lh, lh + n_st), lambda i: (i, 0, 0))],
        out_specs=[pl.BlockSpec((None, rows, lh), lambda i: (i, 0, 0)),
                   pl.BlockSpec((None, rows, n_st), lambda i: (i, 0, 0))],
        out_shape=[jax.ShapeDtypeStruct((g, rows, lh), F32), jax.ShapeDtypeStruct((g, rows, n_st), F32)],
        compiler_params=_cparams("parallel"),
        name="s5_in",
    )(u_g, w_in)

    gp = g * S5_P // 128
    s_scan = jnp.transpose(s_in.reshape(g, rows, 4, S5_P), (1, 2, 0, 3)).reshape(rows, 4, gp, 128)
    blocks_per_row = n_tok // S5_L
    cb = n_ctx // S5_L
    assert blocks_per_row % cb == 0
    nblk = blocks_per_row // cb
    bwd = lambda j: jnp.where(j == 0, 0, nblk - j)
    st_f, st_b = pl.pallas_call(
        _s5_scan_kernel,
        grid=(n_batch, nblk),
        in_specs=[pl.BlockSpec((4, gp, 128), lambda b, j: (0, 0, 0)),
                  pl.BlockSpec((cb, 2, gp, 128), lambda b, j: (b * nblk + j, 0, 0, 0)),
                  pl.BlockSpec((cb, 2, gp, 128), lambda b, j: (b * nblk + bwd(j), 1, 0, 0))],
        out_specs=[pl.BlockSpec((cb, 2, gp, 128), lambda b, j: (b * nblk + j, 0, 0, 0)),
                   pl.BlockSpec((cb, 2, gp, 128), lambda b, j: (b * nblk + bwd(j), 0, 0, 0))],
        out_shape=[jax.ShapeDtypeStruct((rows, 2, gp, 128), F32)] * 2,
        scratch_shapes=[pltpu.VMEM((4, gp, 128), F32)],
        compiler_params=_cparams("parallel", "arbitrary"),
        name="s5_scan",
    )(decay, s_scan, s_scan)
    st = jnp.concatenate([st_f, st_b], axis=1).reshape(rows, 4, g, S5_P)
    st_g = jnp.transpose(st, (2, 0, 1, 3)).reshape(g, rows, n_st)

    d_vec = jnp.tile(d_skip.astype(F32).reshape(g, 1, S5_H), (1, S5_L, 1)).reshape(g, 1, lh)
    y_g = pl.pallas_call(
        _s5_out_kernel,
        grid=(g,),
        in_specs=[pl.BlockSpec((None, rows, lh), lambda i: (i, 0, 0)),
                  pl.BlockSpec((None, rows, n_st), lambda i: (i, 0, 0)),
                  pl.BlockSpec((None, n_st, lh), lambda i: (i, 0, 0)),
                  pl.BlockSpec((None, rows, lh), lambda i: (i, 0, 0)),
                  pl.BlockSpec((None, 1, lh), lambda i: (i, 0, 0))],
        out_specs=pl.BlockSpec((None, rows, lh), lambda i: (i, 0, 0)),
        out_shape=jax.ShapeDtypeStruct((g, rows, lh), BF16),
        compiler_params=_cparams("parallel"),
        name="s5_out",
    )(yi, st_g, w_st, u_g, d_vec)
    y = jnp.transpose(y_g.reshape(g, rows, S5_L, S5_H), (1, 2, 0, 3)).reshape(t, width)

    return pl.pallas_call(
        _glu_kernel,
        grid=(t // ROW_TILE,),
        in_specs=[pl.BlockSpec((ROW_TILE, width), lambda i: (i, 0)),
                  pl.BlockSpec((width, width), lambda i: (0, 0))],
        out_specs=pl.BlockSpec((ROW_TILE, width), lambda i: (i, 0)),
        out_shape=jax.ShapeDtypeStruct((t, width), BF16),
        compiler_params=_cparams("parallel"),
        name="s5_glu",
    )(y, w_glu.astype(BF16))


def _gla_kernel(lb_ref, qf_ref, ff_ref, vf_ref, qb_ref, fb_ref, vb_ref, of_ref, ob_ref, st_ref, *, n_heads, hd):
    @pl.when(pl.program_id(1) == 0)
    def _():
        st_ref[...] = jnp.zeros_like(st_ref)

    c = qf_ref.shape[0]
    items = []
    for d, (q_ref, f_ref, v_ref, o_ref) in enumerate(((qf_ref, ff_ref, vf_ref, of_ref),
                                                       (qb_ref, fb_ref, vb_ref, ob_ref))):
        lower = d == 0
        mask = _tri_mask(c, lower)
        tri = mask.astype(BF16)
        lb = lb_ref[d]
        sig = jax.nn.sigmoid(f_ref[...])
        kk = (1.0 - lb) * (1.0 - sig)
        logf = jnp.log(lb + (1.0 - lb) * sig)
        b = _tri_dot(tri, logf)
        b_end = b[c - 1:c, :] if lower else b[0:1, :]
        qt = (_silu(q_ref[...]) * jnp.exp(b)).astype(BF16)
        kt = (kk * jnp.exp(jnp.minimum(-b, EXP_CLAMP))).astype(BF16)
        kend = (kk * jnp.exp(b_end - b)).astype(BF16)
        dec = jnp.exp(b_end)
        v = v_ref[...].astype(BF16)
        for h in range(n_heads):
            sl = slice(h * hd, (h + 1) * hd)
            items.append((d, h, sl, o_ref, mask, qt[:, sl], kt[:, sl], kend[:, sl], v[:, sl], dec[:, sl], st_ref[d, h]))
    att = [_dot_nt(it[5], it[6]) for it in items]
    inter = [_dot_nt(it[5], it[10]) for it in items]
    upd = [_dot_tn(it[8], it[7]) for it in items]
    intra = [_dot(jnp.where(it[4], a, 0.0), it[8]) for it, a in zip(items, att)]
    for it, oi, os, up in zip(items, intra, inter, upd):
        d, h, sl, o_ref = it[:4]
        o_ref[:, sl] = oi + os
        st_ref[d, h] = it[9] * it[10] + up


def mixer_gla(z, n_batch, n_tok, n_ctx, col0, lb, n_heads, hd):
    width = n_heads * hd
    z3 = z.reshape(n_batch, n_tok, z.shape[1])
    nc, cc = n_tok // CHUNK, n_ctx // CHUNK
    cb0 = col0 // width
    bwd = functools.partial(_bwd_chunk, ctx_chunks=cc, n_chunks=nc)
    blk = (None, CHUNK, width)
    fw = lambda k: pl.BlockSpec(blk, lambda b, j: (b, j, cb0 + k))
    bw = lambda k: pl.BlockSpec(blk, lambda b, j: (b, bwd(j), cb0 + k))
    o_f, o_b = pl.pallas_call(
        functools.partial(_gla_kernel, n_heads=n_heads, hd=hd),
        grid=(n_batch, nc),
        in_specs=[pl.BlockSpec((2, 1, width), lambda b, j: (0, 0, 0)),
                  fw(0), fw(1), fw(3), bw(0), bw(2), bw(3)],
        out_specs=[pl.BlockSpec(blk, lambda b, j: (b, j, 0)),
                   pl.BlockSpec(blk, lambda b, j: (b, bwd(j), 0))],
        out_shape=[jax.ShapeDtypeStruct((n_batch, n_tok, width), F32)] * 2,
        scratch_shapes=[pltpu.VMEM((2, n_heads, hd, hd), F32)],
        compiler_params=_cparams("parallel", "arbitrary"),
        name="gla_scan",
    )(lb.reshape(2, 1, width).astype(F32), z3, z3, z3, z3, z3, z3)
    return o_f.reshape(-1, width), o_b.reshape(-1, width)


def _rotary_tables(n_tok, n_ctx):
    n_lat = n_tok - n_ctx
    rows = n_lat // GRID_W
    row = jnp.repeat(jnp.arange(rows, dtype=F32), GRID_W)
    col = jnp.tile(jnp.arange(GRID_W, dtype=F32), rows)
    n_freq = C_QK // 4
    inv = ROPE_BASE ** (-jnp.arange(n_freq, dtype=F32) / n_freq)
    ang = jnp.concatenate([row[:, None] * inv, col[:, None] * inv], axis=-1)
    ang = jnp.concatenate([jnp.zeros((n_ctx, C_QK // 2), F32), ang], axis=0)
    cos, sin = jnp.cos(ang), jnp.sin(ang)
    return jnp.concatenate([cos, cos], axis=-1), jnp.concatenate([-sin, sin], axis=-1)


def _ret_kernel(cdec_ref, dmat_ref, qdec_ref, kdec_ref,
                qf_ref, kf_ref, vf_ref, cf_ref, sf_ref, qb_ref, kb_ref, vb_ref, cb_ref, sb_ref,
                of_ref, ob_ref, st_ref, *, n_heads):
    @pl.when(pl.program_id(1) == 0)
    def _():
        st_ref[...] = jnp.zeros_like(st_ref)

    half = C_QK // 2
    items = []
    for d, (q_ref, k_ref, v_ref, cos_ref, sin_ref, o_ref) in enumerate(
            ((qf_ref, kf_ref, vf_ref, cf_ref, sf_ref, of_ref), (qb_ref, kb_ref, vb_ref, cb_ref, sb_ref, ob_ref))):
        cos, sin = cos_ref[...], sin_ref[...]
        for h in range(n_heads):
            qs = slice(h * C_QK, (h + 1) * C_QK)
            vs = slice(h * C_V, (h + 1) * C_V)
            qh, kh = q_ref[:, qs].astype(F32), k_ref[:, qs].astype(F32)
            qh = (qh * cos + pltpu.roll(qh, half, axis=1) * sin) * (C_QK ** -0.5)
            kh = kh * cos + pltpu.roll(kh, half, axis=1) * sin
            items.append((d, h, vs, o_ref, qh.astype(BF16), kh.astype(BF16), (qh * qdec_ref[d, h]).astype(BF16),
                          (kh * kdec_ref[d, h]).astype(BF16), v_ref[:, vs].astype(BF16), st_ref[d, h]))
    att = [_dot_nt(it[4], it[5]) for it in items]
    inter = [_dot(it[6], it[9]) for it in items]
    upd = [_dot_tn(it[7], it[8]) for it in items]
    intra = [_dot(a * dmat_ref[it[0], it[1]], it[8]) for it, a in zip(items, att)]
    for it, oi, os, up in zip(items, intra, inter, upd):
        d, h, vs, o_ref = it[:4]
        o_ref[:, vs] = oi + os
        st_ref[d, h] = cdec_ref[d, h] * it[9] + up


def mixer_retention(z, n_batch, n_tok, n_ctx, decay_logit, n_heads):
    qw, vw = n_heads * C_QK, n_heads * C_V
    z3 = z.reshape(n_batch, n_tok, z.shape[1])
    nc, cc = n_tok // CHUNK, n_ctx // CHUNK
    bwd = functools.partial(_bwd_chunk, ctx_chunks=cc, n_chunks=nc)
    log_gamma = jax.nn.log_sigmoid(decay_logit.astype(F32))[:, :, None, None]
    idx = jnp.arange(CHUNK, dtype=F32)
    diff = idx[:, None] - idx[None, :]
    dmat_f = jnp.where(diff >= 0, jnp.exp(jnp.maximum(diff, 0.0) * log_gamma[0]), 0.0)
    dmat_b = jnp.where(diff <= 0, jnp.exp(jnp.maximum(-diff, 0.0) * log_gamma[1]), 0.0)
    dmat = jnp.stack([dmat_f, dmat_b])
    ones = jnp.ones((1, 1, 1, C_QK), F32)
    pos_f, pos_b = idx[None, None, :, None], (CHUNK - 1 - idx)[None, None, :, None]
    lg = log_gamma
    qdec = jnp.concatenate([jnp.exp((pos_f + 1) * lg[0:1]), jnp.exp((pos_b + 1) * lg[1:2])]) * ones
    kdec = jnp.concatenate([jnp.exp((CHUNK - 1 - pos_f) * lg[0:1]), jnp.exp((CHUNK - 1 - pos_b) * lg[1:2])]) * ones
    cdec = jnp.exp(CHUNK * log_gamma[:, :, 0, 0])
    cos2, sin2 = _rotary_tables(n_tok, n_ctx)
    full = lambda shape: pl.BlockSpec(shape, lambda b, j: (0,) * len(shape))
    tab = lambda order: pl.BlockSpec((CHUNK, C_QK), lambda b, j: (order(j), 0))
    ident = lambda j: j
    vcb = 2 * qw // vw
    assert vcb * vw == 2 * qw
    def specs(order):
        return [pl.BlockSpec((None, CHUNK, qw), lambda b, j: (b, order(j), 0)),
                pl.BlockSpec((None, CHUNK, qw), lambda b, j: (b, order(j), 1)),
                pl.BlockSpec((None, CHUNK, vw), lambda b, j: (b, order(j), vcb)),
                tab(order), tab(order)]
    o_f, o_b = pl.pallas_call(
        functools.partial(_ret_kernel, n_heads=n_heads),
        grid=(n_batch, nc),
        in_specs=[pl.BlockSpec(memory_space=pltpu.SMEM), full((2, n_heads, CHUNK, CHUNK)),
                  full((2, n_heads, CHUNK, C_QK)), full((2, n_heads, CHUNK, C_QK))] + specs(ident) + specs(bwd),
        out_specs=[pl.BlockSpec((None, CHUNK, vw), lambda b, j: (b, j, 0)),
                   pl.BlockSpec((None, CHUNK, vw), lambda b, j: (b, bwd(j), 0))],
        out_shape=[jax.ShapeDtypeStruct((n_batch, n_tok, vw), F32)] * 2,
        scratch_shapes=[pltpu.VMEM((2, n_heads, C_QK, C_V), F32)],
        compiler_params=_cparams("parallel", "arbitrary"),
        name="retention_scan",
    )(cdec, dmat, qdec, kdec, z3, z3, z3, cos2, sin2, z3, z3, z3, cos2, sin2)
    return o_f.reshape(-1, vw), o_b.reshape(-1, vw)


def _gdn_conv_kernel(prev_ref, cur_ref, next_ref, w_ref, o_ref, *, tiles_per_row, ctx_tiles, n_heads, hd):
    i = pl.program_id(0)
    r = i % tiles_per_row
    first = jnp.logical_or(r == 0, r == ctx_tiles)
    last = jnp.logical_or(r == ctx_tiles - 1, r == tiles_per_row - 1)
    x = cur_ref[...].astype(F32)
    rows = x.shape[0]
    rid = lax.broadcasted_iota(jnp.int32, x.shape, 0)
    hp = prev_ref.shape[0]
    x_prev = jnp.where(first, 0.0, prev_ref[hp - 1:hp, :].astype(F32))
    x_next = jnp.where(last, 0.0, next_ref[0:1, :].astype(F32))
    left = jnp.where(rid == 0, x_prev, pltpu.roll(x, 1, axis=0))
    right = jnp.where(rid == rows - 1, x_next, pltpu.roll(x, rows - 1, axis=0))
    w = w_ref[...]
    y = _silu(left * w[0:1, :] + x * w[1:2, :] + right * w[2:3, :])
    width = n_heads * hd
    for h in range(3 * n_heads):
        sl = slice(h * hd, (h + 1) * hd)
        yh = y[:, sl]
        if h < 2 * n_heads:
            yh = yh * lax.rsqrt(jnp.sum(yh * yh, axis=-1, keepdims=True) + EPS)
            if h < n_heads:
                yh = yh * (hd ** -0.5)
        o_ref[:, sl] = yh.astype(o_ref.dtype)


def _gdn_kernel(qf_ref, kf_ref, vf_ref, cf_ref, rf_ref, qb_ref, kb_ref, vb_ref, cb_ref, rb_ref,
                of_ref, ob_ref, st_ref, *, n_heads, hd):
    @pl.when(pl.program_id(1) == 0)
    def _():
        st_ref[...] = jnp.zeros_like(st_ref)

    c = qf_ref.shape[0]
    eye = (lax.broadcasted_iota(jnp.int32, (c, c), 0) == lax.broadcasted_iota(jnp.int32, (c, c), 1)).astype(F32)
    items = []
    for d, (q_ref, k_ref, v_ref, col_ref, row_ref, o_ref) in enumerate(
            ((qf_ref, kf_ref, vf_ref, cf_ref, rf_ref, of_ref), (qb_ref, kb_ref, vb_ref, cb_ref, rb_ref, ob_ref))):
        lower = d == 0
        incl = _tri_mask(c, lower)
        strict = jnp.logical_and(incl, eye == 0.0)
        col = col_ref[...]
        row = row_ref[...]
        b_cols = _tri_dot(incl.astype(BF16), col)
        b_rows = _dot_tri(row, _tri_mask(c, not lower).astype(BF16))
        for h in range(n_heads):
            sl = slice(h * hd, (h + 1) * hd)
            ib, ig = d * n_heads + h, (2 + d) * n_heads + h
            be = col[:, ib:ib + 1]
            bc = b_cols[:, ig:ig + 1]
            br = b_rows[ig:ig + 1, :]
            b_end = bc[c - 1:c, :] if lower else bc[0:1, :]
            gam_i = jnp.exp(jnp.where(incl, bc - br, -1e30))
            gam_s = jnp.where(strict, gam_i, 0.0)
            qh, kh, vh = q_ref[:, sl], k_ref[:, sl], v_ref[:, sl]
            eb = jnp.exp(bc)
            rhs = jnp.concatenate([be * vh.astype(F32), (be * eb) * kh.astype(F32)], axis=1).astype(BF16)
            items.append(dict(d=d, h=h, sl=sl, o_ref=o_ref, be=be, gam_i=gam_i, gam_s=gam_s, q=qh, k=kh, rhs=rhs,
                              qe=(qh.astype(F32) * eb).astype(BF16), dec=jnp.exp(b_end),
                              ke=(kh.astype(F32) * jnp.exp(b_end - bc)).astype(BF16), s=st_ref[d, h]))
    kk = [_dot_nt(it["k"], it["k"]) for it in items]
    qk = [_dot_nt(it["q"], it["k"]) for it in items]
    qs = [_dot(it["qe"], it["s"]) for it in items]
    ns = [it["be"] * a * it["gam_s"] for it, a in zip(items, kk)]
    invs = [eye - n for n in ns]
    ps = [_dot(n, n) for n in ns]
    levels = int(np.log2(c)) - 1
    for lvl in range(levels):
        prods = [_dot(inv, p) for inv, p in zip(invs, ps)]
        if lvl < levels - 1:
            ps = [_dot(p, p) for p in ps]
        invs = [inv + pr for inv, pr in zip(invs, prods)]
    sols = [_dot(inv, it["rhs"]) for inv, it in zip(invs, items)]
    sks = [_dot(sol[:, hd:], it["s"]) for sol, it in zip(sols, items)]
    us = [sol[:, :hd] - sk for sol, sk in zip(sols, sks)]
    intra = [_dot(a * it["gam_i"], u) for a, it, u in zip(qk, items, us)]
    upd = [_dot_tn(it["ke"], u) for it, u in zip(items, us)]
    for it, oi, os, up in zip(items, intra, qs, upd):
        it["o_ref"][:, it["sl"]] = oi + os
        st_ref[it["d"], it["h"]] = it["dec"] * it["s"] + up


def mixer_gdn(z, n_batch, n_tok, n_ctx, col0, conv_w, a_log, dt_bias, n_heads, hd):
    t, zw = z.shape
    width = n_heads * hd
    tiles_per_row, ctx_tiles = n_tok // ROW_TILE, n_ctx // ROW_TILE
    cb0 = col0 // (3 * width)
    assert cb0 * 3 * width == col0
    halo = 8
    hpt = ROW_TILE // halo
    n_halo = t // halo
    qkv = pl.pallas_call(
        functools.partial(_gdn_conv_kernel, tiles_per_row=tiles_per_row, ctx_tiles=ctx_tiles, n_heads=n_heads, hd=hd),
        grid=(t // ROW_TILE,),
        in_specs=[pl.BlockSpec((halo, 3 * width), lambda i: (jnp.maximum(i * hpt - 1, 0), cb0)),
                  pl.BlockSpec((ROW_TILE, 3 * width), lambda i: (i, cb0)),
                  pl.BlockSpec((halo, 3 * width), lambda i: (jnp.minimum((i + 1) * hpt, n_halo - 1), cb0)),
                  pl.BlockSpec((3, 3 * width), lambda i: (0, 0))],
        out_specs=pl.BlockSpec((ROW_TILE, 3 * width), lambda i: (i, 0)),
        out_shape=jax.ShapeDtypeStruct((t, 3 * width), BF16),
        compiler_params=_cparams("parallel"),
        name="gdn_conv",
    )(z, z, z, conv_w.astype(F32))

    sc0 = col0 + 4 * width
    small = z[:, sc0:sc0 + 4 * n_heads].astype(F32)
    a_log, dt_bias = a_log.astype(F32), dt_bias.astype(F32)
    be = jax.nn.sigmoid(small[:, :2 * n_heads])
    la_f = -jnp.exp(a_log[0]) * jax.nn.softplus(small[:, 2 * n_heads:3 * n_heads] + dt_bias[0])
    la_b = -jnp.exp(a_log[1]) * jax.nn.softplus(small[:, 3 * n_heads:] + dt_bias[1])
    cols = jnp.concatenate([be, la_f, la_b], axis=1)
    nc, cc = n_tok // CHUNK, n_ctx // CHUNK
    cols3 = cols.reshape(n_batch, n_tok, 4 * n_heads)
    rows4 = jnp.transpose(cols.reshape(n_batch, nc, CHUNK, 4 * n_heads), (0, 1, 3, 2))
    qkv3 = qkv.reshape(n_batch, n_tok, 3 * width)
    bwd = functools.partial(_bwd_chunk, ctx_chunks=cc, n_chunks=nc)
    ident = lambda j: j
    def specs(order):
        return [pl.BlockSpec((None, CHUNK, width), lambda b, j: (b, order(j), 0)),
                pl.BlockSpec((None, CHUNK, width), lambda b, j: (b, order(j), 1)),
                pl.BlockSpec((None, CHUNK, width), lambda b, j: (b, order(j), 2)),
                pl.BlockSpec((None, CHUNK, 4 * n_heads), lambda b, j: (b, order(j), 0)),
                pl.BlockSpec((None, None, 4 * n_heads, CHUNK), lambda b, j: (b, order(j), 0, 0))]
    o_f, o_b = pl.pallas_call(
        functools.partial(_gdn_kernel, n_heads=n_heads, hd=hd),
        grid=(n_batch, nc),
        in_specs=specs(ident) + specs(bwd),
        out_specs=[pl.BlockSpec((None, CHUNK, width), lambda b, j: (b, j, 0)),
                   pl.BlockSpec((None, CHUNK, width), lambda b, j: (b, bwd(j), 0))],
        out_shape=[jax.ShapeDtypeStruct((n_batch, n_tok, width), F32)] * 2,
        scratch_shapes=[pltpu.VMEM((2, n_heads, hd, hd), F32)],
        compiler_params=_cparams("parallel", "arbitrary"),
        name="gdn_scan",
    )(qkv3, qkv3, qkv3, cols3, rows4, qkv3, qkv3, qkv3, cols3, rows4)
    return o_f.reshape(-1, width), o_b.reshape(-1, width)


def _first_max(x, idx, sentinel):
    m = jnp.max(x, axis=0, keepdims=True)
    return m, jnp.min(jnp.where(x == m, idx, sentinel), axis=0, keepdims=True)


def _router_kernel(h_ref, wt_ref, bias_ref, e_ref, g_ref, rank_ref, cnt_ref, carry_ref):
    @pl.when(pl.program_id(0) == 0)
    def _():
        carry_ref[...] = jnp.zeros_like(carry_ref)

    cols = h_ref.shape[0]
    gsz = N_EXPERTS // N_GROUPS
    neg = -jnp.inf
    logits = lax.dot_general(wt_ref[...], h_ref[...], (((1,), (1,)), ((), ())), preferred_element_type=F32)
    scores = jax.nn.sigmoid(logits)
    sel = scores + bias_ref[...]
    i_g = lax.broadcasted_iota(jnp.int32, (gsz, cols), 0)
    blocks, g_scores = [], []
    for g in range(N_GROUPS):
        blk = sel[g * gsz:(g + 1) * gsz, :]
        m1, first = _first_max(blk, i_g, gsz)
        m2 = jnp.max(jnp.where(i_g == first, neg, blk), axis=0, keepdims=True)
        blocks.append(blk)
        g_scores.append(m1 + m2)
    keep = [jnp.zeros((1, cols), jnp.bool_) for _ in range(N_GROUPS)]
    for _ in range(TOPK_GROUPS):
        best = functools.reduce(jnp.maximum, g_scores)
        found = jnp.zeros((1, cols), jnp.bool_)
        for g in range(N_GROUPS):
            pick = jnp.logical_and(g_scores[g] == best, jnp.logical_not(found))
            found = jnp.logical_or(found, pick)
            keep[g] = jnp.logical_or(keep[g], pick)
            g_scores[g] = jnp.where(pick, neg, g_scores[g])
    cur = jnp.concatenate([jnp.where(keep[g], blocks[g], neg) for g in range(N_GROUPS)], axis=0)
    i_e = lax.broadcasted_iota(jnp.int32, (N_EXPERTS, cols), 0)
    picks, gates = [], []
    for _ in range(TOP_K):
        _, idx = _first_max(cur, i_e, N_EXPERTS)
        pick = i_e == idx
        picks.append((idx, pick))
        gates.append(jnp.sum(jnp.where(pick, scores, 0.0), axis=0, keepdims=True))
        cur = jnp.where(pick, neg, cur)
    total = functools.reduce(jnp.add, gates)
    chosen = functools.reduce(jnp.logical_or, [p for _, p in picks]).astype(F32)
    r = lax.broadcasted_iota(jnp.int32, (cols, cols), 0)
    c = lax.broadcasted_iota(jnp.int32, (cols, cols), 1)
    before = jnp.dot(chosen.astype(BF16), (r < c).astype(BF16), preferred_element_type=F32) + carry_ref[...]
    for k, (idx, pick) in enumerate(picks):
        e_ref[k:k + 1, :] = idx
        g_ref[k:k + 1, :] = gates[k] / total * ROUTED_SCALE
        rank_ref[k:k + 1, :] = jnp.sum(jnp.where(pick, before, 0.0), axis=0, keepdims=True).astype(jnp.int32)
    carry_ref[...] += jnp.sum(chosen, axis=1, keepdims=True)
    cnt_ref[...] = carry_ref[...]


def _slot_fetch(slots_hbm, idx_smem, sem_idx, tile, buf):
    return pltpu.make_async_copy(slots_hbm.at[tile], idx_smem.at[buf], sem_idx.at[buf])


def _dispatch_kernel(padded_ref, pend_ref, nu_ref, slots_hbm, hp_ref, zeros_hbm, x_hbm, idx_smem, sem_idx, sem_rows):
    i = pl.program_id(0)
    rows = hp_ref.shape[0]

    @pl.when(i == 0)
    def _():
        _slot_fetch(slots_hbm, idx_smem, sem_idx, 0, 0).start()

        def zero_tail(e):
            return pltpu.make_async_copy(zeros_hbm, x_hbm.at[pl.ds(pend_ref[e] - MOE_BLOCK, MOE_BLOCK)], sem_rows)

        def zstart(e, carry):
            @pl.when(padded_ref[e] > 0)
            def _():
                zero_tail(e).start()
            return carry

        def zwait(e, carry):
            @pl.when(padded_ref[e] > 0)
            def _():
                zero_tail(e).wait()
            return carry

        lax.fori_loop(0, N_EXPERTS, zstart, 0)
        lax.fori_loop(0, N_EXPERTS, zwait, 0)

        def unused_block(j):
            return pltpu.make_async_copy(zeros_hbm, x_hbm.at[pl.ds(j * MOE_BLOCK, MOE_BLOCK)], sem_rows)

        n_blocks = x_hbm.shape[0] // MOE_BLOCK
        lax.fori_loop(nu_ref[0], n_blocks, lambda j, carry: (unused_block(j).start(), carry)[1], 0)
        lax.fori_loop(nu_ref[0], n_blocks, lambda j, carry: (unused_block(j).wait(), carry)[1], 0)

    buf = i % 2
    _slot_fetch(slots_hbm, idx_smem, sem_idx, i, buf).wait()

    @pl.when(i + 1 < pl.num_programs(0))
    def _():
        _slot_fetch(slots_hbm, idx_smem, sem_idx, i + 1, 1 - buf).start()

    def body(r, carry):
        for k in range(TOP_K):
            pltpu.make_async_copy(hp_ref.at[pl.ds(r, 1)], x_hbm.at[pl.ds(idx_smem[buf, k, r], 1)],
                                  sem_rows).start(priority=k % 2)
        return carry

    lax.fori_loop(0, rows, body, 0, unroll=8)
    span = x_hbm.at[pl.ds(0, rows * TOP_K)]
    pltpu.make_async_copy(span, span, sem_rows).wait()


def _expert_kernel(be_ref, nu_ref, x_ref, wg_ref, wu_ref, wd_ref, o_ref, wg_s, wu_s, wd_s):
    i = pl.program_id(0)
    live = i < nu_ref[0]
    new_expert = jnp.logical_or(i == 0, be_ref[i] != be_ref[jnp.maximum(i - 1, 0)])

    @pl.when(jnp.logical_and(live, new_expert))
    def _():
        wg_s[...] = wg_ref[...].astype(BF16)
        wu_s[...] = wu_ref[...].astype(BF16)
        wd_s[...] = wd_ref[...].astype(BF16)

    @pl.when(live)
    def _():
        lo, hi = _unpack_halves(_load_rows(x_ref, 0, x_ref.shape[0]))
        lo, hi = lo.astype(BF16), hi.astype(BF16)
        half = lo.shape[1]
        g = (jnp.dot(lo, wg_s[:half, :], preferred_element_type=F32)
             + jnp.dot(hi, wg_s[half:, :], preferred_element_type=F32))
        u = (jnp.dot(lo, wu_s[:half, :], preferred_element_type=F32)
             + jnp.dot(hi, wu_s[half:, :], preferred_element_type=F32))
        a = (_silu(g) * u).astype(BF16)
        _store_rows(o_ref, _pack_halves(jnp.dot(a, wd_s[...], preferred_element_type=F32)))

    @pl.when(jnp.logical_not(live))
    def _():
        o_ref[...] = jnp.zeros_like(o_ref)


def _combine_kernel(slots_hbm, y_hbm, h_ref, gate_ref, xs_ref, mg_ref, wg_ref, wu_ref, wd_ref, o_ref,
                    idx_smem, rows_buf, sem_idx, sem_rows):
    i = pl.program_id(0)
    rows = h_ref.shape[0]

    @pl.when(i == 0)
    def _():
        _slot_fetch(slots_hbm, idx_smem, sem_idx, 0, 0).start()

    buf = i % 2
    _slot_fetch(slots_hbm, idx_smem, sem_idx, i, buf).wait()

    @pl.when(i + 1 < pl.num_programs(0))
    def _():
        _slot_fetch(slots_hbm, idx_smem, sem_idx, i + 1, 1 - buf).start()

    def body(r, carry):
        for k in range(TOP_K):
            pltpu.make_async_copy(y_hbm.at[pl.ds(idx_smem[buf, k, r], 1)], rows_buf.at[pl.ds(k * rows + r, 1)],
                                  sem_rows).start(priority=k % 2)
        return carry

    lax.fori_loop(0, rows, body, 0, unroll=8)

    h = h_ref[...]
    g = jnp.dot(h, wg_ref[...], preferred_element_type=F32)
    u = jnp.dot(h, wu_ref[...], preferred_element_type=F32)
    a = (_silu(g) * u).astype(BF16)
    y = jnp.dot(a, wd_ref[...], preferred_element_type=F32)

    pltpu.make_async_copy(y_hbm.at[pl.ds(0, rows * TOP_K)], rows_buf, sem_rows).wait()
    half = y.shape[1] // 2
    acc_lo, acc_hi = y[:, :half], y[:, half:]
    gate = gate_ref[...]
    for k in range(TOP_K):
        lo, hi = _unpack_halves(_load_rows(rows_buf, k * rows, rows))
        acc_lo = acc_lo + gate[:, k:k + 1] * lo
        acc_hi = acc_hi + gate[:, k:k + 1] * hi
    mg = mg_ref[...]
    o_ref[:, :half] = xs_ref[:, :half] + mg[:, :half] * acc_lo
    o_ref[:, half:] = xs_ref[:, half:] + mg[:, half:] * acc_hi


def route(h2, w_router, router_bias):
    t, d = h2.shape
    n_tiles = t // ROW_TILE
    n_exp = w_router.shape[1]
    return pl.pallas_call(
        _router_kernel,
        grid=(n_tiles,),
        in_specs=[pl.BlockSpec((ROW_TILE, d), lambda i: (i, 0)),
                  pl.BlockSpec((n_exp, d), lambda i: (0, 0)),
                  pl.BlockSpec((n_exp, 1), lambda i: (0, 0))],
        out_specs=[pl.BlockSpec((TOP_K, ROW_TILE), lambda i: (0, i)),
                   pl.BlockSpec((TOP_K, ROW_TILE), lambda i: (0, i)),
                   pl.BlockSpec((TOP_K, ROW_TILE), lambda i: (0, i)),
                   pl.BlockSpec((n_exp, 1), lambda i: (0, 0))],
        out_shape=[jax.ShapeDtypeStruct((TOP_K, t), jnp.int32), jax.ShapeDtypeStruct((TOP_K, t), F32),
                   jax.ShapeDtypeStruct((TOP_K, t), jnp.int32), jax.ShapeDtypeStruct((n_exp, 1), F32)],
        scratch_shapes=[pltpu.VMEM((n_exp, 1), F32)],
        compiler_params=_cparams("arbitrary"),
        name="router_topk",
    )(h2, jnp.transpose(w_router).astype(BF16), router_bias.astype(F32).reshape(n_exp, 1))


def moe_block(h2, h2_packed, xs, mod, k_gate, tiles_per_row, n_batch,
              w_router, router_bias, w_gate, w_up, w_down, ws_gate, ws_up, ws_down):
    t, d = h2.shape
    row_shape = h2_packed.shape[1:]
    de = ws_gate.shape[1]
    n_tiles = t // ROW_TILE
    eidx, gate_t, rank, cnt = route(h2, w_router, router_bias)
    n_blocks = -(-t * TOP_K // MOE_BLOCK) + N_EXPERTS
    counts = cnt[:, 0].astype(jnp.int32)
    padded = (counts + MOE_BLOCK - 1) // MOE_BLOCK * MOE_BLOCK
    pad_end = jnp.cumsum(padded).astype(jnp.int32)
    pad_start = pad_end - padded
    e_ids = jnp.arange(N_EXPERTS, dtype=jnp.int32)
    slot = rank + jnp.sum(jnp.where(eidx[:, :, None] == e_ids, pad_start, 0), axis=-1)
    slots = jnp.transpose(slot.reshape(TOP_K, n_tiles, ROW_TILE), (1, 0, 2))
    gate = jnp.transpose(gate_t)
    block_start = jnp.arange(n_blocks, dtype=jnp.int32) * MOE_BLOCK
    block_e = jnp.minimum(jnp.sum(pad_end[None, :] <= block_start[:, None], axis=1), N_EXPERTS - 1).astype(jnp.int32)
    n_used = (pad_end[-1] // MOE_BLOCK).astype(jnp.int32).reshape(1)

    idx_scratch = [pltpu.SMEM((2, TOP_K, ROW_TILE), jnp.int32)]
    x_sorted = pl.pallas_call(
        _dispatch_kernel,
        grid_spec=pltpu.PrefetchScalarGridSpec(
            num_scalar_prefetch=3,
            grid=(n_tiles,),
            in_specs=[pl.BlockSpec(memory_space=pl.ANY),
                      pl.BlockSpec((ROW_TILE,) + row_shape, lambda i, pd, pe, nu: (i, 0, 0)),
                      pl.BlockSpec(memory_space=pl.ANY)],
            out_specs=pl.BlockSpec(memory_space=pl.ANY),
            scratch_shapes=idx_scratch + [pltpu.SemaphoreType.DMA((2,)), pltpu.SemaphoreType.DMA(())],
        ),
        out_shape=jax.ShapeDtypeStruct((n_blocks * MOE_BLOCK,) + row_shape, jnp.uint32),
        compiler_params=_cparams("arbitrary"),
        name="moe_dispatch",
    )(padded, pad_end, n_used, slots, h2_packed, jnp.zeros((MOE_BLOCK,) + row_shape, jnp.uint32))

    y_sorted = pl.pallas_call(
        _expert_kernel,
        grid_spec=pltpu.PrefetchScalarGridSpec(
            num_scalar_prefetch=2,
            grid=(n_blocks,),
            in_specs=[pl.BlockSpec((MOE_BLOCK,) + row_shape, lambda i, be, nu: (jnp.minimum(i, nu[0] - 1), 0, 0)),
                      pl.BlockSpec((None, d, de), lambda i, be, nu: (be[i], 0, 0)),
                      pl.BlockSpec((None, d, de), lambda i, be, nu: (be[i], 0, 0)),
                      pl.BlockSpec((None, de, d), lambda i, be, nu: (be[i], 0, 0))],
            out_specs=pl.BlockSpec((MOE_BLOCK,) + row_shape, lambda i, be, nu: (i, 0, 0)),
            scratch_shapes=[pltpu.VMEM((d, de), BF16), pltpu.VMEM((d, de), BF16), pltpu.VMEM((de, d), BF16)],
        ),
        out_shape=jax.ShapeDtypeStruct((n_blocks * MOE_BLOCK,) + row_shape, jnp.uint32),
        compiler_params=_cparams("arbitrary"),
        name="routed_experts",
    )(block_e, n_used, x_sorted, w_gate, w_up, w_down)

    seg = functools.partial(_seg_of_tile, tiles_per_row=tiles_per_row, n_batch=n_batch)
    return pl.pallas_call(
        _combine_kernel,
        grid=(n_tiles,),
        in_specs=[pl.BlockSpec(memory_space=pl.ANY),
                  pl.BlockSpec(memory_space=pl.ANY),
                  pl.BlockSpec((ROW_TILE, d), lambda i: (i, 0)),
                  pl.BlockSpec((ROW_TILE, TOP_K), lambda i: (i, 0)),
                  pl.BlockSpec((ROW_TILE, d), lambda i: (i, 0)),
                  pl.BlockSpec((None, None, 1, d), lambda i: (seg(i), k_gate, 0, 0)),
                  pl.BlockSpec((d, de), lambda i: (0, 0)),
                  pl.BlockSpec((d, de), lambda i: (0, 0)),
                  pl.BlockSpec((de, d), lambda i: (0, 0))],
        out_specs=pl.BlockSpec((ROW_TILE, d), lambda i: (i, 0)),
        out_shape=jax.ShapeDtypeStruct((t, d), F32),
        scratch_shapes=idx_scratch + [pltpu.VMEM((TOP_K * ROW_TILE,) + row_shape, jnp.uint32),
                                      pltpu.SemaphoreType.DMA((2,)), pltpu.SemaphoreType.DMA(())],
        compiler_params=_cparams("arbitrary"),
        name="moe_combine",
    )(slots, y_sorted, h2, gate, xs, mod, ws_gate.astype(BF16), ws_up.astype(BF16), ws_down.astype(BF16))


def kernel(x, c, ctx, c_ctx, w_ada, b_ada, norm1_g, norm2_g, w_in_ab, s5_lambda_re, s5_lambda_im, s5_log_dt, s5_b_re, s5_b_im, s5_c_re, s5_c_im, s5_d, s5_w_glu, hgrn_lb, w_in_cd, ret_decay_logit, gdn_conv_w, gdn_a_log, gdn_dt_bias, gdn_norm_g, w_out, w_router, router_bias, w_exp_gate, w_exp_up, w_exp_down, w_sh_gate, w_sh_up, w_sh_down, final_norm_g):
    n_batch, n_lat, d = x.shape
    n_ctx = ctx.shape[1]
    n_tok = n_ctx + n_lat
    depth = w_ada.shape[0]
    assert n_ctx % ROW_TILE == 0 and n_lat % ROW_TILE == 0
    tiles_per_row, ctx_tiles = n_tok // ROW_TILE, n_ctx // ROW_TILE
    assert ctx_tiles == 1
    t = n_batch * n_tok

    xs = jnp.concatenate([ctx, x], axis=1).reshape(t, d)
    lb_all = jnp.cumsum(jax.nn.softmax(hgrn_lb.astype(F32), axis=0), axis=0)
    cond = jnp.concatenate([c, c_ctx[None]], axis=0)
    cond = jnp.pad(jax.nn.silu(cond), ((0, 8 - (n_batch + 1)), (0, 0)))
    ones_g = jnp.ones((D_HEAD,), F32)

    for i in range(depth):
        j = i // 2
        mod = matmul(cond, w_ada[i], F32, 8, 1024, "ada_mod")[:n_batch + 1] + b_ada[i]
        mod = mod.reshape(n_batch + 1, 6, 1, d)
        h = norm_mod(xs, norm1_g[i], mod, 0, 1, tiles_per_row, n_batch)
        if i % 2 == 0:
            z = matmul(h, w_in_ab[j].astype(BF16), F32, 512, 1024, "in_proj")
            a_width = s5_d.shape[1]
            b_heads = (z.shape[1] - a_width) // 5 // B_HEAD
            y_a = mixer_s5(z, n_batch, n_tok, n_ctx, s5_lambda_re[j], s5_lambda_im[j], s5_log_dt[j],
                           s5_b_re[j], s5_b_im[j], s5_c_re[j], s5_c_im[j], s5_d[j], s5_w_glu[j])
            o_f, o_b = mixer_gla(z, n_batch, n_tok, n_ctx, a_width, lb_all[j], b_heads, B_HEAD)
            gcb = (a_width + 4 * b_heads * B_HEAD) // (b_heads * B_HEAD)
            y_b = post_norm_gate(o_f, o_b, z, gcb, B_HEAD, jnp.ones((B_HEAD,), F32))
            m1, m2 = y_a, y_b
        else:
            cd_cols = w_in_cd.shape[2]
            cd_pad = -(-cd_cols // 1024) * 1024
            w_cd = jnp.pad(w_in_cd[j].astype(BF16), ((0, 0), (0, cd_pad - cd_cols)))
            z = matmul(h, w_cd, F32, 512, 1024, "in_proj")
            c_heads = ret_decay_logit.shape[2]
            d_heads = gdn_a_log.shape[2]
            o_f, o_b = mixer_retention(z, n_batch, n_tok, n_ctx, ret_decay_logit[j], c_heads)
            y_c = post_norm_gate(o_f, o_b, z, (2 * c_heads * C_QK + c_heads * C_V) // (c_heads * C_V), C_V,
                                 jnp.ones((C_V,), F32))
            col0 = 2 * c_heads * C_QK + 2 * c_heads * C_V
            o_f, o_b = mixer_gdn(z, n_batch, n_tok, n_ctx, col0, gdn_conv_w[j], gdn_a_log[j], gdn_dt_bias[j],
                                 d_heads, D_HEAD)
            y_d = post_norm_gate(o_f, o_b, z, (col0 + 3 * d_heads * D_HEAD) // (d_heads * D_HEAD), D_HEAD,
                                 gdn_norm_g[j])
            m1, m2 = y_c, y_d
        xs = out_proj(m1, m2, w_out[i].astype(BF16), xs, mod, 2, tiles_per_row, n_batch)
        h2, h2_packed = norm_mod(xs, norm2_g[i], mod, 3, 4, tiles_per_row, n_batch, packed=True)
        xs = moe_block(h2, h2_packed, xs, mod, 5, tiles_per_row, n_batch, w_router[i], router_bias[i],
                       w_exp_gate[i], w_exp_up[i], w_exp_down[i], w_sh_gate[i], w_sh_up[i], w_sh_down[i])
    out = final_norm(xs, final_norm_g, n_batch, tiles_per_row, ctx_tiles)
    return out.reshape(n_batch, n_lat, d)
```

```python
import functools

import numpy as np
import jax
import jax.numpy as jnp
from jax import lax
from jax.experimental import pallas as pl
from jax.experimental.pallas import tpu as pltpu

F32 = jnp.float32
BF16 = jnp.bfloat16

EPS = 1e-6
GRID_W = 64
CHUNK = 64
S5_L = 32
S5_H = 16
S5_P = 64
B_HEAD = 128
C_QK = 128
C_V = 256
D_HEAD = 128
ROPE_BASE = 10000.0
N_EXPERTS = 64
N_GROUPS = 8
TOPK_GROUPS = 4
TOP_K = 8
ROUTED_SCALE = 2.5
MOE_BLOCK = 256
ROW_TILE = 256
VMEM_LIMIT_V7X = 56 * 1024 * 1024
EXP_CLAMP = 80.0


def _cparams(*sem):
    return pltpu.CompilerParams(dimension_semantics=sem, vmem_limit_bytes=VMEM_LIMIT_V7X)


def _silu(x):
    return x * jax.nn.sigmoid(x)


def _dot(a, b):
    return jnp.dot(a.astype(BF16), b.astype(BF16), preferred_element_type=F32)


def _dot_nt(a, b):
    return lax.dot_general(a.astype(BF16), b.astype(BF16), (((1,), (1,)), ((), ())), preferred_element_type=F32)


def _dot_tn(a, b):
    return lax.dot_general(a.astype(BF16), b.astype(BF16), (((0,), (0,)), ((), ())), preferred_element_type=F32)


def _split3(x):
    x1 = x.astype(BF16)
    r = x - x1.astype(F32)
    x2 = r.astype(BF16)
    x3 = (r - x2.astype(F32)).astype(BF16)
    return x1, x2, x3


def _tri_dot(tri, x):
    return sum(jnp.dot(tri, p, preferred_element_type=F32) for p in _split3(x))


def _dot_tri(x, tri):
    return sum(jnp.dot(p, tri, preferred_element_type=F32) for p in _split3(x))


def _tri_mask(n, lower):
    r = lax.broadcasted_iota(jnp.int32, (n, n), 0)
    c = lax.broadcasted_iota(jnp.int32, (n, n), 1)
    return (r >= c) if lower else (r <= c)


def _seg_of_tile(i, tiles_per_row, n_batch):
    return jnp.where(i % tiles_per_row == 0, n_batch, i // tiles_per_row)


def _pack_halves(y):
    half = y.shape[1] // 2
    lo = lax.bitcast_convert_type(y[:, :half].astype(BF16).astype(F32), jnp.uint32) >> 16
    hi = lax.bitcast_convert_type(y[:, half:].astype(BF16).astype(F32), jnp.uint32) & jnp.uint32(0xFFFF0000)
    return hi | lo


def _unpack_halves(w):
    lo = lax.bitcast_convert_type(w << 16, F32)
    hi = lax.bitcast_convert_type(w & jnp.uint32(0xFFFF0000), F32)
    return lo, hi


LANES = 128


def _store_rows(ref, packed):
    pieces = jnp.stack([packed[:, s * LANES:(s + 1) * LANES] for s in range(ref.shape[1])], axis=0)
    ref[...] = pltpu.einshape("stl->tsl", pieces)


def _load_rows(ref, row0, rows):
    x = pltpu.einshape("tsl->stl", ref[row0:row0 + rows])
    return jnp.concatenate([x[s] for s in range(ref.shape[1])], axis=1)


def _norm_mod_kernel(x_ref, g_ref, shift_ref, scale_ref, o_ref, *packed_ref):
    x = x_ref[...]
    y = x * lax.rsqrt(jnp.mean(x * x, axis=-1, keepdims=True) + EPS) * g_ref[...]
    y = y * (1.0 + scale_ref[...]) + shift_ref[...]
    o_ref[...] = y.astype(o_ref.dtype)
    if packed_ref:
        _store_rows(packed_ref[0], _pack_halves(y))


def norm_mod(xs, g, mod, k_shift, k_scale, tiles_per_row, n_batch, packed=False):
    t, d = xs.shape
    seg = functools.partial(_seg_of_tile, tiles_per_row=tiles_per_row, n_batch=n_batch)
    out_specs = [pl.BlockSpec((ROW_TILE, d), lambda i: (i, 0))]
    out_shape = [jax.ShapeDtypeStruct((t, d), BF16)]
    if packed:
        out_specs.append(pl.BlockSpec((ROW_TILE, d // 2 // LANES, LANES), lambda i: (i, 0, 0)))
        out_shape.append(jax.ShapeDtypeStruct((t, d // 2 // LANES, LANES), jnp.uint32))
    res = pl.pallas_call(
        _norm_mod_kernel,
        grid=(t // ROW_TILE,),
        in_specs=[
            pl.BlockSpec((ROW_TILE, d), lambda i: (i, 0)),
            pl.BlockSpec((1, d), lambda i: (0, 0)),
            pl.BlockSpec((None, None, 1, d), lambda i: (seg(i), k_shift, 0, 0)),
            pl.BlockSpec((None, None, 1, d), lambda i: (seg(i), k_scale, 0, 0)),
        ],
        out_specs=out_specs,
        out_shape=out_shape,
        compiler_params=_cparams("parallel"),
        name="norm_mod",
    )(xs, g.reshape(1, d), mod, mod)
    return res if packed else res[0]


def _final_norm_kernel(x_ref, g_ref, o_ref):
    x = x_ref[...]
    o_ref[...] = x * lax.rsqrt(jnp.mean(x * x, axis=-1, keepdims=True) + EPS) * g_ref[...]


def final_norm(xs, g, n_batch, tiles_per_row, ctx_tiles):
    t, d = xs.shape
    lat_tiles = tiles_per_row - ctx_tiles
    return pl.pallas_call(
        _final_norm_kernel,
        grid=(n_batch, lat_tiles),
        in_specs=[
            pl.BlockSpec((ROW_TILE, d), lambda b, i: (b * tiles_per_row + ctx_tiles + i, 0)),
            pl.BlockSpec((1, d), lambda b, i: (0, 0)),
        ],
        out_specs=pl.BlockSpec((ROW_TILE, d), lambda b, i: (b * lat_tiles + i, 0)),
        out_shape=jax.ShapeDtypeStruct((n_batch * lat_tiles * ROW_TILE, d), F32),
        compiler_params=_cparams("parallel", "parallel"),
        name="final_norm",
    )(xs, g.reshape(1, d))


def _mm_kernel(a_ref, w_ref, o_ref):
    o_ref[...] = jnp.dot(a_ref[...].astype(BF16), w_ref[...].astype(BF16),
                         preferred_element_type=F32).astype(o_ref.dtype)


def matmul(a, w, out_dtype, tm, tn, name):
    m, k = a.shape
    n = w.shape[1]
    return pl.pallas_call(
        _mm_kernel,
        grid=(n // tn, m // tm),
        in_specs=[pl.BlockSpec((tm, k), lambda j, i: (i, 0)),
                  pl.BlockSpec((k, tn), lambda j, i: (0, j))],
        out_specs=pl.BlockSpec((tm, tn), lambda j, i: (i, j)),
        out_shape=jax.ShapeDtypeStruct((m, n), out_dtype),
        compiler_params=_cparams("parallel", "parallel"),
        name=name,
    )(a, w)


def _out_proj_kernel(a1_ref, a2_ref, w1_ref, w2_ref, res_ref, gate_ref, o_ref):
    y = jnp.dot(a1_ref[...], w1_ref[...], preferred_element_type=F32)
    y += jnp.dot(a2_ref[...], w2_ref[...], preferred_element_type=F32)
    o_ref[...] = res_ref[...] + gate_ref[...] * y


def out_proj(a1, a2, w, xs, mod, k_gate, tiles_per_row, n_batch, tn=1024):
    t, d = xs.shape
    k1, k2 = a1.shape[1], a2.shape[1]
    assert k1 == k2
    seg = functools.partial(_seg_of_tile, tiles_per_row=tiles_per_row, n_batch=n_batch)
    return pl.pallas_call(
        _out_proj_kernel,
        grid=(d // tn, t // ROW_TILE),
        in_specs=[
            pl.BlockSpec((ROW_TILE, k1), lambda j, i: (i, 0)),
            pl.BlockSpec((ROW_TILE, k2), lambda j, i: (i, 0)),
            pl.BlockSpec((k1, tn), lambda j, i: (0, j)),
            pl.BlockSpec((k2, tn), lambda j, i: (1, j)),
            pl.BlockSpec((ROW_TILE, tn), lambda j, i: (i, j)),
            pl.BlockSpec((None, None, 1, tn), lambda j, i: (seg(i), k_gate, 0, j)),
        ],
        out_specs=pl.BlockSpec((ROW_TILE, tn), lambda j, i: (i, j)),
        out_shape=jax.ShapeDtypeStruct((t, d), F32),
        compiler_params=_cparams("parallel", "parallel"),
        name="out_proj",
    )(a1, a2, w, w, xs, mod)


def _post_kernel(of_ref, ob_ref, gate_ref, ng_ref, o_ref, *, head_dim):
    o = of_ref[...] + ob_ref[...]
    g = gate_ref[...].astype(F32)
    width = o.shape[1]
    for h in range(width // head_dim):
        sl = slice(h * head_dim, (h + 1) * head_dim)
        oh = o[:, sl]
        y = oh * lax.rsqrt(jnp.mean(oh * oh, axis=-1, keepdims=True) + EPS) * ng_ref[...]
        o_ref[:, sl] = (y * _silu(g[:, sl])).astype(o_ref.dtype)


def post_norm_gate(o_f, o_b, z, gate_col_block, head_dim, norm_g):
    t, width = o_f.shape
    return pl.pallas_call(
        functools.partial(_post_kernel, head_dim=head_dim),
        grid=(t // ROW_TILE,),
        in_specs=[
            pl.BlockSpec((ROW_TILE, width), lambda i: (i, 0)),
            pl.BlockSpec((ROW_TILE, width), lambda i: (i, 0)),
            pl.BlockSpec((ROW_TILE, width), lambda i: (i, gate_col_block)),
            pl.BlockSpec((1, head_dim), lambda i: (0, 0)),
        ],
        out_specs=pl.BlockSpec((ROW_TILE, width), lambda i: (i, 0)),
        out_shape=jax.ShapeDtypeStruct((t, width), BF16),
        compiler_params=_cparams("parallel"),
        name="post_norm_gate",
    )(o_f, o_b, z, norm_g.reshape(1, head_dim).astype(F32))


def _bwd_chunk(j, ctx_chunks, n_chunks):
    return jnp.where(j < ctx_chunks, ctx_chunks - 1 - j, n_chunks + ctx_chunks - 1 - j)


def _s5_direction_tables(lam_re, lam_im, log_dt, b_re, b_im, c_re, c_im, reverse):
    hi = lax.Precision.HIGHEST
    ln = S5_L
    lam_re, lam_im, b_re, b_im, c_re, c_im = (p.astype(F32) for p in (lam_re, lam_im, b_re, b_im, c_re, c_im))
    dt = jnp.exp(log_dt.astype(F32))[:, None]
    mag = jnp.exp(lam_re * dt)
    ab_re, ab_im = mag * jnp.cos(lam_im * dt), mag * jnp.sin(lam_im * dt)
    den = lam_re * lam_re + lam_im * lam_im
    fr = ((ab_re - 1) * lam_re + ab_im * lam_im) / den
    fi = (ab_im * lam_re - (ab_re - 1) * lam_im) / den
    bb_re = fr[..., None] * b_re - fi[..., None] * b_im
    bb_im = fr[..., None] * b_im + fi[..., None] * b_re
    tau = jnp.arange(ln + 1, dtype=F32)[:, None, None]
    pw = jnp.exp(tau * (lam_re * dt))
    pr, pi = pw * jnp.cos(tau * (lam_im * dt)), pw * jnp.sin(tau * (lam_im * dt))
    abr = pr[..., None] * bb_re - pi[..., None] * bb_im
    abi = pr[..., None] * bb_im + pi[..., None] * bb_re
    kern = (jnp.einsum('ghp,tgpk->tghk', c_re, abr[:ln], precision=hi)
            - jnp.einsum('ghp,tgpk->tghk', c_im, abi[:ln], precision=hi))
    pos = jnp.arange(ln)
    lag = (pos[None, :] - pos[:, None]) if not reverse else (pos[:, None] - pos[None, :])
    toe = jnp.where((lag >= 0)[:, :, None, None, None], kern[jnp.clip(lag, 0, ln - 1)], 0.0)
    g = lam_re.shape[0]
    intra = jnp.transpose(toe, (2, 0, 4, 1, 3)).reshape(g, ln * S5_H, ln * S5_H)
    pw_in = (ln - 1 - pos) if not reverse else pos
    inj = jnp.concatenate([abr[pw_in], abi[pw_in]], axis=2)
    inject = jnp.transpose(inj, (1, 0, 3, 2)).reshape(g, ln * S5_H, 2 * S5_P)
    pw_out = (pos + 1) if not reverse else (ln - pos)
    w_re = c_re[None] * pr[pw_out][:, :, None, :] - c_im[None] * pi[pw_out][:, :, None, :]
    w_im = -(c_re[None] * pi[pw_out][:, :, None, :] + c_im[None] * pr[pw_out][:, :, None, :])
    readout = jnp.transpose(jnp.concatenate([w_re, w_im], axis=3), (1, 3, 0, 2)).reshape(g, 2 * S5_P, ln * S5_H)
    decay = jnp.stack([pr[ln], pi[ln]])
    return intra, inject, readout, decay


def _s5_in_kernel(u_ref, w_ref, yi_ref, s_ref):
    r = jnp.dot(u_ref[...], w_ref[...], preferred_element_type=F32)
    n_intra = yi_ref.shape[-1]
    yi_ref[...] = r[:, :n_intra]
    s_ref[...] = r[:, n_intra:]


def _s5_scan_kernel(a_ref, sf_ref, sb_ref, of_ref, ob_ref, st_ref):
    @pl.when(pl.program_id(1) == 0)
    def _():
        st_ref[...] = jnp.zeros_like(st_ref)

    steps = sf_ref.shape[0]
    for d, (s_ref, o_ref) in enumerate(((sf_ref, of_ref), (sb_ref, ob_ref))):
        ar, ai = a_ref[2 * d], a_ref[2 * d + 1]
        sr, si = st_ref[2 * d], st_ref[2 * d + 1]
        for q in range(steps):
            r = q if d == 0 else steps - 1 - q
            o_ref[r, 0] = sr
            o_ref[r, 1] = si
            sr, si = (ar * sr - ai * si + s_ref[r, 0], ar * si + ai * sr + s_ref[r, 1])
        st_ref[2 * d] = sr
        st_ref[2 * d + 1] = si


def _s5_out_kernel(yi_ref, st_ref, w_ref, u_ref, d_ref, o_ref):
    y = yi_ref[...] + jnp.dot(st_ref[...].astype(BF16), w_ref[...], preferred_element_type=F32)
    o_ref[...] = (y + d_ref[...] * u_ref[...].astype(F32)).astype(o_ref.dtype)


def _glu_kernel(y_ref, w_ref, o_ref):
    y = jax.nn.gelu(y_ref[...].astype(F32))
    o_ref[...] = (y * jax.nn.sigmoid(jnp.dot(y.astype(BF16), w_ref[...], preferred_element_type=F32))
                  ).astype(o_ref.dtype)


def mixer_s5(z, n_batch, n_tok, n_ctx, lam_re, lam_im, log_dt, b_re, b_im, c_re, c_im, d_skip, w_glu):
    t = z.shape[0]
    g = lam_re.shape[1]
    width = g * S5_H
    lh = S5_L * S5_H
    rows = t // S5_L
    tabs = [_s5_direction_tables(lam_re[d], lam_im[d], log_dt[d], b_re[d], b_im[d], c_re[d], c_im[d], d == 1)
            for d in range(2)]
    w_in = jnp.concatenate([tabs[0][0] + tabs[1][0], tabs[0][1], tabs[1][1]], axis=2).astype(BF16)
    w_st = jnp.concatenate([tabs[0][2], tabs[1][2]], axis=1).astype(BF16)
    decay = jnp.concatenate([tabs[0][3], tabs[1][3]], axis=0).reshape(4, g * S5_P // 128, 128)
    u = z[:, :width].reshape(rows, S5_L, g, S5_H)
    u_g = jnp.transpose(u, (2, 0, 1, 3)).reshape(g, rows, lh).astype(BF16)

    n_st = 4 * S5_P
    yi, s_in = pl.pallas_call(
        _s5_in_kernel,
        grid=(g,),
        in_specs=[pl.BlockSpec((None, rows, lh), lambda i: (i, 0, 0)),
                  pl.BlockSpec((None, lh, lh + n_st), lambda i: (i, 0, 0))],
        out_specs=[pl.BlockSpec((None, rows, lh), lambda i: (i, 0, 0)),
                   pl.BlockSpec((None, rows, n_st), lambda i: (i, 0, 0))],
        out_shape=[jax.ShapeDtypeStruct((g, rows, lh), F32), jax.ShapeDtypeStruct((g, rows, n_st), F32)],
        compiler_params=_cparams("parallel"),
        name="s5_in",
    )(u_g, w_in)

    gp = g * S5_P // 128
    s_scan = jnp.transpose(s_in.reshape(g, rows, 4, S5_P), (1, 2, 0, 3)).reshape(rows, 4, gp, 128)
    blocks_per_row = n_tok // S5_L
    cb = n_ctx // S5_L
    assert blocks_per_row % cb == 0
    nblk = blocks_per_row // cb
    bwd = lambda j: jnp.where(j == 0, 0, nblk - j)
    st_f, st_b = pl.pallas_call(
        _s5_scan_kernel,
        grid=(n_batch, nblk),
        in_specs=[pl.BlockSpec((4, gp, 128), lambda b, j: (0, 0, 0)),
                  pl.BlockSpec((cb, 2, gp, 128), lambda b, j: (b * nblk + j, 0, 0, 0)),
                  pl.BlockSpec((cb, 2, gp, 128), lambda b, j: (b * nblk + bwd(j), 1, 0, 0))],
        out_specs=[pl.BlockSpec((cb, 2, gp, 128), lambda b, j: (b * nblk + j, 0, 0, 0)),
                   pl.BlockSpec((cb, 2, gp, 128), lambda b, j: (b * nblk + bwd(j), 0, 0, 0))],
        out_shape=[jax.ShapeDtypeStruct((rows, 2, gp, 128), F32)] * 2,
        scratch_shapes=[pltpu.VMEM((4, gp, 128), F32)],
        compiler_params=_cparams("parallel", "arbitrary"),
        name="s5_scan",
    )(decay, s_scan, s_scan)
    st = jnp.concatenate([st_f, st_b], axis=1).reshape(rows, 4, g, S5_P)
    st_g = jnp.transpose(st, (2, 0, 1, 3)).reshape(g, rows, n_st)

    d_vec = jnp.tile(d_skip.astype(F32).reshape(g, 1, S5_H), (1, S5_L, 1)).reshape(g, 1, lh)
    y_g = pl.pallas_call(
        _s5_out_kernel,
        grid=(g,),
        in_specs=[pl.BlockSpec((None, rows, lh), lambda i: (i, 0, 0)),
                  pl.BlockSpec((None, rows, n_st), lambda i: (i, 0, 0)),
                  pl.BlockSpec((None, n_st, lh), lambda i: (i, 0, 0)),
                  pl.BlockSpec((None, rows, lh), lambda i: (i, 0, 0)),
                  pl.BlockSpec((None, 1, lh), lambda i: (i, 0, 0))],
        out_specs=pl.BlockSpec((None, rows, lh), lambda i: (i, 0, 0)),
        out_shape=jax.ShapeDtypeStruct((g, rows, lh), BF16),
        compiler_params=_cparams("parallel"),
        name="s5_out",
    )(yi, st_g, w_st, u_g, d_vec)
    y = jnp.transpose(y_g.reshape(g, rows, S5_L, S5_H), (1, 2, 0, 3)).reshape(t, width)

    return pl.pallas_call(
        _glu_kernel,
        grid=(t // ROW_TILE,),
        in_specs=[pl.BlockSpec((ROW_TILE, width), lambda i: (i, 0)),
                  pl.BlockSpec((width, width), lambda i: (0, 0))],
        out_specs=pl.BlockSpec((ROW_TILE, width), lambda i: (i, 0)),
        out_shape=jax.ShapeDtypeStruct((t, width), BF16),
        compiler_params=_cparams("parallel"),
        name="s5_glu",
    )(y, w_glu.astype(BF16))


def _gla_kernel(lb_ref, qf_ref, ff_ref, vf_ref, qb_ref, fb_ref, vb_ref, of_ref, ob_ref, st_ref, *, n_heads, hd):
    @pl.when(pl.program_id(1) == 0)
    def _():
        st_ref[...] = jnp.zeros_like(st_ref)

    c = qf_ref.shape[0]
    items = []
    for d, (q_ref, f_ref, v_ref, o_ref) in enumerate(((qf_ref, ff_ref, vf_ref, of_ref),
                                                       (qb_ref, fb_ref, vb_ref, ob_ref))):
        lower = d == 0
        mask = _tri_mask(c, lower)
        tri = mask.astype(BF16)
        lb = lb_ref[d]
        sig = jax.nn.sigmoid(f_ref[...])
        kk = (1.0 - lb) * (1.0 - sig)
        logf = jnp.log(lb + (1.0 - lb) * sig)
        b = _tri_dot(tri, logf)
        b_end = b[c - 1:c, :] if lower else b[0:1, :]
        qt = (_silu(q_ref[...]) * jnp.exp(b)).astype(BF16)
        kt = (kk * jnp.exp(jnp.minimum(-b, EXP_CLAMP))).astype(BF16)
        kend = (kk * jnp.exp(b_end - b)).astype(BF16)
        dec = jnp.exp(b_end)
        v = v_ref[...].astype(BF16)
        for h in range(n_heads):
            sl = slice(h * hd, (h + 1) * hd)
            items.append((d, h, sl, o_ref, mask, qt[:, sl], kt[:, sl], kend[:, sl], v[:, sl], dec[:, sl], st_ref[d, h]))
    att = [_dot_nt(it[5], it[6]) for it in items]
    inter = [_dot_nt(it[5], it[10]) for it in items]
    upd = [_dot_tn(it[8], it[7]) for it in items]
    intra = [_dot(jnp.where(it[4], a, 0.0), it[8]) for it, a in zip(items, att)]
    for it, oi, os, up in zip(items, intra, inter, upd):
        d, h, sl, o_ref = it[:4]
        o_ref[:, sl] = oi + os
        st_ref[d, h] = it[9] * it[10] + up


def mixer_gla(z, n_batch, n_tok, n_ctx, col0, lb, n_heads, hd):
    width = n_heads * hd
    z3 = z.reshape(n_batch, n_tok, z.shape[1])
    nc, cc = n_tok // CHUNK, n_ctx // CHUNK
    cb0 = col0 // width
    bwd = functools.partial(_bwd_chunk, ctx_chunks=cc, n_chunks=nc)
    blk = (None, CHUNK, width)
    fw = lambda k: pl.BlockSpec(blk, lambda b, j: (b, j, cb0 + k))
    bw = lambda k: pl.BlockSpec(blk, lambda b, j: (b, bwd(j), cb0 + k))
    o_f, o_b = pl.pallas_call(
        functools.partial(_gla_kernel, n_heads=n_heads, hd=hd),
        grid=(n_batch, nc),
        in_specs=[pl.BlockSpec((2, 1, width), lambda b, j: (0, 0, 0)),
                  fw(0), fw(1), fw(3), bw(0), bw(2), bw(3)],
        out_specs=[pl.BlockSpec(blk, lambda b, j: (b, j, 0)),
                   pl.BlockSpec(blk, lambda b, j: (b, bwd(j), 0))],
        out_shape=[jax.ShapeDtypeStruct((n_batch, n_tok, width), F32)] * 2,
        scratch_shapes=[pltpu.VMEM((2, n_heads, hd, hd), F32)],
        compiler_params=_cparams("parallel", "arbitrary"),
        name="gla_scan",
    )(lb.reshape(2, 1, width).astype(F32), z3, z3, z3, z3, z3, z3)
    return o_f.reshape(-1, width), o_b.reshape(-1, width)


def _rotary_tables(n_tok, n_ctx):
    n_lat = n_tok - n_ctx
    rows = n_lat // GRID_W
    row = jnp.repeat(jnp.arange(rows, dtype=F32), GRID_W)
    col = jnp.tile(jnp.arange(GRID_W, dtype=F32), rows)
    n_freq = C_QK // 4
    inv = ROPE_BASE ** (-jnp.arange(n_freq, dtype=F32) / n_freq)
    ang = jnp.concatenate([row[:, None] * inv, col[:, None] * inv], axis=-1)
    ang = jnp.concatenate([jnp.zeros((n_ctx, C_QK // 2), F32), ang], axis=0)
    cos, sin = jnp.cos(ang), jnp.sin(ang)
    return jnp.concatenate([cos, cos], axis=-1), jnp.concatenate([-sin, sin], axis=-1)


def _ret_kernel(cdec_ref, dmat_ref, qdec_ref, kdec_ref,
                qf_ref, kf_ref, vf_ref, cf_ref, sf_ref, qb_ref, kb_ref, vb_ref, cb_ref, sb_ref,
                of_ref, ob_ref, st_ref, *, n_heads):
    @pl.when(pl.program_id(1) == 0)
    def _():
        st_ref[...] = jnp.zeros_like(st_ref)

    half = C_QK // 2
    items = []
    for d, (q_ref, k_ref, v_ref, cos_ref, sin_ref, o_ref) in enumerate(
            ((qf_ref, kf_ref, vf_ref, cf_ref, sf_ref, of_ref), (qb_ref, kb_ref, vb_ref, cb_ref, sb_ref, ob_ref))):
        cos, sin = cos_ref[...], sin_ref[...]
        for h in range(n_heads):
            qs = slice(h * C_QK, (h + 1) * C_QK)
            vs = slice(h * C_V, (h + 1) * C_V)
            qh, kh = q_ref[:, qs].astype(F32), k_ref[:, qs].astype(F32)
            qh = (qh * cos + pltpu.roll(qh, half, axis=1) * sin) * (C_QK ** -0.5)
            kh = kh * cos + pltpu.roll(kh, half, axis=1) * sin
            items.append((d, h, vs, o_ref, qh.astype(BF16), kh.astype(BF16), (qh * qdec_ref[d, h]).astype(BF16),
                          (kh * kdec_ref[d, h]).astype(BF16), v_ref[:, vs].astype(BF16), st_ref[d, h]))
    att = [_dot_nt(it[4], it[5]) for it in items]
    inter = [_dot(it[6], it[9]) for it in items]
    upd = [_dot_tn(it[7], it[8]) for it in items]
    intra = [_dot(a * dmat_ref[it[0], it[1]], it[8]) for it, a in zip(items, att)]
    for it, oi, os, up in zip(items, intra, inter, upd):
        d, h, vs, o_ref = it[:4]
        o_ref[:, vs] = oi + os
        st_ref[d, h] = cdec_ref[d, h] * it[9] + up


def mixer_retention(z, n_batch, n_tok, n_ctx, decay_logit, n_heads):
    qw, vw = n_heads * C_QK, n_heads * C_V
    z3 = z.reshape(n_batch, n_tok, z.shape[1])
    nc, cc = n_tok // CHUNK, n_ctx // CHUNK
    bwd = functools.partial(_bwd_chunk, ctx_chunks=cc, n_chunks=nc)
    log_gamma = jax.nn.log_sigmoid(decay_logit.astype(F32))[:, :, None, None]
    idx = jnp.arange(CHUNK, dtype=F32)
    diff = idx[:, None] - idx[None, :]
    dmat_f = jnp.where(diff >= 0, jnp.exp(jnp.maximum(diff, 0.0) * log_gamma[0]), 0.0)
    dmat_b = jnp.where(diff <= 0, jnp.exp(jnp.maximum(-diff, 0.0) * log_gamma[1]), 0.0)
    dmat = jnp.stack([dmat_f, dmat_b])
    ones = jnp.ones((1, 1, 1, C_QK), F32)
    pos_f, pos_b = idx[None, None, :, None], (CHUNK - 1 - idx)[None, None, :, None]
    lg = log_gamma
    qdec = jnp.concatenate([jnp.exp((pos_f + 1) * lg[0:1]), jnp.exp((pos_b + 1) * lg[1:2])]) * ones
    kdec = jnp.concatenate([jnp.exp((CHUNK - 1 - pos_f) * lg[0:1]), jnp.exp((CHUNK - 1 - pos_b) * lg[1:2])]) * ones
    cdec = jnp.exp(CHUNK * log_gamma[:, :, 0, 0])
    cos2, sin2 = _rotary_tables(n_tok, n_ctx)
    full = lambda shape: pl.BlockSpec(shape, lambda b, j: (0,) * len(shape))
    tab = lambda order: pl.BlockSpec((CHUNK, C_QK), lambda b, j: (order(j), 0))
    ident = lambda j: j
    vcb = 2 * qw // vw
    assert vcb * vw == 2 * qw
    def specs(order):
        return [pl.BlockSpec((None, CHUNK, qw), lambda b, j: (b, order(j), 0)),
                pl.BlockSpec((None, CHUNK, qw), lambda b, j: (b, order(j), 1)),
                pl.BlockSpec((None, CHUNK, vw), lambda b, j: (b, order(j), vcb)),
                tab(order), tab(order)]
    o_f, o_b = pl.pallas_call(
        functools.partial(_ret_kernel, n_heads=n_heads),
        grid=(n_batch, nc),
        in_specs=[pl.BlockSpec(memory_space=pltpu.SMEM), full((2, n_heads, CHUNK, CHUNK)),
                  full((2, n_heads, CHUNK, C_QK)), full((2, n_heads, CHUNK, C_QK))] + specs(ident) + specs(bwd),
        out_specs=[pl.BlockSpec((None, CHUNK, vw), lambda b, j: (b, j, 0)),
                   pl.BlockSpec((None, CHUNK, vw), lambda b, j: (b, bwd(j), 0))],
        out_shape=[jax.ShapeDtypeStruct((n_batch, n_tok, vw), F32)] * 2,
        scratch_shapes=[pltpu.VMEM((2, n_heads, C_QK, C_V), F32)],
        compiler_params=_cparams("parallel", "arbitrary"),
        name="retention_scan",
    )(cdec, dmat, qdec, kdec, z3, z3, z3, cos2, sin2, z3, z3, z3, cos2, sin2)
    return o_f.reshape(-1, vw), o_b.reshape(-1, vw)


def _gdn_conv_kernel(prev_ref, cur_ref, next_ref, w_ref, o_ref, *, tiles_per_row, ctx_tiles, n_heads, hd):
    i = pl.program_id(0)
    r = i % tiles_per_row
    first = jnp.logical_or(r == 0, r == ctx_tiles)
    last = jnp.logical_or(r == ctx_tiles - 1, r == tiles_per_row - 1)
    x = cur_ref[...].astype(F32)
    rows = x.shape[0]
    rid = lax.broadcasted_iota(jnp.int32, x.shape, 0)
    hp = prev_ref.shape[0]
    x_prev = jnp.where(first, 0.0, prev_ref[hp - 1:hp, :].astype(F32))
    x_next = jnp.where(last, 0.0, next_ref[0:1, :].astype(F32))
    left = jnp.where(rid == 0, x_prev, pltpu.roll(x, 1, axis=0))
    right = jnp.where(rid == rows - 1, x_next, pltpu.roll(x, rows - 1, axis=0))
    w = w_ref[...]
    y = _silu(left * w[0:1, :] + x * w[1:2, :] + right * w[2:3, :])
    width = n_heads * hd
    for h in range(3 * n_heads):
        sl = slice(h * hd, (h + 1) * hd)
        yh = y[:, sl]
        if h < 2 * n_heads:
            yh = yh * lax.rsqrt(jnp.sum(yh * yh, axis=-1, keepdims=True) + EPS)
            if h < n_heads:
                yh = yh * (hd ** -0.5)
        o_ref[:, sl] = yh.astype(o_ref.dtype)


def _gdn_kernel(qf_ref, kf_ref, vf_ref, cf_ref, rf_ref, qb_ref, kb_ref, vb_ref, cb_ref, rb_ref,
                of_ref, ob_ref, st_ref, *, n_heads, hd):
    @pl.when(pl.program_id(1) == 0)
    def _():
        st_ref[...] = jnp.zeros_like(st_ref)

    c = qf_ref.shape[0]
    eye = (lax.broadcasted_iota(jnp.int32, (c, c), 0) == lax.broadcasted_iota(jnp.int32, (c, c), 1)).astype(F32)
    items = []
    for d, (q_ref, k_ref, v_ref, col_ref, row_ref, o_ref) in enumerate(
            ((qf_ref, kf_ref, vf_ref, cf_ref, rf_ref, of_ref), (qb_ref, kb_ref, vb_ref, cb_ref, rb_ref, ob_ref))):
        lower = d == 0
        incl = _tri_mask(c, lower)
        strict = jnp.logical_and(incl, eye == 0.0)
        col = col_ref[...]
        row = row_ref[...]
        b_cols = _tri_dot(incl.astype(BF16), col)
        b_rows = _dot_tri(row, _tri_mask(c, not lower).astype(BF16))
        for h in range(n_heads):
            sl = slice(h * hd, (h + 1) * hd)
            ib, ig = d * n_heads + h, (2 + d) * n_heads + h
            be = col[:, ib:ib + 1]
            bc = b_cols[:, ig:ig + 1]
            br = b_rows[ig:ig + 1, :]
            b_end = bc[c - 1:c, :] if lower else bc[0:1, :]
            gam_i = jnp.exp(jnp.where(incl, bc - br, -1e30))
            gam_s = jnp.where(strict, gam_i, 0.0)
            qh, kh, vh = q_ref[:, sl], k_ref[:, sl], v_ref[:, sl]
            eb = jnp.exp(bc)
            rhs = jnp.concatenate([be * vh.astype(F32), (be * eb) * kh.astype(F32)], axis=1).astype(BF16)
            items.append(dict(d=d, h=h, sl=sl, o_ref=o_ref, be=be, gam_i=gam_i, gam_s=gam_s, q=qh, k=kh, rhs=rhs,
                              qe=(qh.astype(F32) * eb).astype(BF16), dec=jnp.exp(b_end),
                              ke=(kh.astype(F32) * jnp.exp(b_end - bc)).astype(BF16), s=st_ref[d, h]))
    kk = [_dot_nt(it["k"], it["k"]) for it in items]
    qk = [_dot_nt(it["q"], it["k"]) for it in items]
    qs = [_dot(it["qe"], it["s"]) for it in items]
    ns = [it["be"] * a * it["gam_s"] for it, a in zip(items, kk)]
    invs = [eye - n for n in ns]
    ps = [_dot(n, n) for n in ns]
    levels = int(np.log2(c)) - 1
    for lvl in range(levels):
        prods = [_dot(inv, p) for inv, p in zip(invs, ps)]
        if lvl < levels - 1:
            ps = [_dot(p, p) for p in ps]
        invs = [inv + pr for inv, pr in zip(invs, prods)]
    sols = [_dot(inv, it["rhs"]) for inv, it in zip(invs, items)]
    sks = [_dot(sol[:, hd:], it["s"]) for sol, it in zip(sols, items)]
    us = [sol[:, :hd] - sk for sol, sk in zip(sols, sks)]
    intra = [_dot(a * it["gam_i"], u) for a, it, u in zip(qk, items, us)]
    upd = [_dot_tn(it["ke"], u) for it, u in zip(items, us)]
    for it, oi, os, up in zip(items, intra, qs, upd):
        it["o_ref"][:, it["sl"]] = oi + os
        st_ref[it["d"], it["h"]] = it["dec"] * it["s"] + up


def mixer_gdn(z, n_batch, n_tok, n_ctx, col0, conv_w, a_log, dt_bias, n_heads, hd):
    t, zw = z.shape
    width = n_heads * hd
    tiles_per_row, ctx_tiles = n_tok // ROW_TILE, n_ctx // ROW_TILE
    cb0 = col0 // (3 * width)
    assert cb0 * 3 * width == col0
    halo = 8
    hpt = ROW_TILE // halo
    n_halo = t // halo
    qkv = pl.pallas_call(
        functools.partial(_gdn_conv_kernel, tiles_per_row=tiles_per_row, ctx_tiles=ctx_tiles, n_heads=n_heads, hd=hd),
        grid=(t // ROW_TILE,),
        in_specs=[pl.BlockSpec((halo, 3 * width), lambda i: (jnp.maximum(i * hpt - 1, 0), cb0)),
                  pl.BlockSpec((ROW_TILE, 3 * width), lambda i: (i, cb0)),
                  pl.BlockSpec((halo, 3 * width), lambda i: (jnp.minimum((i + 1) * hpt, n_halo - 1), cb0)),
                  pl.BlockSpec((3, 3 * width), lambda i: (0, 0))],
        out_specs=pl.BlockSpec((ROW_TILE, 3 * width), lambda i: (i, 0)),
        out_shape=jax.ShapeDtypeStruct((t, 3 * width), BF16),
        compiler_params=_cparams("parallel"),
        name="gdn_conv",
    )(z, z, z, conv_w.astype(F32))

    sc0 = col0 + 4 * width
    small = z[:, sc0:sc0 + 4 * n_heads].astype(F32)
    a_log, dt_bias = a_log.astype(F32), dt_bias.astype(F32)
    be = jax.nn.sigmoid(small[:, :2 * n_heads])
    la_f = -jnp.exp(a_log[0]) * jax.nn.softplus(small[:, 2 * n_heads:3 * n_heads] + dt_bias[0])
    la_b = -jnp.exp(a_log[1]) * jax.nn.softplus(small[:, 3 * n_heads:] + dt_bias[1])
    cols = jnp.concatenate([be, la_f, la_b], axis=1)
    nc, cc = n_tok // CHUNK, n_ctx // CHUNK
    cols3 = cols.reshape(n_batch, n_tok, 4 * n_heads)
    rows4 = jnp.transpose(cols.reshape(n_batch, nc, CHUNK, 4 * n_heads), (0, 1, 3, 2))
    qkv3 = qkv.reshape(n_batch, n_tok, 3 * width)
    bwd = functools.partial(_bwd_chunk, ctx_chunks=cc, n_chunks=nc)
    ident = lambda j: j
    def specs(order):
        return [pl.BlockSpec((None, CHUNK, width), lambda b, j: (b, order(j), 0)),
                pl.BlockSpec((None, CHUNK, width), lambda b, j: (b, order(j), 1)),
                pl.BlockSpec((None, CHUNK, width), lambda b, j: (b, order(j), 2)),
                pl.BlockSpec((None, CHUNK, 4 * n_heads), lambda b, j: (b, order(j), 0)),
                pl.BlockSpec((None, None, 4 * n_heads, CHUNK), lambda b, j: (b, order(j), 0, 0))]
    o_f, o_b = pl.pallas_call(
        functools.partial(_gdn_kernel, n_heads=n_heads, hd=hd),
        grid=(n_batch, nc),
        in_specs=specs(ident) + specs(bwd),
        out_specs=[pl.BlockSpec((None, CHUNK, width), lambda b, j: (b, j, 0)),
                   pl.BlockSpec((None, CHUNK, width), lambda b, j: (b, bwd(j), 0))],
        out_shape=[jax.ShapeDtypeStruct((n_batch, n_tok, width), F32)] * 2,
        scratch_shapes=[pltpu.VMEM((2, n_heads, hd, hd), F32)],
        compiler_params=_cparams("parallel", "arbitrary"),
        name="gdn_scan",
    )(qkv3, qkv3, qkv3, cols3, rows4, qkv3, qkv3, qkv3, cols3, rows4)
    return o_f.reshape(-1, width), o_b.reshape(-1, width)


def _first_max(x, idx, sentinel):
    m = jnp.max(x, axis=0, keepdims=True)
    return m, jnp.min(jnp.where(x == m, idx, sentinel), axis=0, keepdims=True)


def _router_kernel(h_ref, wt_ref, bias_ref, e_ref, g_ref, rank_ref, cnt_ref, carry_ref):
    @pl.when(pl.program_id(0) == 0)
    def _():
        carry_ref[...] = jnp.zeros_like(carry_ref)

    cols = h_ref.shape[0]
    gsz = N_EXPERTS // N_GROUPS
    neg = -jnp.inf
    logits = lax.dot_general(wt_ref[...], h_ref[...], (((1,), (1,)), ((), ())), preferred_element_type=F32)
    scores = jax.nn.sigmoid(logits)
    sel = scores + bias_ref[...]
    i_g = lax.broadcasted_iota(jnp.int32, (gsz, cols), 0)
    blocks, g_scores = [], []
    for g in range(N_GROUPS):
        blk = sel[g * gsz:(g + 1) * gsz, :]
        m1, first = _first_max(blk, i_g, gsz)
        m2 = jnp.max(jnp.where(i_g == first, neg, blk), axis=0, keepdims=True)
        blocks.append(blk)
        g_scores.append(m1 + m2)
    keep = [jnp.zeros((1, cols), jnp.bool_) for _ in range(N_GROUPS)]
    for _ in range(TOPK_GROUPS):
        best = functools.reduce(jnp.maximum, g_scores)
        found = jnp.zeros((1, cols), jnp.bool_)
        for g in range(N_GROUPS):
            pick = jnp.logical_and(g_scores[g] == best, jnp.logical_not(found))
            found = jnp.logical_or(found, pick)
            keep[g] = jnp.logical_or(keep[g], pick)
            g_scores[g] = jnp.where(pick, neg, g_scores[g])
    cur = jnp.concatenate([jnp.where(keep[g], blocks[g], neg) for g in range(N_GROUPS)], axis=0)
    i_e = lax.broadcasted_iota(jnp.int32, (N_EXPERTS, cols), 0)
    picks, gates = [], []
    for _ in range(TOP_K):
        _, idx = _first_max(cur, i_e, N_EXPERTS)
        pick = i_e == idx
        picks.append((idx, pick))
        gates.append(jnp.sum(jnp.where(pick, scores, 0.0), axis=0, keepdims=True))
        cur = jnp.where(pick, neg, cur)
    total = functools.reduce(jnp.add, gates)
    chosen = functools.reduce(jnp.logical_or, [p for _, p in picks]).astype(F32)
    r = lax.broadcasted_iota(jnp.int32, (cols, cols), 0)
    c = lax.broadcasted_iota(jnp.int32, (cols, cols), 1)
    before = jnp.dot(chosen.astype(BF16), (r < c).astype(BF16), preferred_element_type=F32) + carry_ref[...]
    for k, (idx, pick) in enumerate(picks):
        e_ref[k:k + 1, :] = idx
        g_ref[k:k + 1, :] = gates[k] / total * ROUTED_SCALE
        rank_ref[k:k + 1, :] = jnp.sum(jnp.where(pick, before, 0.0), axis=0, keepdims=True).astype(jnp.int32)
    carry_ref[...] += jnp.sum(chosen, axis=1, keepdims=True)
    cnt_ref[...] = carry_ref[...]


def _slot_fetch(slots_hbm, idx_smem, sem_idx, tile, buf):
    return pltpu.make_async_copy(slots_hbm.at[tile], idx_smem.at[buf], sem_idx.at[buf])


def _expert_kernel(be_ref, nu_ref, tok_hbm, hp_hbm, wg_ref, wu_ref, wd_ref, o_ref,
                   wg_s, wu_s, wd_s, idx_smem, x_buf, sem_idx, sem_rows):
    i = pl.program_id(0)
    n_live = nu_ref[0]
    live = i < n_live
    new_expert = jnp.logical_or(i == 0, be_ref[i] != be_ref[jnp.maximum(i - 1, 0)])
    rows = x_buf.shape[1]
    idx_rows = idx_smem.shape[1]

    def idx_fetch(blk):
        return pltpu.make_async_copy(tok_hbm.at[blk], idx_smem.at[blk % 2], sem_idx.at[blk % 2])

    def gather_start(blk):
        b = blk % 2
        for q in range(idx_rows):
            def body(r, carry):
                pltpu.make_async_copy(hp_hbm.at[pl.ds(idx_smem[b, q, r], 1)],
                                      x_buf.at[b, pl.ds(q * LANES + r, 1)], sem_rows.at[b]).start(priority=q % 2)
                return carry
            lax.fori_loop(0, LANES, body, 0, unroll=8)

    @pl.when(i == 0)
    def _():
        idx_fetch(0).start()
        idx_fetch(0).wait()
        gather_start(0)

        @pl.when(n_live > 1)
        def _():
            idx_fetch(1).start()

    @pl.when(i + 1 < n_live)
    def _():
        idx_fetch(i + 1).wait()
        gather_start(i + 1)

        @pl.when(i + 2 < n_live)
        def _():
            idx_fetch(i + 2).start()

    @pl.when(jnp.logical_and(live, new_expert))
    def _():
        wg_s[...] = wg_ref[...].astype(BF16)
        wu_s[...] = wu_ref[...].astype(BF16)
        wd_s[...] = wd_ref[...].astype(BF16)

    @pl.when(live)
    def _():
        b = i % 2
        pltpu.make_async_copy(hp_hbm.at[pl.ds(0, rows)], x_buf.at[b], sem_rows.at[b]).wait()
        lo, hi = _unpack_halves(_load_rows(x_buf.at[b], 0, rows))
        lo, hi = lo.astype(BF16), hi.astype(BF16)
        half = lo.shape[1]
        g = (jnp.dot(lo, wg_s[:half, :], preferred_element_type=F32)
             + jnp.dot(hi, wg_s[half:, :], preferred_element_type=F32))
        u = (jnp.dot(lo, wu_s[:half, :], preferred_element_type=F32)
             + jnp.dot(hi, wu_s[half:, :], preferred_element_type=F32))
        a = (_silu(g) * u).astype(BF16)
        _store_rows(o_ref, _pack_halves(jnp.dot(a, wd_s[...], preferred_element_type=F32)))

    @pl.when(jnp.logical_not(live))
    def _():
        o_ref[...] = jnp.zeros_like(o_ref)


def _combine_kernel(slots_hbm, y_hbm, h_ref, gate_ref, xs_ref, mg_ref, wg_ref, wu_ref, wd_ref, o_ref,
                    idx_smem, rows_buf, sem_idx, sem_rows):
    i = pl.program_id(0)
    rows = h_ref.shape[0]

    @pl.when(i == 0)
    def _():
        _slot_fetch(slots_hbm, idx_smem, sem_idx, 0, 0).start()

    buf = i % 2
    _slot_fetch(slots_hbm, idx_smem, sem_idx, i, buf).wait()

    @pl.when(i + 1 < pl.num_programs(0))
    def _():
        _slot_fetch(slots_hbm, idx_smem, sem_idx, i + 1, 1 - buf).start()

    def body(r, carry):
        for k in range(TOP_K):
            pltpu.make_async_copy(y_hbm.at[pl.ds(idx_smem[buf, k, r], 1)], rows_buf.at[pl.ds(k * rows + r, 1)],
                                  sem_rows).start(priority=k % 2)
        return carry

    lax.fori_loop(0, rows, body, 0, unroll=8)

    h = h_ref[...]
    g = jnp.dot(h, wg_ref[...], preferred_element_type=F32)
    u = jnp.dot(h, wu_ref[...], preferred_element_type=F32)
    a = (_silu(g) * u).astype(BF16)
    y = jnp.dot(a, wd_ref[...], preferred_element_type=F32)

    pltpu.make_async_copy(y_hbm.at[pl.ds(0, rows * TOP_K)], rows_buf, sem_rows).wait()
    half = y.shape[1] // 2
    acc_lo, acc_hi = y[:, :half], y[:, half:]
    gate = gate_ref[...]
    for k in range(TOP_K):
        lo, hi = _unpack_halves(_load_rows(rows_buf, k * rows, rows))
        acc_lo = acc_lo + gate[:, k:k + 1] * lo
        acc_hi = acc_hi + gate[:, k:k + 1] * hi
    mg = mg_ref[...]
    o_ref[:, :half] = xs_ref[:, :half] + mg[:, :half] * acc_lo
    o_ref[:, half:] = xs_ref[:, half:] + mg[:, half:] * acc_hi


def route(h2, w_router, router_bias):
    t, d = h2.shape
    n_tiles = t // ROW_TILE
    n_exp = w_router.shape[1]
    return pl.pallas_call(
        _router_kernel,
        grid=(n_tiles,),
        in_specs=[pl.BlockSpec((ROW_TILE, d), lambda i: (i, 0)),
                  pl.BlockSpec((n_exp, d), lambda i: (0, 0)),
                  pl.BlockSpec((n_exp, 1), lambda i: (0, 0))],
        out_specs=[pl.BlockSpec((TOP_K, ROW_TILE), lambda i: (0, i)),
                   pl.BlockSpec((TOP_K, ROW_TILE), lambda i: (0, i)),
                   pl.BlockSpec((TOP_K, ROW_TILE), lambda i: (0, i)),
                   pl.BlockSpec((n_exp, 1), lambda i: (0, 0))],
        out_shape=[jax.ShapeDtypeStruct((TOP_K, t), jnp.int32), jax.ShapeDtypeStruct((TOP_K, t), F32),
                   jax.ShapeDtypeStruct((TOP_K, t), jnp.int32), jax.ShapeDtypeStruct((n_exp, 1), F32)],
        scratch_shapes=[pltpu.VMEM((n_exp, 1), F32)],
        compiler_params=_cparams("arbitrary"),
        name="router_topk",
    )(h2, jnp.transpose(w_router).astype(BF16), router_bias.astype(F32).reshape(n_exp, 1))


def moe_block(h2, h2_packed, xs, mod, k_gate, tiles_per_row, n_batch,
              w_router, router_bias, w_gate, w_up, w_down, ws_gate, ws_up, ws_down):
    t, d = h2.shape
    row_shape = h2_packed.shape[1:]
    de = ws_gate.shape[1]
    n_tiles = t // ROW_TILE
    eidx, gate_t, rank, cnt = route(h2, w_router, router_bias)
    n_blocks = -(-t * TOP_K // MOE_BLOCK) + N_EXPERTS
    counts = cnt[:, 0].astype(jnp.int32)
    padded = (counts + MOE_BLOCK - 1) // MOE_BLOCK * MOE_BLOCK
    pad_end = jnp.cumsum(padded).astype(jnp.int32)
    pad_start = pad_end - padded
    e_ids = jnp.arange(N_EXPERTS, dtype=jnp.int32)
    slot = rank + jnp.sum(jnp.where(eidx[:, :, None] == e_ids, pad_start, 0), axis=-1)
    slots = jnp.transpose(slot.reshape(TOP_K, n_tiles, ROW_TILE), (1, 0, 2))
    gate = jnp.transpose(gate_t)
    block_start = jnp.arange(n_blocks, dtype=jnp.int32) * MOE_BLOCK
    block_e = jnp.minimum(jnp.sum(pad_end[None, :] <= block_start[:, None], axis=1), N_EXPERTS - 1).astype(jnp.int32)
    n_used = (pad_end[-1] // MOE_BLOCK).astype(jnp.int32).reshape(1)

    idx_scratch = [pltpu.SMEM((2, TOP_K, ROW_TILE), jnp.int32)]
    token_ids = jnp.broadcast_to(jnp.arange(t, dtype=jnp.int32), (TOP_K, t))
    slot_tok = jnp.zeros((n_blocks * MOE_BLOCK,), jnp.int32).at[slot.reshape(-1)].set(
        token_ids.reshape(-1), unique_indices=True)
    slot_tok = slot_tok.reshape(n_blocks, MOE_BLOCK // LANES, LANES)

    y_sorted = pl.pallas_call(
        _expert_kernel,
        grid_spec=pltpu.PrefetchScalarGridSpec(
            num_scalar_prefetch=2,
            grid=(n_blocks,),
            in_specs=[pl.BlockSpec(memory_space=pl.ANY),
                      pl.BlockSpec(memory_space=pl.ANY),
                      pl.BlockSpec((None, d, de), lambda i, be, nu: (be[i], 0, 0)),
                      pl.BlockSpec((None, d, de), lambda i, be, nu: (be[i], 0, 0)),
                      pl.BlockSpec((None, de, d), lambda i, be, nu: (be[i], 0, 0))],
            out_specs=pl.BlockSpec((MOE_BLOCK,) + row_shape, lambda i, be, nu: (i, 0, 0)),
            scratch_shapes=[pltpu.VMEM((d, de), BF16), pltpu.VMEM((d, de), BF16), pltpu.VMEM((de, d), BF16),
                            pltpu.SMEM((2, MOE_BLOCK // LANES, LANES), jnp.int32),
                            pltpu.VMEM((2, MOE_BLOCK) + row_shape, jnp.uint32),
                            pltpu.SemaphoreType.DMA((2,)), pltpu.SemaphoreType.DMA((2,))],
        ),
        out_shape=jax.ShapeDtypeStruct((n_blocks * MOE_BLOCK,) + row_shape, jnp.uint32),
        compiler_params=_cparams("arbitrary"),
        name="routed_experts",
    )(block_e, n_used, slot_tok, h2_packed, w_gate, w_up, w_down)

    seg = functools.partial(_seg_of_tile, tiles_per_row=tiles_per_row, n_batch=n_batch)
    return pl.pallas_call(
        _combine_kernel,
        grid=(n_tiles,),
        in_specs=[pl.BlockSpec(memory_space=pl.ANY),
                  pl.BlockSpec(memory_space=pl.ANY),
                  pl.BlockSpec((ROW_TILE, d), lambda i: (i, 0)),
                  pl.BlockSpec((ROW_TILE, TOP_K), lambda i: (i, 0)),
                  pl.BlockSpec((ROW_TILE, d), lambda i: (i, 0)),
                  pl.BlockSpec((None, None, 1, d), lambda i: (seg(i), k_gate, 0, 0)),
                  pl.BlockSpec((d, de), lambda i: (0, 0)),
                  pl.BlockSpec((d, de), lambda i: (0, 0)),
                  pl.BlockSpec((de, d), lambda i: (0, 0))],
        out_specs=pl.BlockSpec((ROW_TILE, d), lambda i: (i, 0)),
        out_shape=jax.ShapeDtypeStruct((t, d), F32),
        scratch_shapes=idx_scratch + [pltpu.VMEM((TOP_K * ROW_TILE,) + row_shape, jnp.uint32),
                                      pltpu.SemaphoreType.DMA((2,)), pltpu.SemaphoreType.DMA(())],
        compiler_params=_cparams("arbitrary"),
        name="moe_combine",
    )(slots, y_sorted, h2, gate, xs, mod, ws_gate.astype(BF16), ws_up.astype(BF16), ws_down.astype(BF16))


def kernel(x, c, ctx, c_ctx, w_ada, b_ada, norm1_g, norm2_g, w_in_ab, s5_lambda_re, s5_lambda_im, s5_log_dt, s5_b_re, s5_b_im, s5_c_re, s5_c_im, s5_d, s5_w_glu, hgrn_lb, w_in_cd, ret_decay_logit, gdn_conv_w, gdn_a_log, gdn_dt_bias, gdn_norm_g, w_out, w_router, router_bias, w_exp_gate, w_exp_up, w_exp_down, w_sh_gate, w_sh_up, w_sh_down, final_norm_g):
    n_batch, n_lat, d = x.shape
    n_ctx = ctx.shape[1]
    n_tok = n_ctx + n_lat
    depth = w_ada.shape[0]
    assert n_ctx % ROW_TILE == 0 and n_lat % ROW_TILE == 0
    tiles_per_row, ctx_tiles = n_tok // ROW_TILE, n_ctx // ROW_TILE
    assert ctx_tiles == 1
    t = n_batch * n_tok

    xs = jnp.concatenate([ctx, x], axis=1).reshape(t, d)
    lb_all = jnp.cumsum(jax.nn.softmax(hgrn_lb.astype(F32), axis=0), axis=0)
    cond = jnp.concatenate([c, c_ctx[None]], axis=0)
    cond = jnp.pad(jax.nn.silu(cond), ((0, 8 - (n_batch + 1)), (0, 0)))
    ones_g = jnp.ones((D_HEAD,), F32)

    for i in range(depth):
        j = i // 2
        mod = matmul(cond, w_ada[i], F32, 8, 1024, "ada_mod")[:n_batch + 1] + b_ada[i]
        mod = mod.reshape(n_batch + 1, 6, 1, d)
        h = norm_mod(xs, norm1_g[i], mod, 0, 1, tiles_per_row, n_batch)
        if i % 2 == 0:
            z = matmul(h, w_in_ab[j].astype(BF16), F32, 512, 1024, "in_proj")
            a_width = s5_d.shape[1]
            b_heads = (z.shape[1] - a_width) // 5 // B_HEAD
            y_a = mixer_s5(z, n_batch, n_tok, n_ctx, s5_lambda_re[j], s5_lambda_im[j], s5_log_dt[j],
                           s5_b_re[j], s5_b_im[j], s5_c_re[j], s5_c_im[j], s5_d[j], s5_w_glu[j])
            o_f, o_b = mixer_gla(z, n_batch, n_tok, n_ctx, a_width, lb_all[j], b_heads, B_HEAD)
            gcb = (a_width + 4 * b_heads * B_HEAD) // (b_heads * B_HEAD)
            y_b = post_norm_gate(o_f, o_b, z, gcb, B_HEAD, jnp.ones((B_HEAD,), F32))
            m1, m2 = y_a, y_b
        else:
            cd_cols = w_in_cd.shape[2]
            cd_pad = -(-cd_cols // 1024) * 1024
            w_cd = jnp.pad(w_in_cd[j].astype(BF16), ((0, 0), (0, cd_pad - cd_cols)))
            z = matmul(h, w_cd, F32, 512, 1024, "in_proj")
            c_heads = ret_decay_logit.shape[2]
            d_heads = gdn_a_log.shape[2]
            o_f, o_b = mixer_retention(z, n_batch, n_tok, n_ctx, ret_decay_logit[j], c_heads)
            y_c = post_norm_gate(o_f, o_b, z, (2 * c_heads * C_QK + c_heads * C_V) // (c_heads * C_V), C_V,
                                 jnp.ones((C_V,), F32))
            col0 = 2 * c_heads * C_QK + 2 * c_heads * C_V
            o_f, o_b = mixer_gdn(z, n_batch, n_tok, n_ctx, col0, gdn_conv_w[j], gdn_a_log[j], gdn_dt_bias[j],
                                 d_heads, D_HEAD)
            y_d = post_norm_gate(o_f, o_b, z, (col0 + 3 * d_heads * D_HEAD) // (d_heads * D_HEAD), D_HEAD,
                                 gdn_norm_g[j])
            m1, m2 = y_c, y_d
        xs = out_proj(m1, m2, w_out[i].astype(BF16), xs, mod, 2, tiles_per_row, n_batch)
        h2, h2_packed = norm_mod(xs, norm2_g[i], mod, 3, 4, tiles_per_row, n_batch, packed=True)
        xs = moe_block(h2, h2_packed, xs, mod, 5, tiles_per_row, n_batch, w_router[i], router_bias[i],
                       w_exp_gate[i], w_exp_up[i], w_exp_down[i], w_sh_gate[i], w_sh_up[i], w_sh_down[i])
    out = final_norm(xs, final_norm_g, n_batch, tiles_per_row, ctx_tiles)
    return out.reshape(n_batch, n_lat, d)
```

```python
import functools

import numpy as np
import jax
import jax.numpy as jnp
from jax import lax
from jax.experimental import pallas as pl
from jax.experimental.pallas import tpu as pltpu

F32 = jnp.float32
BF16 = jnp.bfloat16

EPS = 1e-6
GRID_W = 64
CHUNK = 64
S5_L = 32
S5_H = 16
S5_P = 64
B_HEAD = 128
C_QK = 128
C_V = 256
D_HEAD = 128
ROPE_BASE = 10000.0
N_EXPERTS = 64
N_GROUPS = 8
TOPK_GROUPS = 4
TOP_K = 8
ROUTED_SCALE = 2.5
MOE_BLOCK = 512
ROW_TILE = 256
VMEM_LIMIT_V7X = 56 * 1024 * 1024
EXP_CLAMP = 80.0
GLA_SUB = 16


def _cparams(*sem):
    return pltpu.CompilerParams(dimension_semantics=sem, vmem_limit_bytes=VMEM_LIMIT_V7X)


def _silu(x):
    return x * jax.nn.sigmoid(x)


def _dot(a, b):
    return jnp.dot(a.astype(BF16), b.astype(BF16), preferred_element_type=F32)


def _dot_nt(a, b):
    return lax.dot_general(a.astype(BF16), b.astype(BF16), (((1,), (1,)), ((), ())), preferred_element_type=F32)


def _dot_tn(a, b):
    return lax.dot_general(a.astype(BF16), b.astype(BF16), (((0,), (0,)), ((), ())), preferred_element_type=F32)


def _split3(x):
    x1 = x.astype(BF16)
    r = x - x1.astype(F32)
    x2 = r.astype(BF16)
    x3 = (r - x2.astype(F32)).astype(BF16)
    return x1, x2, x3


def _tri_dot(tri, x):
    return sum(jnp.dot(tri, p, preferred_element_type=F32) for p in _split3(x))


def _dot_tri(x, tri):
    return sum(jnp.dot(p, tri, preferred_element_type=F32) for p in _split3(x))


def _tri_mask(n, lower):
    r = lax.broadcasted_iota(jnp.int32, (n, n), 0)
    c = lax.broadcasted_iota(jnp.int32, (n, n), 1)
    return (r >= c) if lower else (r <= c)


def _seg_of_tile(i, tiles_per_row, n_batch):
    return jnp.where(i % tiles_per_row == 0, n_batch, i // tiles_per_row)


def _pack_halves(y):
    half = y.shape[1] // 2
    lo = lax.bitcast_convert_type(y[:, :half].astype(BF16).astype(F32), jnp.uint32) >> 16
    hi = lax.bitcast_convert_type(y[:, half:].astype(BF16).astype(F32), jnp.uint32) & jnp.uint32(0xFFFF0000)
    return hi | lo


def _unpack_halves(w):
    lo = lax.bitcast_convert_type(w << 16, F32)
    hi = lax.bitcast_convert_type(w & jnp.uint32(0xFFFF0000), F32)
    return lo, hi


LANES = 128


def _store_rows(ref, packed):
    pieces = jnp.stack([packed[:, s * LANES:(s + 1) * LANES] for s in range(ref.shape[1])], axis=0)
    ref[...] = pltpu.einshape("stl->tsl", pieces)


def _load_rows(ref, row0, rows):
    x = pltpu.einshape("tsl->stl", ref[row0:row0 + rows])
    return jnp.concatenate([x[s] for s in range(ref.shape[1])], axis=1)


def _norm_mod_kernel(x_ref, g_ref, shift_ref, scale_ref, o_ref, *packed_ref):
    x = x_ref[...]
    y = x * lax.rsqrt(jnp.mean(x * x, axis=-1, keepdims=True) + EPS) * g_ref[...]
    y = y * (1.0 + scale_ref[...]) + shift_ref[...]
    o_ref[...] = y.astype(o_ref.dtype)
    if packed_ref:
        _store_rows(packed_ref[0], _pack_halves(y))


def norm_mod(xs, g, mod, k_shift, k_scale, tiles_per_row, n_batch, packed=False):
    t, d = xs.shape
    seg = functools.partial(_seg_of_tile, tiles_per_row=tiles_per_row, n_batch=n_batch)
    out_specs = [pl.BlockSpec((ROW_TILE, d), lambda i: (i, 0))]
    out_shape = [jax.ShapeDtypeStruct((t, d), BF16)]
    if packed:
        out_specs.append(pl.BlockSpec((ROW_TILE, d // 2 // LANES, LANES), lambda i: (i, 0, 0)))
        out_shape.append(jax.ShapeDtypeStruct((t, d // 2 // LANES, LANES), jnp.uint32))
    res = pl.pallas_call(
        _norm_mod_kernel,
        grid=(t // ROW_TILE,),
        in_specs=[
            pl.BlockSpec((ROW_TILE, d), lambda i: (i, 0)),
            pl.BlockSpec((1, d), lambda i: (0, 0)),
            pl.BlockSpec((None, None, 1, d), lambda i: (seg(i), k_shift, 0, 0)),
            pl.BlockSpec((None, None, 1, d), lambda i: (seg(i), k_scale, 0, 0)),
        ],
        out_specs=out_specs,
        out_shape=out_shape,
        compiler_params=_cparams("parallel"),
        name="norm_mod",
    )(xs, g.reshape(1, d), mod, mod)
    return res if packed else res[0]


def _final_norm_kernel(x_ref, g_ref, o_ref):
    x = x_ref[...]
    o_ref[...] = x * lax.rsqrt(jnp.mean(x * x, axis=-1, keepdims=True) + EPS) * g_ref[...]


def final_norm(xs, g, n_batch, tiles_per_row, ctx_tiles):
    t, d = xs.shape
    lat_tiles = tiles_per_row - ctx_tiles
    return pl.pallas_call(
        _final_norm_kernel,
        grid=(n_batch, lat_tiles),
        in_specs=[
            pl.BlockSpec((ROW_TILE, d), lambda b, i: (b * tiles_per_row + ctx_tiles + i, 0)),
            pl.BlockSpec((1, d), lambda b, i: (0, 0)),
        ],
        out_specs=pl.BlockSpec((ROW_TILE, d), lambda b, i: (b * lat_tiles + i, 0)),
        out_shape=jax.ShapeDtypeStruct((n_batch * lat_tiles * ROW_TILE, d), F32),
        compiler_params=_cparams("parallel", "parallel"),
        name="final_norm",
    )(xs, g.reshape(1, d))


def _mm_kernel(a_ref, w_ref, o_ref):
    o_ref[...] = jnp.dot(a_ref[...].astype(BF16), w_ref[...].astype(BF16),
                         preferred_element_type=F32).astype(o_ref.dtype)


def matmul(a, w, out_dtype, tm, tn, name):
    m, k = a.shape
    n = w.shape[1]
    return pl.pallas_call(
        _mm_kernel,
        grid=(n // tn, m // tm),
        in_specs=[pl.BlockSpec((tm, k), lambda j, i: (i, 0)),
                  pl.BlockSpec((k, tn), lambda j, i: (0, j))],
        out_specs=pl.BlockSpec((tm, tn), lambda j, i: (i, j)),
        out_shape=jax.ShapeDtypeStruct((m, n), out_dtype),
        compiler_params=_cparams("parallel", "parallel"),
        name=name,
    )(a, w)


def _out_proj_kernel(a1_ref, a2_ref, w1_ref, w2_ref, res_ref, gate_ref, o_ref):
    y = jnp.dot(a1_ref[...], w1_ref[...], preferred_element_type=F32)
    y += jnp.dot(a2_ref[...], w2_ref[...], preferred_element_type=F32)
    o_ref[...] = res_ref[...] + gate_ref[...] * y


def out_proj(a1, a2, w, xs, mod, k_gate, tiles_per_row, n_batch, tn=1024):
    t, d = xs.shape
    k1, k2 = a1.shape[1], a2.shape[1]
    assert k1 == k2
    seg = functools.partial(_seg_of_tile, tiles_per_row=tiles_per_row, n_batch=n_batch)
    return pl.pallas_call(
        _out_proj_kernel,
        grid=(d // tn, t // ROW_TILE),
        in_specs=[
            pl.BlockSpec((ROW_TILE, k1), lambda j, i: (i, 0)),
            pl.BlockSpec((ROW_TILE, k2), lambda j, i: (i, 0)),
            pl.BlockSpec((k1, tn), lambda j, i: (0, j)),
            pl.BlockSpec((k2, tn), lambda j, i: (1, j)),
            pl.BlockSpec((ROW_TILE, tn), lambda j, i: (i, j)),
            pl.BlockSpec((None, None, 1, tn), lambda j, i: (seg(i), k_gate, 0, j)),
        ],
        out_specs=pl.BlockSpec((ROW_TILE, tn), lambda j, i: (i, j)),
        out_shape=jax.ShapeDtypeStruct((t, d), F32),
        compiler_params=_cparams("parallel", "parallel"),
        name="out_proj",
    )(a1, a2, w, w, xs, mod)


def _post_kernel(of_ref, ob_ref, gate_ref, ng_ref, o_ref, *, head_dim):
    o = of_ref[...] + ob_ref[...]
    g = gate_ref[...].astype(F32)
    width = o.shape[1]
    for h in range(width // head_dim):
        sl = slice(h * head_dim, (h + 1) * head_dim)
        oh = o[:, sl]
        y = oh * lax.rsqrt(jnp.mean(oh * oh, axis=-1, keepdims=True) + EPS) * ng_ref[...]
        o_ref[:, sl] = (y * _silu(g[:, sl])).astype(o_ref.dtype)


def post_norm_gate(o_f, o_b, z, gate_col_block, head_dim, norm_g):
    t, width = o_f.shape
    return pl.pallas_call(
        functools.partial(_post_kernel, head_dim=head_dim),
        grid=(t // ROW_TILE,),
        in_specs=[
            pl.BlockSpec((ROW_TILE, width), lambda i: (i, 0)),
            pl.BlockSpec((ROW_TILE, width), lambda i: (i, 0)),
            pl.BlockSpec((ROW_TILE, width), lambda i: (i, gate_col_block)),
            pl.BlockSpec((1, head_dim), lambda i: (0, 0)),
        ],
        out_specs=pl.BlockSpec((ROW_TILE, width), lambda i: (i, 0)),
        out_shape=jax.ShapeDtypeStruct((t, width), BF16),
        compiler_params=_cparams("parallel"),
        name="post_norm_gate",
    )(o_f, o_b, z, norm_g.reshape(1, head_dim).astype(F32))


def _bwd_chunk(j, ctx_chunks, n_chunks):
    return jnp.where(j < ctx_chunks, ctx_chunks - 1 - j, n_chunks + ctx_chunks - 1 - j)


def _s5_direction_tables(lam_re, lam_im, log_dt, b_re, b_im, c_re, c_im, reverse):
    hi = lax.Precision.HIGHEST
    ln = S5_L
    lam_re, lam_im, b_re, b_im, c_re, c_im = (p.astype(F32) for p in (lam_re, lam_im, b_re, b_im, c_re, c_im))
    dt = jnp.exp(log_dt.astype(F32))[:, None]
    mag = jnp.exp(lam_re * dt)
    ab_re, ab_im = mag * jnp.cos(lam_im * dt), mag * jnp.sin(lam_im * dt)
    den = lam_re * lam_re + lam_im * lam_im
    fr = ((ab_re - 1) * lam_re + ab_im * lam_im) / den
    fi = (ab_im * lam_re - (ab_re - 1) * lam_im) / den
    bb_re = fr[..., None] * b_re - fi[..., None] * b_im
    bb_im = fr[..., None] * b_im + fi[..., None] * b_re
    tau = jnp.arange(ln + 1, dtype=F32)[:, None, None]
    pw = jnp.exp(tau * (lam_re * dt))
    pr, pi = pw * jnp.cos(tau * (lam_im * dt)), pw * jnp.sin(tau * (lam_im * dt))
    abr = pr[..., None] * bb_re - pi[..., None] * bb_im
    abi = pr[..., None] * bb_im + pi[..., None] * bb_re
    kern = (jnp.einsum('ghp,tgpk->tghk', c_re, abr[:ln], precision=hi)
            - jnp.einsum('ghp,tgpk->tghk', c_im, abi[:ln], precision=hi))
    pos = jnp.arange(ln)
    lag = (pos[None, :] - pos[:, None]) if not reverse else (pos[:, None] - pos[None, :])
    toe = jnp.where((lag >= 0)[:, :, None, None, None], kern[jnp.clip(lag, 0, ln - 1)], 0.0)
    g = lam_re.shape[0]
    intra = jnp.transpose(toe, (2, 0, 4, 1, 3)).reshape(g, ln * S5_H, ln * S5_H)
    pw_in = (ln - 1 - pos) if not reverse else pos
    inj = jnp.concatenate([abr[pw_in], abi[pw_in]], axis=2)
    inject = jnp.transpose(inj, (1, 0, 3, 2)).reshape(g, ln * S5_H, 2 * S5_P)
    pw_out = (pos + 1) if not reverse else (ln - pos)
    w_re = c_re[None] * pr[pw_out][:, :, None, :] - c_im[None] * pi[pw_out][:, :, None, :]
    w_im = -(c_re[None] * pi[pw_out][:, :, None, :] + c_im[None] * pr[pw_out][:, :, None, :])
    readout = jnp.transpose(jnp.concatenate([w_re, w_im], axis=3), (1, 3, 0, 2)).reshape(g, 2 * S5_P, ln * S5_H)
    decay = jnp.stack([pr[ln], pi[ln]])
    return intra, inject, readout, decay


def _s5_in_kernel(u_ref, w_ref, yi_ref, s_ref):
    r = jnp.dot(u_ref[...], w_ref[...], preferred_element_type=F32)
    n_intra = yi_ref.shape[-1]
    yi_ref[...] = r[:, :n_intra]
    s_ref[...] = r[:, n_intra:]


def _s5_scan_kernel(a_ref, sf_ref, sb_ref, of_ref, ob_ref, st_ref):
    @pl.when(pl.program_id(1) == 0)
    def _():
        st_ref[...] = jnp.zeros_like(st_ref)

    steps = sf_ref.shape[0]
    for d, (s_ref, o_ref) in enumerate(((sf_ref, of_ref), (sb_ref, ob_ref))):
        ar, ai = a_ref[2 * d], a_ref[2 * d + 1]
        sr, si = st_ref[2 * d], st_ref[2 * d + 1]
        for q in range(steps):
            r = q if d == 0 else steps - 1 - q
            o_ref[r, 0] = sr
            o_ref[r, 1] = si
            sr, si = (ar * sr - ai * si + s_ref[r, 0], ar * si + ai * sr + s_ref[r, 1])
        st_ref[2 * d] = sr
        st_ref[2 * d + 1] = si


def _s5_out_kernel(yi_ref, st_ref, w_ref, u_ref, d_ref, o_ref):
    y = yi_ref[...] + jnp.dot(st_ref[...].astype(BF16), w_ref[...], preferred_element_type=F32)
    o_ref[...] = (y + d_ref[...] * u_ref[...].astype(F32)).astype(o_ref.dtype)


def _glu_kernel(y_ref, w_ref, o_ref):
    y = jax.nn.gelu(y_ref[...].astype(F32))
    o_ref[...] = (y * jax.nn.sigmoid(jnp.dot(y.astype(BF16), w_ref[...], preferred_element_type=F32))
                  ).astype(o_ref.dtype)


def mixer_s5(z, n_batch, n_tok, n_ctx, lam_re, lam_im, log_dt, b_re, b_im, c_re, c_im, d_skip, w_glu):
    t = z.shape[0]
    g = lam_re.shape[1]
    width = g * S5_H
    lh = S5_L * S5_H
    rows = t // S5_L
    tabs = [_s5_direction_tables(lam_re[d], lam_im[d], log_dt[d], b_re[d], b_im[d], c_re[d], c_im[d], d == 1)
            for d in range(2)]
    w_in = jnp.concatenate([tabs[0][0] + tabs[1][0], tabs[0][1], tabs[1][1]], axis=2).astype(BF16)
    w_st = jnp.concatenate([tabs[0][2], tabs[1][2]], axis=1).astype(BF16)
    decay = jnp.concatenate([tabs[0][3], tabs[1][3]], axis=0).reshape(4, g * S5_P // 128, 128)
    u = z[:, :width].reshape(rows, S5_L, g, S5_H)
    u_g = jnp.transpose(u, (2, 0, 1, 3)).reshape(g, rows, lh).astype(BF16)

    n_st = 4 * S5_P
    yi, s_in = pl.pallas_call(
        _s5_in_kernel,
        grid=(g,),
        in_specs=[pl.BlockSpec((None, rows, lh), lambda i: (i, 0, 0)),
                  pl.BlockSpec((None, lh, lh + n_st), lambda i: (i, 0, 0))],
        out_specs=[pl.BlockSpec((None, rows, lh), lambda i: (i, 0, 0)),
                   pl.BlockSpec((None, rows, n_st), lambda i: (i, 0, 0))],
        out_shape=[jax.ShapeDtypeStruct((g, rows, lh), F32), jax.ShapeDtypeStruct((g, rows, n_st), F32)],
        compiler_params=_cparams("parallel"),
        name="s5_in",
    )(u_g, w_in)

    gp = g * S5_P // 128
    s_scan = jnp.transpose(s_in.reshape(g, rows, 4, S5_P), (1, 2, 0, 3)).reshape(rows, 4, gp, 128)
    blocks_per_row = n_tok // S5_L
    cb = n_ctx // S5_L
    assert blocks_per_row % cb == 0
    nblk = blocks_per_row // cb
    bwd = lambda j: jnp.where(j == 0, 0, nblk - j)
    st_f, st_b = pl.pallas_call(
        _s5_scan_kernel,
        grid=(n_batch, nblk),
        in_specs=[pl.BlockSpec((4, gp, 128), lambda b, j: (0, 0, 0)),
                  pl.BlockSpec((cb, 2, gp, 128), lambda b, j: (b * nblk + j, 0, 0, 0)),
                  pl.BlockSpec((cb, 2, gp, 128), lambda b, j: (b * nblk + bwd(j), 1, 0, 0))],
        out_specs=[pl.BlockSpec((cb, 2, gp, 128), lambda b, j: (b * nblk + j, 0, 0, 0)),
                   pl.BlockSpec((cb, 2, gp, 128), lambda b, j: (b * nblk + bwd(j), 0, 0, 0))],
        out_shape=[jax.ShapeDtypeStruct((rows, 2, gp, 128), F32)] * 2,
        scratch_shapes=[pltpu.VMEM((4, gp, 128), F32)],
        compiler_params=_cparams("parallel", "arbitrary"),
        name="s5_scan",
    )(decay, s_scan, s_scan)
    st = jnp.concatenate([st_f, st_b], axis=1).reshape(rows, 4, g, S5_P)
    st_g = jnp.transpose(st, (2, 0, 1, 3)).reshape(g, rows, n_st)

    d_vec = jnp.tile(d_skip.astype(F32).reshape(g, 1, S5_H), (1, S5_L, 1)).reshape(g, 1, lh)
    y_g = pl.pallas_call(
        _s5_out_kernel,
        grid=(g,),
        in_specs=[pl.BlockSpec((None, rows, lh), lambda i: (i, 0, 0)),
                  pl.BlockSpec((None, rows, n_st), lambda i: (i, 0, 0)),
                  pl.BlockSpec((None, n_st, lh), lambda i: (i, 0, 0)),
                  pl.BlockSpec((None, rows, lh), lambda i: (i, 0, 0)),
                  pl.BlockSpec((None, 1, lh), lambda i: (i, 0, 0))],
        out_specs=pl.BlockSpec((None, rows, lh), lambda i: (i, 0, 0)),
        out_shape=jax.ShapeDtypeStruct((g, rows, lh), BF16),
        compiler_params=_cparams("parallel"),
        name="s5_out",
    )(yi, st_g, w_st, u_g, d_vec)
    y = jnp.transpose(y_g.reshape(g, rows, S5_L, S5_H), (1, 2, 0, 3)).reshape(t, width)

    return pl.pallas_call(
        _glu_kernel,
        grid=(t // ROW_TILE,),
        in_specs=[pl.BlockSpec((ROW_TILE, width), lambda i: (i, 0)),
                  pl.BlockSpec((width, width), lambda i: (0, 0))],
        out_specs=pl.BlockSpec((ROW_TILE, width), lambda i: (i, 0)),
        out_shape=jax.ShapeDtypeStruct((t, width), BF16),
        compiler_params=_cparams("parallel"),
        name="s5_glu",
    )(y, w_glu.astype(BF16))


def _gla_kernel(lb_ref, qf_ref, ff_ref, vf_ref, qb_ref, fb_ref, vb_ref, of_ref, ob_ref, st_ref, *, n_heads, hd):
    @pl.when(pl.program_id(1) == 0)
    def _():
        st_ref[...] = jnp.zeros_like(st_ref)

    c = qf_ref.shape[0]
    items = []
    for d, (q_ref, f_ref, v_ref, o_ref) in enumerate(((qf_ref, ff_ref, vf_ref, of_ref),
                                                       (qb_ref, fb_ref, vb_ref, ob_ref))):
        lower = d == 0
        mask = _tri_mask(c, lower)
        tri = mask.astype(BF16)
        lb = lb_ref[d]
        sig = jax.nn.sigmoid(f_ref[...])
        kk = (1.0 - lb) * (1.0 - sig)
        logf = jnp.log(lb + (1.0 - lb) * sig)
        b = _tri_dot(tri, logf)
        b_end = b[c - 1:c, :] if lower else b[0:1, :]
        qs = _silu(q_ref[...])
        qt = (qs * jnp.exp(b)).astype(BF16)
        q_blk, k_blk = [], []
        for blk in range(c // GLA_SUB):
            r0, r1 = blk * GLA_SUB, (blk + 1) * GLA_SUB
            if lower:
                beta = b[r0 - 1:r0, :] if blk > 0 else 0.0
            else:
                beta = b[r1:r1 + 1, :] if r1 < c else 0.0
            q_blk.append((qs[r0:r1, :] * jnp.exp(b[r0:r1, :] - beta)).astype(BF16))
            k_blk.append((kk * jnp.exp(jnp.minimum(beta - b, EXP_CLAMP))).astype(BF16))
        kend = (kk * jnp.exp(b_end - b)).astype(BF16)
        dec = jnp.exp(b_end)
        v = v_ref[...].astype(BF16)
        for h in range(n_heads):
            sl = slice(h * hd, (h + 1) * hd)
            items.append((d, h, sl, o_ref, mask, qt[:, sl], [(qb[:, sl], kb[:, sl]) for qb, kb in zip(q_blk, k_blk)],
                          kend[:, sl], v[:, sl], dec[:, sl], st_ref[d, h]))
    att = [jnp.concatenate([_dot_nt(qb, kb) for qb, kb in it[6]], axis=0) for it in items]
    inter = [_dot_nt(it[5], it[10]) for it in items]
    upd = [_dot_tn(it[8], it[7]) for it in items]
    intra = [_dot(jnp.where(it[4], a, 0.0), it[8]) for it, a in zip(items, att)]
    for it, oi, os, up in zip(items, intra, inter, upd):
        d, h, sl, o_ref = it[:4]
        o_ref[:, sl] = oi + os
        st_ref[d, h] = it[9] * it[10] + up


def mixer_gla(z, n_batch, n_tok, n_ctx, col0, lb, n_heads, hd):
    width = n_heads * hd
    z3 = z.reshape(n_batch, n_tok, z.shape[1])
    nc, cc = n_tok // CHUNK, n_ctx // CHUNK
    cb0 = col0 // width
    bwd = functools.partial(_bwd_chunk, ctx_chunks=cc, n_chunks=nc)
    blk = (None, CHUNK, width)
    fw = lambda k: pl.BlockSpec(blk, lambda b, j: (b, j, cb0 + k))
    bw = lambda k: pl.BlockSpec(blk, lambda b, j: (b, bwd(j), cb0 + k))
    o_f, o_b = pl.pallas_call(
        functools.partial(_gla_kernel, n_heads=n_heads, hd=hd),
        grid=(n_batch, nc),
        in_specs=[pl.BlockSpec((2, 1, width), lambda b, j: (0, 0, 0)),
                  fw(0), fw(1), fw(3), bw(0), bw(2), bw(3)],
        out_specs=[pl.BlockSpec(blk, lambda b, j: (b, j, 0)),
                   pl.BlockSpec(blk, lambda b, j: (b, bwd(j), 0))],
        out_shape=[jax.ShapeDtypeStruct((n_batch, n_tok, width), F32)] * 2,
        scratch_shapes=[pltpu.VMEM((2, n_heads, hd, hd), F32)],
        compiler_params=_cparams("parallel", "arbitrary"),
        name="gla_scan",
    )(lb.reshape(2, 1, width).astype(F32), z3, z3, z3, z3, z3, z3)
    return o_f.reshape(-1, width), o_b.reshape(-1, width)


def _rotary_tables(n_tok, n_ctx):
    n_lat = n_tok - n_ctx
    rows = n_lat // GRID_W
    row = jnp.repeat(jnp.arange(rows, dtype=F32), GRID_W)
    col = jnp.tile(jnp.arange(GRID_W, dtype=F32), rows)
    n_freq = C_QK // 4
    inv = ROPE_BASE ** (-jnp.arange(n_freq, dtype=F32) / n_freq)
    ang = jnp.concatenate([row[:, None] * inv, col[:, None] * inv], axis=-1)
    ang = jnp.concatenate([jnp.zeros((n_ctx, C_QK // 2), F32), ang], axis=0)
    cos, sin = jnp.cos(ang), jnp.sin(ang)
    return jnp.concatenate([cos, cos], axis=-1), jnp.concatenate([-sin, sin], axis=-1)


def _ret_kernel(cdec_ref, dmat_ref, qdec_ref, kdec_ref,
                qf_ref, kf_ref, vf_ref, cf_ref, sf_ref, qb_ref, kb_ref, vb_ref, cb_ref, sb_ref,
                of_ref, ob_ref, st_ref, *, n_heads):
    @pl.when(pl.program_id(1) == 0)
    def _():
        st_ref[...] = jnp.zeros_like(st_ref)

    half = C_QK // 2
    items = []
    for d, (q_ref, k_ref, v_ref, cos_ref, sin_ref, o_ref) in enumerate(
            ((qf_ref, kf_ref, vf_ref, cf_ref, sf_ref, of_ref), (qb_ref, kb_ref, vb_ref, cb_ref, sb_ref, ob_ref))):
        cos, sin = cos_ref[...], sin_ref[...]
        for h in range(n_heads):
            qs = slice(h * C_QK, (h + 1) * C_QK)
            vs = slice(h * C_V, (h + 1) * C_V)
            qh, kh = q_ref[:, qs].astype(F32), k_ref[:, qs].astype(F32)
            qh = (qh * cos + pltpu.roll(qh, half, axis=1) * sin) * (C_QK ** -0.5)
            kh = kh * cos + pltpu.roll(kh, half, axis=1) * sin
            items.append((d, h, vs, o_ref, qh.astype(BF16), kh.astype(BF16), (qh * qdec_ref[d, h]).astype(BF16),
                          (kh * kdec_ref[d, h]).astype(BF16), v_ref[:, vs].astype(BF16), st_ref[d, h]))
    att = [_dot_nt(it[4], it[5]) for it in items]
    inter = [_dot(it[6], it[9]) for it in items]
    upd = [_dot_tn(it[7], it[8]) for it in items]
    intra = [_dot(a * dmat_ref[it[0], it[1]], it[8]) for it, a in zip(items, att)]
    for it, oi, os, up in zip(items, intra, inter, upd):
        d, h, vs, o_ref = it[:4]
        o_ref[:, vs] = oi + os
        st_ref[d, h] = cdec_ref[d, h] * it[9] + up


def mixer_retention(z, n_batch, n_tok, n_ctx, decay_logit, n_heads):
    qw, vw = n_heads * C_QK, n_heads * C_V
    z3 = z.reshape(n_batch, n_tok, z.shape[1])
    nc, cc = n_tok // CHUNK, n_ctx // CHUNK
    bwd = functools.partial(_bwd_chunk, ctx_chunks=cc, n_chunks=nc)
    log_gamma = jax.nn.log_sigmoid(decay_logit.astype(F32))[:, :, None, None]
    idx = jnp.arange(CHUNK, dtype=F32)
    diff = idx[:, None] - idx[None, :]
    dmat_f = jnp.where(diff >= 0, jnp.exp(jnp.maximum(diff, 0.0) * log_gamma[0]), 0.0)
    dmat_b = jnp.where(diff <= 0, jnp.exp(jnp.maximum(-diff, 0.0) * log_gamma[1]), 0.0)
    dmat = jnp.stack([dmat_f, dmat_b])
    ones = jnp.ones((1, 1, 1, C_QK), F32)
    pos_f, pos_b = idx[None, None, :, None], (CHUNK - 1 - idx)[None, None, :, None]
    lg = log_gamma
    qdec = jnp.concatenate([jnp.exp((pos_f + 1) * lg[0:1]), jnp.exp((pos_b + 1) * lg[1:2])]) * ones
    kdec = jnp.concatenate([jnp.exp((CHUNK - 1 - pos_f) * lg[0:1]), jnp.exp((CHUNK - 1 - pos_b) * lg[1:2])]) * ones
    cdec = jnp.exp(CHUNK * log_gamma[:, :, 0, 0])
    cos2, sin2 = _rotary_tables(n_tok, n_ctx)
    full = lambda shape: pl.BlockSpec(shape, lambda b, j: (0,) * len(shape))
    tab = lambda order: pl.BlockSpec((CHUNK, C_QK), lambda b, j: (order(j), 0))
    ident = lambda j: j
    vcb = 2 * qw // vw
    assert vcb * vw == 2 * qw
    def specs(order):
        return [pl.BlockSpec((None, CHUNK, qw), lambda b, j: (b, order(j), 0)),
                pl.BlockSpec((None, CHUNK, qw), lambda b, j: (b, order(j), 1)),
                pl.BlockSpec((None, CHUNK, vw), lambda b, j: (b, order(j), vcb)),
                tab(order), tab(order)]
    o_f, o_b = pl.pallas_call(
        functools.partial(_ret_kernel, n_heads=n_heads),
        grid=(n_batch, nc),
        in_specs=[pl.BlockSpec(memory_space=pltpu.SMEM), full((2, n_heads, CHUNK, CHUNK)),
                  full((2, n_heads, CHUNK, C_QK)), full((2, n_heads, CHUNK, C_QK))] + specs(ident) + specs(bwd),
        out_specs=[pl.BlockSpec((None, CHUNK, vw), lambda b, j: (b, j, 0)),
                   pl.BlockSpec((None, CHUNK, vw), lambda b, j: (b, bwd(j), 0))],
        out_shape=[jax.ShapeDtypeStruct((n_batch, n_tok, vw), F32)] * 2,
        scratch_shapes=[pltpu.VMEM((2, n_heads, C_QK, C_V), F32)],
        compiler_params=_cparams("parallel", "arbitrary"),
        name="retention_scan",
    )(cdec, dmat, qdec, kdec, z3, z3, z3, cos2, sin2, z3, z3, z3, cos2, sin2)
    return o_f.reshape(-1, vw), o_b.reshape(-1, vw)


def _gdn_conv_kernel(prev_ref, cur_ref, next_ref, w_ref, o_ref, *, tiles_per_row, ctx_tiles, n_heads, hd):
    i = pl.program_id(0)
    r = i % tiles_per_row
    first = jnp.logical_or(r == 0, r == ctx_tiles)
    last = jnp.logical_or(r == ctx_tiles - 1, r == tiles_per_row - 1)
    x = cur_ref[...].astype(F32)
    rows = x.shape[0]
    rid = lax.broadcasted_iota(jnp.int32, x.shape, 0)
    hp = prev_ref.shape[0]
    x_prev = jnp.where(first, 0.0, prev_ref[hp - 1:hp, :].astype(F32))
    x_next = jnp.where(last, 0.0, next_ref[0:1, :].astype(F32))
    left = jnp.where(rid == 0, x_prev, pltpu.roll(x, 1, axis=0))
    right = jnp.where(rid == rows - 1, x_next, pltpu.roll(x, rows - 1, axis=0))
    w = w_ref[...]
    y = _silu(left * w[0:1, :] + x * w[1:2, :] + right * w[2:3, :])
    width = n_heads * hd
    for h in range(3 * n_heads):
        sl = slice(h * hd, (h + 1) * hd)
        yh = y[:, sl]
        if h < 2 * n_heads:
            yh = yh * lax.rsqrt(jnp.sum(yh * yh, axis=-1, keepdims=True) + EPS)
            if h < n_heads:
                yh = yh * (hd ** -0.5)
        o_ref[:, sl] = yh.astype(o_ref.dtype)


def _gdn_kernel(qf_ref, kf_ref, vf_ref, cf_ref, rf_ref, qb_ref, kb_ref, vb_ref, cb_ref, rb_ref,
                of_ref, ob_ref, st_ref, *, n_heads, hd):
    @pl.when(pl.program_id(1) == 0)
    def _():
        st_ref[...] = jnp.zeros_like(st_ref)

    c = qf_ref.shape[0]
    eye = (lax.broadcasted_iota(jnp.int32, (c, c), 0) == lax.broadcasted_iota(jnp.int32, (c, c), 1)).astype(F32)
    items = []
    for d, (q_ref, k_ref, v_ref, col_ref, row_ref, o_ref) in enumerate(
            ((qf_ref, kf_ref, vf_ref, cf_ref, rf_ref, of_ref), (qb_ref, kb_ref, vb_ref, cb_ref, rb_ref, ob_ref))):
        lower = d == 0
        incl = _tri_mask(c, lower)
        strict = jnp.logical_and(incl, eye == 0.0)
        col = col_ref[...]
        row = row_ref[...]
        b_cols = _tri_dot(incl.astype(BF16), col)
        b_rows = _dot_tri(row, _tri_mask(c, not lower).astype(BF16))
        for h in range(n_heads):
            sl = slice(h * hd, (h + 1) * hd)
            ib, ig = d * n_heads + h, (2 + d) * n_heads + h
            be = col[:, ib:ib + 1]
            bc = b_cols[:, ig:ig + 1]
            br = b_rows[ig:ig + 1, :]
            b_end = bc[c - 1:c, :] if lower else bc[0:1, :]
            gam_i = jnp.exp(jnp.where(incl, bc - br, -1e30))
            gam_s = jnp.where(strict, gam_i, 0.0)
            qh, kh, vh = q_ref[:, sl], k_ref[:, sl], v_ref[:, sl]
            eb = jnp.exp(bc)
            rhs = jnp.concatenate([be * vh.astype(F32), (be * eb) * kh.astype(F32)], axis=1).astype(BF16)
            items.append(dict(d=d, h=h, sl=sl, o_ref=o_ref, be=be, gam_i=gam_i, gam_s=gam_s, q=qh, k=kh, rhs=rhs,
                              qe=(qh.astype(F32) * eb).astype(BF16), dec=jnp.exp(b_end),
                              ke=(kh.astype(F32) * jnp.exp(b_end - bc)).astype(BF16), s=st_ref[d, h]))
    kk = [_dot_nt(it["k"], it["k"]) for it in items]
    qk = [_dot_nt(it["q"], it["k"]) for it in items]
    qs = [_dot(it["qe"], it["s"]) for it in items]
    ns = [it["be"] * a * it["gam_s"] for it, a in zip(items, kk)]
    invs = [eye - n for n in ns]
    ps = [_dot(n, n) for n in ns]
    levels = int(np.log2(c)) - 1
    for lvl in range(levels):
        prods = [_dot(inv, p) for inv, p in zip(invs, ps)]
        if lvl < levels - 1:
            ps = [_dot(p, p) for p in ps]
        invs = [inv + pr for inv, pr in zip(invs, prods)]
    sols = [_dot(inv, it["rhs"]) for inv, it in zip(invs, items)]
    sks = [_dot(sol[:, hd:], it["s"]) for sol, it in zip(sols, items)]
    us = [sol[:, :hd] - sk for sol, sk in zip(sols, sks)]
    intra = [_dot(a * it["gam_i"], u) for a, it, u in zip(qk, items, us)]
    upd = [_dot_tn(it["ke"], u) for it, u in zip(items, us)]
    for it, oi, os, up in zip(items, intra, qs, upd):
        it["o_ref"][:, it["sl"]] = oi + os
        st_ref[it["d"], it["h"]] = it["dec"] * it["s"] + up


def mixer_gdn(z, n_batch, n_tok, n_ctx, col0, conv_w, a_log, dt_bias, n_heads, hd):
    t, zw = z.shape
    width = n_heads * hd
    tiles_per_row, ctx_tiles = n_tok // ROW_TILE, n_ctx // ROW_TILE
    cb0 = col0 // (3 * width)
    assert cb0 * 3 * width == col0
    halo = 8
    hpt = ROW_TILE // halo
    n_halo = t // halo
    qkv = pl.pallas_call(
        functools.partial(_gdn_conv_kernel, tiles_per_row=tiles_per_row, ctx_tiles=ctx_tiles, n_heads=n_heads, hd=hd),
        grid=(t // ROW_TILE,),
        in_specs=[pl.BlockSpec((halo, 3 * width), lambda i: (jnp.maximum(i * hpt - 1, 0), cb0)),
                  pl.BlockSpec((ROW_TILE, 3 * width), lambda i: (i, cb0)),
                  pl.BlockSpec((halo, 3 * width), lambda i: (jnp.minimum((i + 1) * hpt, n_halo - 1), cb0)),
                  pl.BlockSpec((3, 3 * width), lambda i: (0, 0))],
        out_specs=pl.BlockSpec((ROW_TILE, 3 * width), lambda i: (i, 0)),
        out_shape=jax.ShapeDtypeStruct((t, 3 * width), BF16),
        compiler_params=_cparams("parallel"),
        name="gdn_conv",
    )(z, z, z, conv_w.astype(F32))

    sc0 = col0 + 4 * width
    small = z[:, sc0:sc0 + 4 * n_heads].astype(F32)
    a_log, dt_bias = a_log.astype(F32), dt_bias.astype(F32)
    be = jax.nn.sigmoid(small[:, :2 * n_heads])
    la_f = -jnp.exp(a_log[0]) * jax.nn.softplus(small[:, 2 * n_heads:3 * n_heads] + dt_bias[0])
    la_b = -jnp.exp(a_log[1]) * jax.nn.softplus(small[:, 3 * n_heads:] + dt_bias[1])
    cols = jnp.concatenate([be, la_f, la_b], axis=1)
    nc, cc = n_tok // CHUNK, n_ctx // CHUNK
    cols3 = cols.reshape(n_batch, n_tok, 4 * n_heads)
    rows4 = jnp.transpose(cols.reshape(n_batch, nc, CHUNK, 4 * n_heads), (0, 1, 3, 2))
    qkv3 = qkv.reshape(n_batch, n_tok, 3 * width)
    bwd = functools.partial(_bwd_chunk, ctx_chunks=cc, n_chunks=nc)
    ident = lambda j: j
    def specs(order):
        return [pl.BlockSpec((None, CHUNK, width), lambda b, j: (b, order(j), 0)),
                pl.BlockSpec((None, CHUNK, width), lambda b, j: (b, order(j), 1)),
                pl.BlockSpec((None, CHUNK, width), lambda b, j: (b, order(j), 2)),
                pl.BlockSpec((None, CHUNK, 4 * n_heads), lambda b, j: (b, order(j), 0)),
                pl.BlockSpec((None, None, 4 * n_heads, CHUNK), lambda b, j: (b, order(j), 0, 0))]
    o_f, o_b = pl.pallas_call(
        functools.partial(_gdn_kernel, n_heads=n_heads, hd=hd),
        grid=(n_batch, nc),
        in_specs=specs(ident) + specs(bwd),
        out_specs=[pl.BlockSpec((None, CHUNK, width), lambda b, j: (b, j, 0)),
                   pl.BlockSpec((None, CHUNK, width), lambda b, j: (b, bwd(j), 0))],
        out_shape=[jax.ShapeDtypeStruct((n_batch, n_tok, width), F32)] * 2,
        scratch_shapes=[pltpu.VMEM((2, n_heads, hd, hd), F32)],
        compiler_params=_cparams("parallel", "arbitrary"),
        name="gdn_scan",
    )(qkv3, qkv3, qkv3, cols3, rows4, qkv3, qkv3, qkv3, cols3, rows4)
    return o_f.reshape(-1, width), o_b.reshape(-1, width)


def _first_max(x, idx, sentinel):
    m = jnp.max(x, axis=0, keepdims=True)
    return m, jnp.min(jnp.where(x == m, idx, sentinel), axis=0, keepdims=True)


def _router_kernel(h_ref, wt_ref, bias_ref, e_ref, g_ref, rank_ref, cnt_ref, carry_ref):
    @pl.when(pl.program_id(0) == 0)
    def _():
        carry_ref[...] = jnp.zeros_like(carry_ref)

    cols = h_ref.shape[0]
    gsz = N_EXPERTS // N_GROUPS
    neg = -jnp.inf
    logits = lax.dot_general(wt_ref[...], h_ref[...], (((1,), (1,)), ((), ())), preferred_element_type=F32)
    scores = jax.nn.sigmoid(logits)
    sel = scores + bias_ref[...]
    i_g = lax.broadcasted_iota(jnp.int32, (gsz, cols), 0)
    blocks, g_scores = [], []
    for g in range(N_GROUPS):
        blk = sel[g * gsz:(g + 1) * gsz, :]
        m1, first = _first_max(blk, i_g, gsz)
        m2 = jnp.max(jnp.where(i_g == first, neg, blk), axis=0, keepdims=True)
        blocks.append(blk)
        g_scores.append(m1 + m2)
    keep = [jnp.zeros((1, cols), jnp.bool_) for _ in range(N_GROUPS)]
    for _ in range(TOPK_GROUPS):
        best = functools.reduce(jnp.maximum, g_scores)
        found = jnp.zeros((1, cols), jnp.bool_)
        for g in range(N_GROUPS):
            pick = jnp.logical_and(g_scores[g] == best, jnp.logical_not(found))
            found = jnp.logical_or(found, pick)
            keep[g] = jnp.logical_or(keep[g], pick)
            g_scores[g] = jnp.where(pick, neg, g_scores[g])
    cur = jnp.concatenate([jnp.where(keep[g], blocks[g], neg) for g in range(N_GROUPS)], axis=0)
    i_e = lax.broadcasted_iota(jnp.int32, (N_EXPERTS, cols), 0)
    picks, gates = [], []
    for _ in range(TOP_K):
        _, idx = _first_max(cur, i_e, N_EXPERTS)
        pick = i_e == idx
        picks.append((idx, pick))
        gates.append(jnp.sum(jnp.where(pick, scores, 0.0), axis=0, keepdims=True))
        cur = jnp.where(pick, neg, cur)
    total = functools.reduce(jnp.add, gates)
    chosen = functools.reduce(jnp.logical_or, [p for _, p in picks]).astype(F32)
    r = lax.broadcasted_iota(jnp.int32, (cols, cols), 0)
    c = lax.broadcasted_iota(jnp.int32, (cols, cols), 1)
    before = jnp.dot(chosen.astype(BF16), (r < c).astype(BF16), preferred_element_type=F32) + carry_ref[...]
    for k, (idx, pick) in enumerate(picks):
        e_ref[k:k + 1, :] = idx
        g_ref[k:k + 1, :] = gates[k] / total * ROUTED_SCALE
        rank_ref[k:k + 1, :] = jnp.sum(jnp.where(pick, before, 0.0), axis=0, keepdims=True).astype(jnp.int32)
    carry_ref[...] += jnp.sum(chosen, axis=1, keepdims=True)
    cnt_ref[...] = carry_ref[...]


def _slot_fetch(slots_hbm, idx_smem, sem_idx, tile, buf):
    return pltpu.make_async_copy(slots_hbm.at[tile], idx_smem.at[buf], sem_idx.at[buf])


def _expert_kernel(be_ref, nu_ref, tok_hbm, hp_hbm, wg_ref, wu_ref, wd_ref, o_ref,
                   wg_s, wu_s, wd_s, idx_smem, x_buf, sem_idx, sem_rows):
    i = pl.program_id(0)
    n_live = nu_ref[0]
    live = i < n_live
    new_expert = jnp.logical_or(i == 0, be_ref[i] != be_ref[jnp.maximum(i - 1, 0)])
    rows = x_buf.shape[1]
    idx_rows = idx_smem.shape[1]

    def idx_fetch(blk):
        return pltpu.make_async_copy(tok_hbm.at[blk], idx_smem.at[blk % 2], sem_idx.at[blk % 2])

    def gather_start(blk):
        b = blk % 2
        for q in range(idx_rows):
            def body(r, carry):
                pltpu.make_async_copy(hp_hbm.at[pl.ds(idx_smem[b, q, r], 1)],
                                      x_buf.at[b, pl.ds(q * LANES + r, 1)], sem_rows.at[b]).start(priority=q % 2)
                return carry
            lax.fori_loop(0, LANES, body, 0, unroll=8)

    @pl.when(i == 0)
    def _():
        idx_fetch(0).start()
        idx_fetch(0).wait()
        gather_start(0)

        @pl.when(n_live > 1)
        def _():
            idx_fetch(1).start()

    @pl.when(i + 1 < n_live)
    def _():
        idx_fetch(i + 1).wait()
        gather_start(i + 1)

        @pl.when(i + 2 < n_live)
        def _():
            idx_fetch(i + 2).start()

    @pl.when(jnp.logical_and(live, new_expert))
    def _():
        wg_s[...] = wg_ref[...].astype(BF16)
        wu_s[...] = wu_ref[...].astype(BF16)
        wd_s[...] = wd_ref[...].astype(BF16)

    @pl.when(live)
    def _():
        b = i % 2
        pltpu.make_async_copy(hp_hbm.at[pl.ds(0, rows)], x_buf.at[b], sem_rows.at[b]).wait()
        lo, hi = _unpack_halves(_load_rows(x_buf.at[b], 0, rows))
        lo, hi = lo.astype(BF16), hi.astype(BF16)
        half = lo.shape[1]
        g = (jnp.dot(lo, wg_s[:half, :], preferred_element_type=F32)
             + jnp.dot(hi, wg_s[half:, :], preferred_element_type=F32))
        u = (jnp.dot(lo, wu_s[:half, :], preferred_element_type=F32)
             + jnp.dot(hi, wu_s[half:, :], preferred_element_type=F32))
        a = (_silu(g) * u).astype(BF16)
        _store_rows(o_ref, _pack_halves(jnp.dot(a, wd_s[...], preferred_element_type=F32)))

    @pl.when(jnp.logical_not(live))
    def _():
        o_ref[...] = jnp.zeros_like(o_ref)


def _combine_kernel(slots_hbm, y_hbm, h_ref, gate_ref, xs_ref, mg_ref, wg_ref, wu_ref, wd_ref, o_ref,
                    idx_smem, rows_buf, sem_idx, sem_rows):
    i = pl.program_id(0)
    rows = h_ref.shape[0]

    @pl.when(i == 0)
    def _():
        _slot_fetch(slots_hbm, idx_smem, sem_idx, 0, 0).start()

    buf = i % 2
    _slot_fetch(slots_hbm, idx_smem, sem_idx, i, buf).wait()

    @pl.when(i + 1 < pl.num_programs(0))
    def _():
        _slot_fetch(slots_hbm, idx_smem, sem_idx, i + 1, 1 - buf).start()

    def body(r, carry):
        for k in range(TOP_K):
            pltpu.make_async_copy(y_hbm.at[pl.ds(idx_smem[buf, k, r], 1)], rows_buf.at[pl.ds(k * rows + r, 1)],
                                  sem_rows).start(priority=k % 2)
        return carry

    lax.fori_loop(0, rows, body, 0, unroll=8)

    h = h_ref[...]
    g = jnp.dot(h, wg_ref[...], preferred_element_type=F32)
    u = jnp.dot(h, wu_ref[...], preferred_element_type=F32)
    a = (_silu(g) * u).astype(BF16)
    y = jnp.dot(a, wd_ref[...], preferred_element_type=F32)

    pltpu.make_async_copy(y_hbm.at[pl.ds(0, rows * TOP_K)], rows_buf, sem_rows).wait()
    half = y.shape[1] // 2
    acc_lo, acc_hi = y[:, :half], y[:, half:]
    gate = gate_ref[...]
    for k in range(TOP_K):
        lo, hi = _unpack_halves(_load_rows(rows_buf, k * rows, rows))
        acc_lo = acc_lo + gate[:, k:k + 1] * lo
        acc_hi = acc_hi + gate[:, k:k + 1] * hi
    mg = mg_ref[...]
    o_ref[:, :half] = xs_ref[:, :half] + mg[:, :half] * acc_lo
    o_ref[:, half:] = xs_ref[:, half:] + mg[:, half:] * acc_hi


def route(h2, w_router, router_bias):
    t, d = h2.shape
    n_tiles = t // ROW_TILE
    n_exp = w_router.shape[1]
    return pl.pallas_call(
        _router_kernel,
        grid=(n_tiles,),
        in_specs=[pl.BlockSpec((ROW_TILE, d), lambda i: (i, 0)),
                  pl.BlockSpec((n_exp, d), lambda i: (0, 0)),
                  pl.BlockSpec((n_exp, 1), lambda i: (0, 0))],
        out_specs=[pl.BlockSpec((TOP_K, ROW_TILE), lambda i: (0, i)),
                   pl.BlockSpec((TOP_K, ROW_TILE), lambda i: (0, i)),
                   pl.BlockSpec((TOP_K, ROW_TILE), lambda i: (0, i)),
                   pl.BlockSpec((n_exp, 1), lambda i: (0, 0))],
        out_shape=[jax.ShapeDtypeStruct((TOP_K, t), jnp.int32), jax.ShapeDtypeStruct((TOP_K, t), F32),
                   jax.ShapeDtypeStruct((TOP_K, t), jnp.int32), jax.ShapeDtypeStruct((n_exp, 1), F32)],
        scratch_shapes=[pltpu.VMEM((n_exp, 1), F32)],
        compiler_params=_cparams("arbitrary"),
        name="router_topk",
    )(h2, jnp.transpose(w_router).astype(BF16), router_bias.astype(F32).reshape(n_exp, 1))


def moe_block(h2, h2_packed, xs, mod, k_gate, tiles_per_row, n_batch,
              w_router, router_bias, w_gate, w_up, w_down, ws_gate, ws_up, ws_down):
    t, d = h2.shape
    row_shape = h2_packed.shape[1:]
    de = ws_gate.shape[1]
    n_tiles = t // ROW_TILE
    eidx, gate_t, rank, cnt = route(h2, w_router, router_bias)
    n_blocks = -(-t * TOP_K // MOE_BLOCK) + N_EXPERTS
    counts = cnt[:, 0].astype(jnp.int32)
    padded = (counts + MOE_BLOCK - 1) // MOE_BLOCK * MOE_BLOCK
    pad_end = jnp.cumsum(padded).astype(jnp.int32)
    pad_start = pad_end - padded
    e_ids = jnp.arange(N_EXPERTS, dtype=jnp.int32)
    slot = rank + jnp.sum(jnp.where(eidx[:, :, None] == e_ids, pad_start, 0), axis=-1)
    slots = jnp.transpose(slot.reshape(TOP_K, n_tiles, ROW_TILE), (1, 0, 2))
    gate = jnp.transpose(gate_t)
    block_start = jnp.arange(n_blocks, dtype=jnp.int32) * MOE_BLOCK
    block_e = jnp.minimum(jnp.sum(pad_end[None, :] <= block_start[:, None], axis=1), N_EXPERTS - 1).astype(jnp.int32)
    n_used = (pad_end[-1] // MOE_BLOCK).astype(jnp.int32).reshape(1)

    idx_scratch = [pltpu.SMEM((2, TOP_K, ROW_TILE), jnp.int32)]
    token_ids = jnp.broadcast_to(jnp.arange(t, dtype=jnp.int32), (TOP_K, t))
    slot_tok = jnp.zeros((n_blocks * MOE_BLOCK,), jnp.int32).at[slot.reshape(-1)].set(
        token_ids.reshape(-1), unique_indices=True)
    slot_tok = slot_tok.reshape(n_blocks, MOE_BLOCK // LANES, LANES)

    y_sorted = pl.pallas_call(
        _expert_kernel,
        grid_spec=pltpu.PrefetchScalarGridSpec(
            num_scalar_prefetch=2,
            grid=(n_blocks,),
            in_specs=[pl.BlockSpec(memory_space=pl.ANY),
                      pl.BlockSpec(memory_space=pl.ANY),
                      pl.BlockSpec((None, d, de), lambda i, be, nu: (be[i], 0, 0)),
                      pl.BlockSpec((None, d, de), lambda i, be, nu: (be[i], 0, 0)),
                      pl.BlockSpec((None, de, d), lambda i, be, nu: (be[i], 0, 0))],
            out_specs=pl.BlockSpec((MOE_BLOCK,) + row_shape, lambda i, be, nu: (i, 0, 0)),
            scratch_shapes=[pltpu.VMEM((d, de), BF16), pltpu.VMEM((d, de), BF16), pltpu.VMEM((de, d), BF16),
                            pltpu.SMEM((2, MOE_BLOCK // LANES, LANES), jnp.int32),
                            pltpu.VMEM((2, MOE_BLOCK) + row_shape, jnp.uint32),
                            pltpu.SemaphoreType.DMA((2,)), pltpu.SemaphoreType.DMA((2,))],
        ),
        out_shape=jax.ShapeDtypeStruct((n_blocks * MOE_BLOCK,) + row_shape, jnp.uint32),
        compiler_params=_cparams("arbitrary"),
        name="routed_experts",
    )(block_e, n_used, slot_tok, h2_packed, w_gate, w_up, w_down)

    seg = functools.partial(_seg_of_tile, tiles_per_row=tiles_per_row, n_batch=n_batch)
    return pl.pallas_call(
        _combine_kernel,
        grid=(n_tiles,),
        in_specs=[pl.BlockSpec(memory_space=pl.ANY),
                  pl.BlockSpec(memory_space=pl.ANY),
                  pl.BlockSpec((ROW_TILE, d), lambda i: (i, 0)),
                  pl.BlockSpec((ROW_TILE, TOP_K), lambda i: (i, 0)),
                  pl.BlockSpec((ROW_TILE, d), lambda i: (i, 0)),
                  pl.BlockSpec((None, None, 1, d), lambda i: (seg(i), k_gate, 0, 0)),
                  pl.BlockSpec((d, de), lambda i: (0, 0)),
                  pl.BlockSpec((d, de), lambda i: (0, 0)),
                  pl.BlockSpec((de, d), lambda i: (0, 0))],
        out_specs=pl.BlockSpec((ROW_TILE, d), lambda i: (i, 0)),
        out_shape=jax.ShapeDtypeStruct((t, d), F32),
        scratch_shapes=idx_scratch + [pltpu.VMEM((TOP_K * ROW_TILE,) + row_shape, jnp.uint32),
                                      pltpu.SemaphoreType.DMA((2,)), pltpu.SemaphoreType.DMA(())],
        compiler_params=_cparams("arbitrary"),
        name="moe_combine",
    )(slots, y_sorted, h2, gate, xs, mod, ws_gate.astype(BF16), ws_up.astype(BF16), ws_down.astype(BF16))


def kernel(x, c, ctx, c_ctx, w_ada, b_ada, norm1_g, norm2_g, w_in_ab, s5_lambda_re, s5_lambda_im, s5_log_dt, s5_b_re, s5_b_im, s5_c_re, s5_c_im, s5_d, s5_w_glu, hgrn_lb, w_in_cd, ret_decay_logit, gdn_conv_w, gdn_a_log, gdn_dt_bias, gdn_norm_g, w_out, w_router, router_bias, w_exp_gate, w_exp_up, w_exp_down, w_sh_gate, w_sh_up, w_sh_down, final_norm_g):
    n_batch, n_lat, d = x.shape
    n_ctx = ctx.shape[1]
    n_tok = n_ctx + n_lat
    depth = w_ada.shape[0]
    assert n_ctx % ROW_TILE == 0 and n_lat % ROW_TILE == 0
    tiles_per_row, ctx_tiles = n_tok // ROW_TILE, n_ctx // ROW_TILE
    assert ctx_tiles == 1
    t = n_batch * n_tok

    xs = jnp.concatenate([ctx, x], axis=1).reshape(t, d)
    lb_all = jnp.cumsum(jax.nn.softmax(hgrn_lb.astype(F32), axis=0), axis=0)
    cond = jnp.concatenate([c, c_ctx[None]], axis=0)
    cond = jnp.pad(jax.nn.silu(cond), ((0, 8 - (n_batch + 1)), (0, 0)))
    ones_g = jnp.ones((D_HEAD,), F32)

    for i in range(depth):
        j = i // 2
        mod = matmul(cond, w_ada[i], F32, 8, 1024, "ada_mod")[:n_batch + 1] + b_ada[i]
        mod = mod.reshape(n_batch + 1, 6, 1, d)
        h = norm_mod(xs, norm1_g[i], mod, 0, 1, tiles_per_row, n_batch)
        if i % 2 == 0:
            z = matmul(h, w_in_ab[j].astype(BF16), F32, 512, 1024, "in_proj")
            a_width = s5_d.shape[1]
            b_heads = (z.shape[1] - a_width) // 5 // B_HEAD
            y_a = mixer_s5(z, n_batch, n_tok, n_ctx, s5_lambda_re[j], s5_lambda_im[j], s5_log_dt[j],
                           s5_b_re[j], s5_b_im[j], s5_c_re[j], s5_c_im[j], s5_d[j], s5_w_glu[j])
            o_f, o_b = mixer_gla(z, n_batch, n_tok, n_ctx, a_width, lb_all[j], b_heads, B_HEAD)
            gcb = (a_width + 4 * b_heads * B_HEAD) // (b_heads * B_HEAD)
            y_b = post_norm_gate(o_f, o_b, z, gcb, B_HEAD, jnp.ones((B_HEAD,), F32))
            m1, m2 = y_a, y_b
        else:
            cd_cols = w_in_cd.shape[2]
            cd_pad = -(-cd_cols // 1024) * 1024
            w_cd = jnp.pad(w_in_cd[j].astype(BF16), ((0, 0), (0, cd_pad - cd_cols)))
            z = matmul(h, w_cd, F32, 512, 1024, "in_proj")
            c_heads = ret_decay_logit.shape[2]
            d_heads = gdn_a_log.shape[2]
            o_f, o_b = mixer_retention(z, n_batch, n_tok, n_ctx, ret_decay_logit[j], c_heads)
            y_c = post_norm_gate(o_f, o_b, z, (2 * c_heads * C_QK + c_heads * C_V) // (c_heads * C_V), C_V,
                                 jnp.ones((C_V,), F32))
            col0 = 2 * c_heads * C_QK + 2 * c_heads * C_V
            o_f, o_b = mixer_gdn(z, n_batch, n_tok, n_ctx, col0, gdn_conv_w[j], gdn_a_log[j], gdn_dt_bias[j],
                                 d_heads, D_HEAD)
            y_d = post_norm_gate(o_f, o_b, z, (col0 + 3 * d_heads * D_HEAD) // (d_heads * D_HEAD), D_HEAD,
                                 gdn_norm_g[j])
            m1, m2 = y_c, y_d
        xs = out_proj(m1, m2, w_out[i].astype(BF16), xs, mod, 2, tiles_per_row, n_batch)
        h2, h2_packed = norm_mod(xs, norm2_g[i], mod, 3, 4, tiles_per_row, n_batch, packed=True)
        xs = moe_block(h2, h2_packed, xs, mod, 5, tiles_per_row, n_batch, w_router[i], router_bias[i],
                       w_exp_gate[i], w_exp_up[i], w_exp_down[i], w_sh_gate[i], w_sh_up[i], w_sh_down[i])
    out = final_norm(xs, final_norm_g, n_batch, tiles_per_row, ctx_tiles)
    return out.reshape(n_batch, n_lat, d)
```

```python
import functools

import numpy as np
import jax
import jax.numpy as jnp
from jax import lax
from jax.experimental import pallas as pl
from jax.experimental.pallas import tpu as pltpu

F32 = jnp.float32
BF16 = jnp.bfloat16

EPS = 1e-6
GRID_W = 64
CHUNK = 64
S5_L = 32
S5_H = 16
S5_P = 64
B_HEAD = 128
C_QK = 128
C_V = 256
D_HEAD = 128
ROPE_BASE = 10000.0
N_EXPERTS = 64
N_GROUPS = 8
TOPK_GROUPS = 4
TOP_K = 8
ROUTED_SCALE = 2.5
MOE_BLOCK = 256
ROW_TILE = 256
VMEM_LIMIT_V7X = 56 * 1024 * 1024
EXP_CLAMP = 80.0
GLA_SUB = 16


def _cparams(*sem):
    return pltpu.CompilerParams(dimension_semantics=sem, vmem_limit_bytes=VMEM_LIMIT_V7X)


def _silu(x):
    return x * jax.nn.sigmoid(x)


def _dot(a, b):
    return jnp.dot(a.astype(BF16), b.astype(BF16), preferred_element_type=F32)


def _dot_nt(a, b):
    return lax.dot_general(a.astype(BF16), b.astype(BF16), (((1,), (1,)), ((), ())), preferred_element_type=F32)


def _dot_tn(a, b):
    return lax.dot_general(a.astype(BF16), b.astype(BF16), (((0,), (0,)), ((), ())), preferred_element_type=F32)


def _split3(x):
    x1 = x.astype(BF16)
    r = x - x1.astype(F32)
    x2 = r.astype(BF16)
    x3 = (r - x2.astype(F32)).astype(BF16)
    return x1, x2, x3


def _tri_dot(tri, x):
    return sum(jnp.dot(tri, p, preferred_element_type=F32) for p in _split3(x))


def _dot_tri(x, tri):
    return sum(jnp.dot(p, tri, preferred_element_type=F32) for p in _split3(x))


def _tri_mask(n, lower):
    r = lax.broadcasted_iota(jnp.int32, (n, n), 0)
    c = lax.broadcasted_iota(jnp.int32, (n, n), 1)
    return (r >= c) if lower else (r <= c)


def _seg_of_tile(i, tiles_per_row, n_batch):
    return jnp.where(i % tiles_per_row == 0, n_batch, i // tiles_per_row)


def _pack_halves(y):
    half = y.shape[1] // 2
    lo = lax.bitcast_convert_type(y[:, :half].astype(BF16).astype(F32), jnp.uint32) >> 16
    hi = lax.bitcast_convert_type(y[:, half:].astype(BF16).astype(F32), jnp.uint32) & jnp.uint32(0xFFFF0000)
    return hi | lo


def _unpack_halves(w):
    lo = lax.bitcast_convert_type(w << 16, F32)
    hi = lax.bitcast_convert_type(w & jnp.uint32(0xFFFF0000), F32)
    return lo, hi


LANES = 128


def _store_rows(ref, packed):
    pieces = jnp.stack([packed[:, s * LANES:(s + 1) * LANES] for s in range(ref.shape[1])], axis=0)
    ref[...] = pltpu.einshape("stl->tsl", pieces)


def _load_rows(ref, row0, rows):
    x = pltpu.einshape("tsl->stl", ref[row0:row0 + rows])
    return jnp.concatenate([x[s] for s in range(ref.shape[1])], axis=1)


def _norm_mod_kernel(x_ref, g_ref, shift_ref, scale_ref, o_ref, *packed_ref):
    x = x_ref[...]
    y = x * lax.rsqrt(jnp.mean(x * x, axis=-1, keepdims=True) + EPS) * g_ref[...]
    y = y * (1.0 + scale_ref[...]) + shift_ref[...]
    o_ref[...] = y.astype(o_ref.dtype)
    if packed_ref:
        _store_rows(packed_ref[0], _pack_halves(y))


def norm_mod(xs, g, mod, k_shift, k_scale, tiles_per_row, n_batch, packed=False):
    t, d = xs.shape
    seg = functools.partial(_seg_of_tile, tiles_per_row=tiles_per_row, n_batch=n_batch)
    out_specs = [pl.BlockSpec((ROW_TILE, d), lambda i: (i, 0))]
    out_shape = [jax.ShapeDtypeStruct((t, d), BF16)]
    if packed:
        out_specs.append(pl.BlockSpec((ROW_TILE, d // 2 // LANES, LANES), lambda i: (i, 0, 0)))
        out_shape.append(jax.ShapeDtypeStruct((t, d // 2 // LANES, LANES), jnp.uint32))
    res = pl.pallas_call(
        _norm_mod_kernel,
        grid=(t // ROW_TILE,),
        in_specs=[
            pl.BlockSpec((ROW_TILE, d), lambda i: (i, 0)),
            pl.BlockSpec((1, d), lambda i: (0, 0)),
            pl.BlockSpec((None, None, 1, d), lambda i: (seg(i), k_shift, 0, 0)),
            pl.BlockSpec((None, None, 1, d), lambda i: (seg(i), k_scale, 0, 0)),
        ],
        out_specs=out_specs,
        out_shape=out_shape,
        compiler_params=_cparams("parallel"),
        name="norm_mod",
    )(xs, g.reshape(1, d), mod, mod)
    return res if packed else res[0]


def _final_norm_kernel(x_ref, g_ref, o_ref):
    x = x_ref[...]
    o_ref[...] = x * lax.rsqrt(jnp.mean(x * x, axis=-1, keepdims=True) + EPS) * g_ref[...]


def final_norm(xs, g, n_batch, tiles_per_row, ctx_tiles):
    t, d = xs.shape
    lat_tiles = tiles_per_row - ctx_tiles
    return pl.pallas_call(
        _final_norm_kernel,
        grid=(n_batch, lat_tiles),
        in_specs=[
            pl.BlockSpec((ROW_TILE, d), lambda b, i: (b * tiles_per_row + ctx_tiles + i, 0)),
            pl.BlockSpec((1, d), lambda b, i: (0, 0)),
        ],
        out_specs=pl.BlockSpec((ROW_TILE, d), lambda b, i: (b * lat_tiles + i, 0)),
        out_shape=jax.ShapeDtypeStruct((n_batch * lat_tiles * ROW_TILE, d), F32),
        compiler_params=_cparams("parallel", "parallel"),
        name="final_norm",
    )(xs, g.reshape(1, d))


def _mm_kernel(a_ref, w_ref, o_ref):
    o_ref[...] = jnp.dot(a_ref[...].astype(BF16), w_ref[...].astype(BF16),
                         preferred_element_type=F32).astype(o_ref.dtype)


def matmul(a, w, out_dtype, tm, tn, name):
    m, k = a.shape
    n = w.shape[1]
    return pl.pallas_call(
        _mm_kernel,
        grid=(n // tn, m // tm),
        in_specs=[pl.BlockSpec((tm, k), lambda j, i: (i, 0)),
                  pl.BlockSpec((k, tn), lambda j, i: (0, j))],
        out_specs=pl.BlockSpec((tm, tn), lambda j, i: (i, j)),
        out_shape=jax.ShapeDtypeStruct((m, n), out_dtype),
        compiler_params=_cparams("parallel", "parallel"),
        name=name,
    )(a, w)


def _out_proj_kernel(a1_ref, a2_ref, w1_ref, w2_ref, res_ref, gate_ref, o_ref):
    y = jnp.dot(a1_ref[...], w1_ref[...], preferred_element_type=F32)
    y += jnp.dot(a2_ref[...], w2_ref[...], preferred_element_type=F32)
    o_ref[...] = res_ref[...] + gate_ref[...] * y


def out_proj(a1, a2, w, xs, mod, k_gate, tiles_per_row, n_batch, tn=1024):
    t, d = xs.shape
    k1, k2 = a1.shape[1], a2.shape[1]
    assert k1 == k2
    seg = functools.partial(_seg_of_tile, tiles_per_row=tiles_per_row, n_batch=n_batch)
    return pl.pallas_call(
        _out_proj_kernel,
        grid=(d // tn, t // ROW_TILE),
        in_specs=[
            pl.BlockSpec((ROW_TILE, k1), lambda j, i: (i, 0)),
            pl.BlockSpec((ROW_TILE, k2), lambda j, i: (i, 0)),
            pl.BlockSpec((k1, tn), lambda j, i: (0, j)),
            pl.BlockSpec((k2, tn), lambda j, i: (1, j)),
            pl.BlockSpec((ROW_TILE, tn), lambda j, i: (i, j)),
            pl.BlockSpec((None, None, 1, tn), lambda j, i: (seg(i), k_gate, 0, j)),
        ],
        out_specs=pl.BlockSpec((ROW_TILE, tn), lambda j, i: (i, j)),
        out_shape=jax.ShapeDtypeStruct((t, d), F32),
        compiler_params=_cparams("parallel", "parallel"),
        name="out_proj",
    )(a1, a2, w, w, xs, mod)


def _post_kernel(of_ref, ob_ref, gate_ref, ng_ref, o_ref, *, head_dim):
    o = of_ref[...].astype(F32) + ob_ref[...].astype(F32)
    g = gate_ref[...].astype(F32)
    width = o.shape[1]
    for h in range(width // head_dim):
        sl = slice(h * head_dim, (h + 1) * head_dim)
        oh = o[:, sl]
        y = oh * lax.rsqrt(jnp.mean(oh * oh, axis=-1, keepdims=True) + EPS) * ng_ref[...]
        o_ref[:, sl] = (y * _silu(g[:, sl])).astype(o_ref.dtype)


def post_norm_gate(o_f, o_b, z, gate_col_block, head_dim, norm_g):
    t, width = o_f.shape
    return pl.pallas_call(
        functools.partial(_post_kernel, head_dim=head_dim),
        grid=(t // ROW_TILE,),
        in_specs=[
            pl.BlockSpec((ROW_TILE, width), lambda i: (i, 0)),
            pl.BlockSpec((ROW_TILE, width), lambda i: (i, 0)),
            pl.BlockSpec((ROW_TILE, width), lambda i: (i, gate_col_block)),
            pl.BlockSpec((1, head_dim), lambda i: (0, 0)),
        ],
        out_specs=pl.BlockSpec((ROW_TILE, width), lambda i: (i, 0)),
        out_shape=jax.ShapeDtypeStruct((t, width), BF16),
        compiler_params=_cparams("parallel"),
        name="post_norm_gate",
    )(o_f, o_b, z, norm_g.reshape(1, head_dim).astype(F32))


def _bwd_chunk(j, ctx_chunks, n_chunks):
    return jnp.where(j < ctx_chunks, ctx_chunks - 1 - j, n_chunks + ctx_chunks - 1 - j)


def _s5_direction_tables(lam_re, lam_im, log_dt, b_re, b_im, c_re, c_im, reverse):
    hi = lax.Precision.HIGHEST
    ln = S5_L
    lam_re, lam_im, b_re, b_im, c_re, c_im = (p.astype(F32) for p in (lam_re, lam_im, b_re, b_im, c_re, c_im))
    dt = jnp.exp(log_dt.astype(F32))[:, None]
    mag = jnp.exp(lam_re * dt)
    ab_re, ab_im = mag * jnp.cos(lam_im * dt), mag * jnp.sin(lam_im * dt)
    den = lam_re * lam_re + lam_im * lam_im
    fr = ((ab_re - 1) * lam_re + ab_im * lam_im) / den
    fi = (ab_im * lam_re - (ab_re - 1) * lam_im) / den
    bb_re = fr[..., None] * b_re - fi[..., None] * b_im
    bb_im = fr[..., None] * b_im + fi[..., None] * b_re
    tau = jnp.arange(ln + 1, dtype=F32)[:, None, None]
    pw = jnp.exp(tau * (lam_re * dt))
    pr, pi = pw * jnp.cos(tau * (lam_im * dt)), pw * jnp.sin(tau * (lam_im * dt))
    abr = pr[..., None] * bb_re - pi[..., None] * bb_im
    abi = pr[..., None] * bb_im + pi[..., None] * bb_re
    kern = (jnp.einsum('ghp,tgpk->tghk', c_re, abr[:ln], precision=hi)
            - jnp.einsum('ghp,tgpk->tghk', c_im, abi[:ln], precision=hi))
    pos = jnp.arange(ln)
    lag = (pos[None, :] - pos[:, None]) if not reverse else (pos[:, None] - pos[None, :])
    toe = jnp.where((lag >= 0)[:, :, None, None, None], kern[jnp.clip(lag, 0, ln - 1)], 0.0)
    g = lam_re.shape[0]
    intra = jnp.transpose(toe, (2, 0, 4, 1, 3)).reshape(g, ln * S5_H, ln * S5_H)
    pw_in = (ln - 1 - pos) if not reverse else pos
    inj = jnp.concatenate([abr[pw_in], abi[pw_in]], axis=2)
    inject = jnp.transpose(inj, (1, 0, 3, 2)).reshape(g, ln * S5_H, 2 * S5_P)
    pw_out = (pos + 1) if not reverse else (ln - pos)
    w_re = c_re[None] * pr[pw_out][:, :, None, :] - c_im[None] * pi[pw_out][:, :, None, :]
    w_im = -(c_re[None] * pi[pw_out][:, :, None, :] + c_im[None] * pr[pw_out][:, :, None, :])
    readout = jnp.transpose(jnp.concatenate([w_re, w_im], axis=3), (1, 3, 0, 2)).reshape(g, 2 * S5_P, ln * S5_H)
    decay = jnp.stack([pr[ln], pi[ln]])
    return intra, inject, readout, decay


def _s5_in_kernel(u_ref, w_ref, yi_ref, s_ref):
    r = jnp.dot(u_ref[...], w_ref[...], preferred_element_type=F32)
    n_intra = yi_ref.shape[-1]
    yi_ref[...] = r[:, :n_intra]
    s_ref[...] = r[:, n_intra:]


def _s5_scan_kernel(a_ref, sf_ref, sb_ref, of_ref, ob_ref, st_ref):
    @pl.when(pl.program_id(1) == 0)
    def _():
        st_ref[...] = jnp.zeros_like(st_ref)

    steps = sf_ref.shape[0]
    for d, (s_ref, o_ref) in enumerate(((sf_ref, of_ref), (sb_ref, ob_ref))):
        ar, ai = a_ref[2 * d], a_ref[2 * d + 1]
        sr, si = st_ref[2 * d], st_ref[2 * d + 1]
        for q in range(steps):
            r = q if d == 0 else steps - 1 - q
            o_ref[r, 0] = sr
            o_ref[r, 1] = si
            sr, si = (ar * sr - ai * si + s_ref[r, 0], ar * si + ai * sr + s_ref[r, 1])
        st_ref[2 * d] = sr
        st_ref[2 * d + 1] = si


def _s5_out_kernel(yi_ref, st_ref, w_ref, u_ref, d_ref, o_ref):
    y = yi_ref[...] + jnp.dot(st_ref[...].astype(BF16), w_ref[...], preferred_element_type=F32)
    o_ref[...] = (y + d_ref[...] * u_ref[...].astype(F32)).astype(o_ref.dtype)


def _glu_kernel(y_ref, w_ref, o_ref):
    y = jax.nn.gelu(y_ref[...].astype(F32))
    o_ref[...] = (y * jax.nn.sigmoid(jnp.dot(y.astype(BF16), w_ref[...], preferred_element_type=F32))
                  ).astype(o_ref.dtype)


def mixer_s5(z, n_batch, n_tok, n_ctx, lam_re, lam_im, log_dt, b_re, b_im, c_re, c_im, d_skip, w_glu):
    t = z.shape[0]
    g = lam_re.shape[1]
    width = g * S5_H
    lh = S5_L * S5_H
    rows = t // S5_L
    tabs = [_s5_direction_tables(lam_re[d], lam_im[d], log_dt[d], b_re[d], b_im[d], c_re[d], c_im[d], d == 1)
            for d in range(2)]
    w_in = jnp.concatenate([tabs[0][0] + tabs[1][0], tabs[0][1], tabs[1][1]], axis=2).astype(BF16)
    w_st = jnp.concatenate([tabs[0][2], tabs[1][2]], axis=1).astype(BF16)
    decay = jnp.concatenate([tabs[0][3], tabs[1][3]], axis=0).reshape(4, g * S5_P // 128, 128)
    u = z[:, :width].reshape(rows, S5_L, g, S5_H)
    u_g = jnp.transpose(u, (2, 0, 1, 3)).reshape(g, rows, lh).astype(BF16)

    n_st = 4 * S5_P
    yi, s_in = pl.pallas_call(
        _s5_in_kernel,
        grid=(g,),
        in_specs=[pl.BlockSpec((None, rows, lh), lambda i: (i, 0, 0)),
                  pl.BlockSpec((None, lh, lh + n_st), lambda i: (i, 0, 0))],
        out_specs=[pl.BlockSpec((None, rows, lh), lambda i: (i, 0, 0)),
                   pl.BlockSpec((None, rows, n_st), lambda i: (i, 0, 0))],
        out_shape=[jax.ShapeDtypeStruct((g, rows, lh), F32), jax.ShapeDtypeStruct((g, rows, n_st), F32)],
        compiler_params=_cparams("parallel"),
        name="s5_in",
    )(u_g, w_in)

    gp = g * S5_P // 128
    s_scan = jnp.transpose(s_in.reshape(g, rows, 4, S5_P), (1, 2, 0, 3)).reshape(rows, 4, gp, 128)
    blocks_per_row = n_tok // S5_L
    cb = n_ctx // S5_L
    assert blocks_per_row % cb == 0
    nblk = blocks_per_row // cb
    bwd = lambda j: jnp.where(j == 0, 0, nblk - j)
    st_f, st_b = pl.pallas_call(
        _s5_scan_kernel,
        grid=(n_batch, nblk),
        in_specs=[pl.BlockSpec((4, gp, 128), lambda b, j: (0, 0, 0)),
                  pl.BlockSpec((cb, 2, gp, 128), lambda b, j: (b * nblk + j, 0, 0, 0)),
                  pl.BlockSpec((cb, 2, gp, 128), lambda b, j: (b * nblk + bwd(j), 1, 0, 0))],
        out_specs=[pl.BlockSpec((cb, 2, gp, 128), lambda b, j: (b * nblk + j, 0, 0, 0)),
                   pl.BlockSpec((cb, 2, gp, 128), lambda b, j: (b * nblk + bwd(j), 0, 0, 0))],
        out_shape=[jax.ShapeDtypeStruct((rows, 2, gp, 128), F32)] * 2,
        scratch_shapes=[pltpu.VMEM((4, gp, 128), F32)],
        compiler_params=_cparams("parallel", "arbitrary"),
        name="s5_scan",
    )(decay, s_scan, s_scan)
    st = jnp.concatenate([st_f, st_b], axis=1).reshape(rows, 4, g, S5_P)
    st_g = jnp.transpose(st, (2, 0, 1, 3)).reshape(g, rows, n_st)

    d_vec = jnp.tile(d_skip.astype(F32).reshape(g, 1, S5_H), (1, S5_L, 1)).reshape(g, 1, lh)
    y_g = pl.pallas_call(
        _s5_out_kernel,
        grid=(g,),
        in_specs=[pl.BlockSpec((None, rows, lh), lambda i: (i, 0, 0)),
                  pl.BlockSpec((None, rows, n_st), lambda i: (i, 0, 0)),
                  pl.BlockSpec((None, n_st, lh), lambda i: (i, 0, 0)),
                  pl.BlockSpec((None, rows, lh), lambda i: (i, 0, 0)),
                  pl.BlockSpec((None, 1, lh), lambda i: (i, 0, 0))],
        out_specs=pl.BlockSpec((None, rows, lh), lambda i: (i, 0, 0)),
        out_shape=jax.ShapeDtypeStruct((g, rows, lh), BF16),
        compiler_params=_cparams("parallel"),
        name="s5_out",
    )(yi, st_g, w_st, u_g, d_vec)
    y = jnp.transpose(y_g.reshape(g, rows, S5_L, S5_H), (1, 2, 0, 3)).reshape(t, width)

    return pl.pallas_call(
        _glu_kernel,
        grid=(t // ROW_TILE,),
        in_specs=[pl.BlockSpec((ROW_TILE, width), lambda i: (i, 0)),
                  pl.BlockSpec((width, width), lambda i: (0, 0))],
        out_specs=pl.BlockSpec((ROW_TILE, width), lambda i: (i, 0)),
        out_shape=jax.ShapeDtypeStruct((t, width), BF16),
        compiler_params=_cparams("parallel"),
        name="s5_glu",
    )(y, w_glu.astype(BF16))


def _gla_kernel(lb_ref, qf_ref, ff_ref, vf_ref, qb_ref, fb_ref, vb_ref, of_ref, ob_ref, st_ref, *, n_heads, hd):
    @pl.when(pl.program_id(1) == 0)
    def _():
        st_ref[...] = jnp.zeros_like(st_ref)

    c = qf_ref.shape[0]
    items = []
    for d, (q_ref, f_ref, v_ref, o_ref) in enumerate(((qf_ref, ff_ref, vf_ref, of_ref),
                                                       (qb_ref, fb_ref, vb_ref, ob_ref))):
        lower = d == 0
        mask = _tri_mask(c, lower)
        tri = mask.astype(BF16)
        lb = lb_ref[d]
        sig = jax.nn.sigmoid(f_ref[...])
        kk = (1.0 - lb) * (1.0 - sig)
        logf = jnp.log(lb + (1.0 - lb) * sig)
        b = _tri_dot(tri, logf)
        b_end = b[c - 1:c, :] if lower else b[0:1, :]
        qs = _silu(q_ref[...])
        qt = (qs * jnp.exp(b)).astype(BF16)
        q_blk, k_blk = [], []
        for blk in range(c // GLA_SUB):
            r0, r1 = blk * GLA_SUB, (blk + 1) * GLA_SUB
            if lower:
                beta = b[r0 - 1:r0, :] if blk > 0 else 0.0
            else:
                beta = b[r1:r1 + 1, :] if r1 < c else 0.0
            q_blk.append((qs[r0:r1, :] * jnp.exp(b[r0:r1, :] - beta)).astype(BF16))
            k_blk.append((kk * jnp.exp(jnp.minimum(beta - b, EXP_CLAMP))).astype(BF16))
        kend = (kk * jnp.exp(b_end - b)).astype(BF16)
        dec = jnp.exp(b_end)
        v = v_ref[...].astype(BF16)
        for h in range(n_heads):
            sl = slice(h * hd, (h + 1) * hd)
            items.append((d, h, sl, o_ref, mask, qt[:, sl], [(qb[:, sl], kb[:, sl]) for qb, kb in zip(q_blk, k_blk)],
                          kend[:, sl], v[:, sl], dec[:, sl], st_ref[d, h]))
    att = [jnp.concatenate([_dot_nt(qb, kb) for qb, kb in it[6]], axis=0) for it in items]
    inter = [_dot_nt(it[5], it[10]) for it in items]
    upd = [_dot_tn(it[8], it[7]) for it in items]
    intra = [_dot(jnp.where(it[4], a, 0.0), it[8]) for it, a in zip(items, att)]
    for it, oi, os, up in zip(items, intra, inter, upd):
        d, h, sl, o_ref = it[:4]
        o_ref[:, sl] = (oi + os).astype(o_ref.dtype)
        st_ref[d, h] = it[9] * it[10] + up


def mixer_gla(z, n_batch, n_tok, n_ctx, col0, lb, n_heads, hd):
    width = n_heads * hd
    z3 = z.reshape(n_batch, n_tok, z.shape[1])
    nc, cc = n_tok // CHUNK, n_ctx // CHUNK
    cb0 = col0 // width
    bwd = functools.partial(_bwd_chunk, ctx_chunks=cc, n_chunks=nc)
    blk = (None, CHUNK, width)
    fw = lambda k: pl.BlockSpec(blk, lambda b, j: (b, j, cb0 + k))
    bw = lambda k: pl.BlockSpec(blk, lambda b, j: (b, bwd(j), cb0 + k))
    o_f, o_b = pl.pallas_call(
        functools.partial(_gla_kernel, n_heads=n_heads, hd=hd),
        grid=(n_batch, nc),
        in_specs=[pl.BlockSpec((2, 1, width), lambda b, j: (0, 0, 0)),
                  fw(0), fw(1), fw(3), bw(0), bw(2), bw(3)],
        out_specs=[pl.BlockSpec(blk, lambda b, j: (b, j, 0)),
                   pl.BlockSpec(blk, lambda b, j: (b, bwd(j), 0))],
        out_shape=[jax.ShapeDtypeStruct((n_batch, n_tok, width), BF16)] * 2,
        scratch_shapes=[pltpu.VMEM((2, n_heads, hd, hd), F32)],
        compiler_params=_cparams("parallel", "arbitrary"),
        name="gla_scan",
    )(lb.reshape(2, 1, width).astype(F32), z3, z3, z3, z3, z3, z3)
    return o_f.reshape(-1, width), o_b.reshape(-1, width)


def _rotary_tables(n_tok, n_ctx):
    n_lat = n_tok - n_ctx
    rows = n_lat // GRID_W
    row = jnp.repeat(jnp.arange(rows, dtype=F32), GRID_W)
    col = jnp.tile(jnp.arange(GRID_W, dtype=F32), rows)
    n_freq = C_QK // 4
    inv = ROPE_BASE ** (-jnp.arange(n_freq, dtype=F32) / n_freq)
    ang = jnp.concatenate([row[:, None] * inv, col[:, None] * inv], axis=-1)
    ang = jnp.concatenate([jnp.zeros((n_ctx, C_QK // 2), F32), ang], axis=0)
    cos, sin = jnp.cos(ang), jnp.sin(ang)
    return jnp.concatenate([cos, cos], axis=-1), jnp.concatenate([-sin, sin], axis=-1)


def _ret_kernel(cdec_ref, dmat_ref, qdec_ref, kdec_ref,
                qf_ref, kf_ref, vf_ref, cf_ref, sf_ref, qb_ref, kb_ref, vb_ref, cb_ref, sb_ref,
                of_ref, ob_ref, st_ref, *, n_heads):
    @pl.when(pl.program_id(1) == 0)
    def _():
        st_ref[...] = jnp.zeros_like(st_ref)

    half = C_QK // 2
    items = []
    for d, (q_ref, k_ref, v_ref, cos_ref, sin_ref, o_ref) in enumerate(
            ((qf_ref, kf_ref, vf_ref, cf_ref, sf_ref, of_ref), (qb_ref, kb_ref, vb_ref, cb_ref, sb_ref, ob_ref))):
        cos, sin = cos_ref[...], sin_ref[...]
        for h in range(n_heads):
            qs = slice(h * C_QK, (h + 1) * C_QK)
            vs = slice(h * C_V, (h + 1) * C_V)
            qh, kh = q_ref[:, qs].astype(F32), k_ref[:, qs].astype(F32)
            qh = (qh * cos + pltpu.roll(qh, half, axis=1) * sin) * (C_QK ** -0.5)
            kh = kh * cos + pltpu.roll(kh, half, axis=1) * sin
            items.append((d, h, vs, o_ref, qh.astype(BF16), kh.astype(BF16), (qh * qdec_ref[d, h]).astype(BF16),
                          (kh * kdec_ref[d, h]).astype(BF16), v_ref[:, vs].astype(BF16), st_ref[d, h]))
    att = [_dot_nt(it[4], it[5]) for it in items]
    inter = [_dot(it[6], it[9]) for it in items]
    upd = [_dot_tn(it[7], it[8]) for it in items]
    intra = [_dot(a * dmat_ref[it[0], it[1]], it[8]) for it, a in zip(items, att)]
    for it, oi, os, up in zip(items, intra, inter, upd):
        d, h, vs, o_ref = it[:4]
        o_ref[:, vs] = (oi + os).astype(o_ref.dtype)
        st_ref[d, h] = cdec_ref[d, h] * it[9] + up


def mixer_retention(z, n_batch, n_tok, n_ctx, decay_logit, n_heads):
    qw, vw = n_heads * C_QK, n_heads * C_V
    z3 = z.reshape(n_batch, n_tok, z.shape[1])
    nc, cc = n_tok // CHUNK, n_ctx // CHUNK
    bwd = functools.partial(_bwd_chunk, ctx_chunks=cc, n_chunks=nc)
    log_gamma = jax.nn.log_sigmoid(decay_logit.astype(F32))[:, :, None, None]
    idx = jnp.arange(CHUNK, dtype=F32)
    diff = idx[:, None] - idx[None, :]
    dmat_f = jnp.where(diff >= 0, jnp.exp(jnp.maximum(diff, 0.0) * log_gamma[0]), 0.0)
    dmat_b = jnp.where(diff <= 0, jnp.exp(jnp.maximum(-diff, 0.0) * log_gamma[1]), 0.0)
    dmat = jnp.stack([dmat_f, dmat_b])
    ones = jnp.ones((1, 1, 1, C_QK), F32)
    pos_f, pos_b = idx[None, None, :, None], (CHUNK - 1 - idx)[None, None, :, None]
    lg = log_gamma
    qdec = jnp.concatenate([jnp.exp((pos_f + 1) * lg[0:1]), jnp.exp((pos_b + 1) * lg[1:2])]) * ones
    kdec = jnp.concatenate([jnp.exp((CHUNK - 1 - pos_f) * lg[0:1]), jnp.exp((CHUNK - 1 - pos_b) * lg[1:2])]) * ones
    cdec = jnp.exp(CHUNK * log_gamma[:, :, 0, 0])
    cos2, sin2 = _rotary_tables(n_tok, n_ctx)
    full = lambda shape: pl.BlockSpec(shape, lambda b, j: (0,) * len(shape))
    tab = lambda order: pl.BlockSpec((CHUNK, C_QK), lambda b, j: (order(j), 0))
    ident = lambda j: j
    vcb = 2 * qw // vw
    assert vcb * vw == 2 * qw
    def specs(order):
        return [pl.BlockSpec((None, CHUNK, qw), lambda b, j: (b, order(j), 0)),
                pl.BlockSpec((None, CHUNK, qw), lambda b, j: (b, order(j), 1)),
                pl.BlockSpec((None, CHUNK, vw), lambda b, j: (b, order(j), vcb)),
                tab(order), tab(order)]
    o_f, o_b = pl.pallas_call(
        functools.partial(_ret_kernel, n_heads=n_heads),
        grid=(n_batch, nc),
        in_specs=[pl.BlockSpec(memory_space=pltpu.SMEM), full((2, n_heads, CHUNK, CHUNK)),
                  full((2, n_heads, CHUNK, C_QK)), full((2, n_heads, CHUNK, C_QK))] + specs(ident) + specs(bwd),
        out_specs=[pl.BlockSpec((None, CHUNK, vw), lambda b, j: (b, j, 0)),
                   pl.BlockSpec((None, CHUNK, vw), lambda b, j: (b, bwd(j), 0))],
        out_shape=[jax.ShapeDtypeStruct((n_batch, n_tok, vw), BF16)] * 2,
        scratch_shapes=[pltpu.VMEM((2, n_heads, C_QK, C_V), F32)],
        compiler_params=_cparams("parallel", "arbitrary"),
        name="retention_scan",
    )(cdec, dmat, qdec, kdec, z3, z3, z3, cos2, sin2, z3, z3, z3, cos2, sin2)
    return o_f.reshape(-1, vw), o_b.reshape(-1, vw)


def _gdn_conv_kernel(prev_ref, cur_ref, next_ref, w_ref, o_ref, *, tiles_per_row, ctx_tiles, n_heads, hd):
    i = pl.program_id(0)
    r = i % tiles_per_row
    first = jnp.logical_or(r == 0, r == ctx_tiles)
    last = jnp.logical_or(r == ctx_tiles - 1, r == tiles_per_row - 1)
    x = cur_ref[...].astype(F32)
    rows = x.shape[0]
    rid = lax.broadcasted_iota(jnp.int32, x.shape, 0)
    hp = prev_ref.shape[0]
    x_prev = jnp.where(first, 0.0, prev_ref[hp - 1:hp, :].astype(F32))
    x_next = jnp.where(last, 0.0, next_ref[0:1, :].astype(F32))
    left = jnp.where(rid == 0, x_prev, pltpu.roll(x, 1, axis=0))
    right = jnp.where(rid == rows - 1, x_next, pltpu.roll(x, rows - 1, axis=0))
    w = w_ref[...]
    y = _silu(left * w[0:1, :] + x * w[1:2, :] + right * w[2:3, :])
    width = n_heads * hd
    for h in range(3 * n_heads):
        sl = slice(h * hd, (h + 1) * hd)
        yh = y[:, sl]
        if h < 2 * n_heads:
            yh = yh * lax.rsqrt(jnp.sum(yh * yh, axis=-1, keepdims=True) + EPS)
            if h < n_heads:
                yh = yh * (hd ** -0.5)
        o_ref[:, sl] = yh.astype(o_ref.dtype)


def _gdn_kernel(qf_ref, kf_ref, vf_ref, cf_ref, rf_ref, qb_ref, kb_ref, vb_ref, cb_ref, rb_ref,
                of_ref, ob_ref, st_ref, *, n_heads, hd):
    @pl.when(pl.program_id(1) == 0)
    def _():
        st_ref[...] = jnp.zeros_like(st_ref)

    c = qf_ref.shape[0]
    eye = (lax.broadcasted_iota(jnp.int32, (c, c), 0) == lax.broadcasted_iota(jnp.int32, (c, c), 1)).astype(F32)
    items = []
    for d, (q_ref, k_ref, v_ref, col_ref, row_ref, o_ref) in enumerate(
            ((qf_ref, kf_ref, vf_ref, cf_ref, rf_ref, of_ref), (qb_ref, kb_ref, vb_ref, cb_ref, rb_ref, ob_ref))):
        lower = d == 0
        incl = _tri_mask(c, lower)
        strict = jnp.logical_and(incl, eye == 0.0)
        col = col_ref[...]
        row = row_ref[...]
        b_cols = _tri_dot(incl.astype(BF16), col)
        b_rows = _dot_tri(row, _tri_mask(c, not lower).astype(BF16))
        for h in range(n_heads):
            sl = slice(h * hd, (h + 1) * hd)
            ib, ig = d * n_heads + h, (2 + d) * n_heads + h
            be = col[:, ib:ib + 1]
            bc = b_cols[:, ig:ig + 1]
            br = b_rows[ig:ig + 1, :]
            b_end = bc[c - 1:c, :] if lower else bc[0:1, :]
            gam_i = jnp.exp(jnp.where(incl, bc - br, -1e30))
            gam_s = jnp.where(strict, gam_i, 0.0)
            qh, kh, vh = q_ref[:, sl], k_ref[:, sl], v_ref[:, sl]
            eb = jnp.exp(bc)
            rhs = jnp.concatenate([be * vh.astype(F32), (be * eb) * kh.astype(F32)], axis=1).astype(BF16)
            items.append(dict(d=d, h=h, sl=sl, o_ref=o_ref, be=be, gam_i=gam_i, gam_s=gam_s, q=qh, k=kh, rhs=rhs,
                              qe=(qh.astype(F32) * eb).astype(BF16), dec=jnp.exp(b_end),
                              ke=(kh.astype(F32) * jnp.exp(b_end - bc)).astype(BF16), s=st_ref[d, h]))
    kk = [_dot_nt(it["k"], it["k"]) for it in items]
    qk = [_dot_nt(it["q"], it["k"]) for it in items]
    qs = [_dot(it["qe"], it["s"]) for it in items]
    ns = [it["be"] * a * it["gam_s"] for it, a in zip(items, kk)]
    invs = [eye - n for n in ns]
    ps = [_dot(n, n) for n in ns]
    levels = int(np.log2(c)) - 1
    for lvl in range(levels):
        prods = [_dot(inv, p) for inv, p in zip(invs, ps)]
        if lvl < levels - 1:
            ps = [_dot(p, p) for p in ps]
        invs = [inv + pr for inv, pr in zip(invs, prods)]
    sols = [_dot(inv, it["rhs"]) for inv, it in zip(invs, items)]
    sks = [_dot(sol[:, hd:], it["s"]) for sol, it in zip(sols, items)]
    us = [sol[:, :hd] - sk for sol, sk in zip(sols, sks)]
    intra = [_dot(a * it["gam_i"], u) for a, it, u in zip(qk, items, us)]
    upd = [_dot_tn(it["ke"], u) for it, u in zip(items, us)]
    for it, oi, os, up in zip(items, intra, qs, upd):
        it["o_ref"][:, it["sl"]] = (oi + os).astype(it["o_ref"].dtype)
        st_ref[it["d"], it["h"]] = it["dec"] * it["s"] + up


def mixer_gdn(z, n_batch, n_tok, n_ctx, col0, conv_w, a_log, dt_bias, n_heads, hd):
    t, zw = z.shape
    width = n_heads * hd
    tiles_per_row, ctx_tiles = n_tok // ROW_TILE, n_ctx // ROW_TILE
    cb0 = col0 // (3 * width)
    assert cb0 * 3 * width == col0
    halo = 8
    hpt = ROW_TILE // halo
    n_halo = t // halo
    qkv = pl.pallas_call(
        functools.partial(_gdn_conv_kernel, tiles_per_row=tiles_per_row, ctx_tiles=ctx_tiles, n_heads=n_heads, hd=hd),
        grid=(t // ROW_TILE,),
        in_specs=[pl.BlockSpec((halo, 3 * width), lambda i: (jnp.maximum(i * hpt - 1, 0), cb0)),
                  pl.BlockSpec((ROW_TILE, 3 * width), lambda i: (i, cb0)),
                  pl.BlockSpec((halo, 3 * width), lambda i: (jnp.minimum((i + 1) * hpt, n_halo - 1), cb0)),
                  pl.BlockSpec((3, 3 * width), lambda i: (0, 0))],
        out_specs=pl.BlockSpec((ROW_TILE, 3 * width), lambda i: (i, 0)),
        out_shape=jax.ShapeDtypeStruct((t, 3 * width), BF16),
        compiler_params=_cparams("parallel"),
        name="gdn_conv",
    )(z, z, z, conv_w.astype(F32))

    sc0 = col0 + 4 * width
    small = z[:, sc0:sc0 + 4 * n_heads].astype(F32)
    a_log, dt_bias = a_log.astype(F32), dt_bias.astype(F32)
    be = jax.nn.sigmoid(small[:, :2 * n_heads])
    la_f = -jnp.exp(a_log[0]) * jax.nn.softplus(small[:, 2 * n_heads:3 * n_heads] + dt_bias[0])
    la_b = -jnp.exp(a_log[1]) * jax.nn.softplus(small[:, 3 * n_heads:] + dt_bias[1])
    cols = jnp.concatenate([be, la_f, la_b], axis=1)
    nc, cc = n_tok // CHUNK, n_ctx // CHUNK
    cols3 = cols.reshape(n_batch, n_tok, 4 * n_heads)
    rows4 = jnp.transpose(cols.reshape(n_batch, nc, CHUNK, 4 * n_heads), (0, 1, 3, 2))
    qkv3 = qkv.reshape(n_batch, n_tok, 3 * width)
    bwd = functools.partial(_bwd_chunk, ctx_chunks=cc, n_chunks=nc)
    ident = lambda j: j
    def specs(order):
        return [pl.BlockSpec((None, CHUNK, width), lambda b, j: (b, order(j), 0)),
                pl.BlockSpec((None, CHUNK, width), lambda b, j: (b, order(j), 1)),
                pl.BlockSpec((None, CHUNK, width), lambda b, j: (b, order(j), 2)),
                pl.BlockSpec((None, CHUNK, 4 * n_heads), lambda b, j: (b, order(j), 0)),
                pl.BlockSpec((None, None, 4 * n_heads, CHUNK), lambda b, j: (b, order(j), 0, 0))]
    o_f, o_b = pl.pallas_call(
        functools.partial(_gdn_kernel, n_heads=n_heads, hd=hd),
        grid=(n_batch, nc),
        in_specs=specs(ident) + specs(bwd),
        out_specs=[pl.BlockSpec((None, CHUNK, width), lambda b, j: (b, j, 0)),
                   pl.BlockSpec((None, CHUNK, width), lambda b, j: (b, bwd(j), 0))],
        out_shape=[jax.ShapeDtypeStruct((n_batch, n_tok, width), BF16)] * 2,
        scratch_shapes=[pltpu.VMEM((2, n_heads, hd, hd), F32)],
        compiler_params=_cparams("parallel", "arbitrary"),
        name="gdn_scan",
    )(qkv3, qkv3, qkv3, cols3, rows4, qkv3, qkv3, qkv3, cols3, rows4)
    return o_f.reshape(-1, width), o_b.reshape(-1, width)


def _first_max(x, idx, sentinel):
    m = jnp.max(x, axis=0, keepdims=True)
    return m, jnp.min(jnp.where(x == m, idx, sentinel), axis=0, keepdims=True)


def _router_kernel(h_ref, wt_ref, bias_ref, e_ref, g_ref, rank_ref, cnt_ref, carry_ref):
    @pl.when(pl.program_id(0) == 0)
    def _():
        carry_ref[...] = jnp.zeros_like(carry_ref)

    cols = h_ref.shape[0]
    gsz = N_EXPERTS // N_GROUPS
    neg = -jnp.inf
    logits = lax.dot_general(wt_ref[...], h_ref[...], (((1,), (1,)), ((), ())), preferred_element_type=F32)
    scores = jax.nn.sigmoid(logits)
    sel = scores + bias_ref[...]
    i_g = lax.broadcasted_iota(jnp.int32, (gsz, cols), 0)
    blocks, g_scores = [], []
    for g in range(N_GROUPS):
        blk = sel[g * gsz:(g + 1) * gsz, :]
        m1, first = _first_max(blk, i_g, gsz)
        m2 = jnp.max(jnp.where(i_g == first, neg, blk), axis=0, keepdims=True)
        blocks.append(blk)
        g_scores.append(m1 + m2)
    keep = [jnp.zeros((1, cols), jnp.bool_) for _ in range(N_GROUPS)]
    for _ in range(TOPK_GROUPS):
        best = functools.reduce(jnp.maximum, g_scores)
        found = jnp.zeros((1, cols), jnp.bool_)
        for g in range(N_GROUPS):
            pick = jnp.logical_and(g_scores[g] == best, jnp.logical_not(found))
            found = jnp.logical_or(found, pick)
            keep[g] = jnp.logical_or(keep[g], pick)
            g_scores[g] = jnp.where(pick, neg, g_scores[g])
    cur = jnp.concatenate([jnp.where(keep[g], blocks[g], neg) for g in range(N_GROUPS)], axis=0)
    i_e = lax.broadcasted_iota(jnp.int32, (N_EXPERTS, cols), 0)
    picks, gates = [], []
    for _ in range(TOP_K):
        _, idx = _first_max(cur, i_e, N_EXPERTS)
        pick = i_e == idx
        picks.append((idx, pick))
        gates.append(jnp.sum(jnp.where(pick, scores, 0.0), axis=0, keepdims=True))
        cur = jnp.where(pick, neg, cur)
    total = functools.reduce(jnp.add, gates)
    chosen = functools.reduce(jnp.logical_or, [p for _, p in picks]).astype(F32)
    r = lax.broadcasted_iota(jnp.int32, (cols, cols), 0)
    c = lax.broadcasted_iota(jnp.int32, (cols, cols), 1)
    before = jnp.dot(chosen.astype(BF16), (r < c).astype(BF16), preferred_element_type=F32) + carry_ref[...]
    for k, (idx, pick) in enumerate(picks):
        e_ref[k:k + 1, :] = idx
        g_ref[k:k + 1, :] = gates[k] / total * ROUTED_SCALE
        rank_ref[k:k + 1, :] = jnp.sum(jnp.where(pick, before, 0.0), axis=0, keepdims=True).astype(jnp.int32)
    carry_ref[...] += jnp.sum(chosen, axis=1, keepdims=True)
    cnt_ref[...] = carry_ref[...]


def _slot_fetch(slots_hbm, idx_smem, sem_idx, tile, buf):
    return pltpu.make_async_copy(slots_hbm.at[tile], idx_smem.at[buf], sem_idx.at[buf])


def _expert_kernel(be_ref, nu_ref, tok_hbm, hp_hbm, wg_ref, wu_ref, wd_ref, o_ref,
                   wg_s, wu_s, wd_s, idx_smem, x_buf, sem_idx, sem_rows):
    i = pl.program_id(0)
    n_live = nu_ref[0]
    live = i < n_live
    new_expert = jnp.logical_or(i == 0, be_ref[i] != be_ref[jnp.maximum(i - 1, 0)])
    rows = x_buf.shape[1]
    idx_rows = idx_smem.shape[1]

    def idx_fetch(blk):
        return pltpu.make_async_copy(tok_hbm.at[blk], idx_smem.at[blk % 2], sem_idx.at[blk % 2])

    def gather_start(blk):
        b = blk % 2
        for q in range(idx_rows):
            def body(r, carry):
                pltpu.make_async_copy(hp_hbm.at[pl.ds(idx_smem[b, q, r], 1)],
                                      x_buf.at[b, pl.ds(q * LANES + r, 1)], sem_rows.at[b]).start(priority=q % 2)
                return carry
            lax.fori_loop(0, LANES, body, 0, unroll=8)

    @pl.when(i == 0)
    def _():
        idx_fetch(0).start()
        idx_fetch(0).wait()
        gather_start(0)

        @pl.when(n_live > 1)
        def _():
            idx_fetch(1).start()

    @pl.when(i + 1 < n_live)
    def _():
        idx_fetch(i + 1).wait()
        gather_start(i + 1)

        @pl.when(i + 2 < n_live)
        def _():
            idx_fetch(i + 2).start()

    @pl.when(jnp.logical_and(live, new_expert))
    def _():
        wg_s[...] = wg_ref[...].astype(BF16)
        wu_s[...] = wu_ref[...].astype(BF16)
        wd_s[...] = wd_ref[...].astype(BF16)

    @pl.when(live)
    def _():
        b = i % 2
        pltpu.make_async_copy(hp_hbm.at[pl.ds(0, rows)], x_buf.at[b], sem_rows.at[b]).wait()
        lo, hi = _unpack_halves(_load_rows(x_buf.at[b], 0, rows))
        lo, hi = lo.astype(BF16), hi.astype(BF16)
        half = lo.shape[1]
        g = (jnp.dot(lo, wg_s[:half, :], preferred_element_type=F32)
             + jnp.dot(hi, wg_s[half:, :], preferred_element_type=F32))
        u = (jnp.dot(lo, wu_s[:half, :], preferred_element_type=F32)
             + jnp.dot(hi, wu_s[half:, :], preferred_element_type=F32))
        a = (_silu(g) * u).astype(BF16)
        _store_rows(o_ref, _pack_halves(jnp.dot(a, wd_s[...], preferred_element_type=F32)))

    @pl.when(jnp.logical_not(live))
    def _():
        o_ref[...] = jnp.zeros_like(o_ref)


def _combine_kernel(slots_hbm, y_hbm, h_ref, gate_ref, xs_ref, mg_ref, wg_ref, wu_ref, wd_ref, o_ref,
                    idx_smem, rows_buf, sem_idx, sem_rows):
    i = pl.program_id(0)
    rows = h_ref.shape[0]

    @pl.when(i == 0)
    def _():
        _slot_fetch(slots_hbm, idx_smem, sem_idx, 0, 0).start()

    buf = i % 2
    _slot_fetch(slots_hbm, idx_smem, sem_idx, i, buf).wait()

    @pl.when(i + 1 < pl.num_programs(0))
    def _():
        _slot_fetch(slots_hbm, idx_smem, sem_idx, i + 1, 1 - buf).start()

    def body(r, carry):
        for k in range(TOP_K):
            pltpu.make_async_copy(y_hbm.at[pl.ds(idx_smem[buf, k, r], 1)], rows_buf.at[pl.ds(k * rows + r, 1)],
                                  sem_rows).start(priority=k % 2)
        return carry

    lax.fori_loop(0, rows, body, 0, unroll=8)

    h = h_ref[...]
    g = jnp.dot(h, wg_ref[...], preferred_element_type=F32)
    u = jnp.dot(h, wu_ref[...], preferred_element_type=F32)
    a = (_silu(g) * u).astype(BF16)
    y = jnp.dot(a, wd_ref[...], preferred_element_type=F32)

    pltpu.make_async_copy(y_hbm.at[pl.ds(0, rows * TOP_K)], rows_buf, sem_rows).wait()
    half = y.shape[1] // 2
    acc_lo, acc_hi = y[:, :half], y[:, half:]
    gate = gate_ref[...]
    for k in range(TOP_K):
        lo, hi = _unpack_halves(_load_rows(rows_buf, k * rows, rows))
        acc_lo = acc_lo + gate[:, k:k + 1] * lo
        acc_hi = acc_hi + gate[:, k:k + 1] * hi
    mg = mg_ref[...]
    o_ref[:, :half] = xs_ref[:, :half] + mg[:, :half] * acc_lo
    o_ref[:, half:] = xs_ref[:, half:] + mg[:, half:] * acc_hi


def route(h2, w_router, router_bias):
    t, d = h2.shape
    n_tiles = t // ROW_TILE
    n_exp = w_router.shape[1]
    return pl.pallas_call(
        _router_kernel,
        grid=(n_tiles,),
        in_specs=[pl.BlockSpec((ROW_TILE, d), lambda i: (i, 0)),
                  pl.BlockSpec((n_exp, d), lambda i: (0, 0)),
                  pl.BlockSpec((n_exp, 1), lambda i: (0, 0))],
        out_specs=[pl.BlockSpec((TOP_K, ROW_TILE), lambda i: (0, i)),
                   pl.BlockSpec((TOP_K, ROW_TILE), lambda i: (0, i)),
                   pl.BlockSpec((TOP_K, ROW_TILE), lambda i: (0, i)),
                   pl.BlockSpec((n_exp, 1), lambda i: (0, 0))],
        out_shape=[jax.ShapeDtypeStruct((TOP_K, t), jnp.int32), jax.ShapeDtypeStruct((TOP_K, t), F32),
                   jax.ShapeDtypeStruct((TOP_K, t), jnp.int32), jax.ShapeDtypeStruct((n_exp, 1), F32)],
        scratch_shapes=[pltpu.VMEM((n_exp, 1), F32)],
        compiler_params=_cparams("arbitrary"),
        name="router_topk",
    )(h2, jnp.transpose(w_router).astype(BF16), router_bias.astype(F32).reshape(n_exp, 1))


def moe_block(h2, h2_packed, xs, mod, k_gate, tiles_per_row, n_batch,
              w_router, router_bias, w_gate, w_up, w_down, ws_gate, ws_up, ws_down):
    t, d = h2.shape
    row_shape = h2_packed.shape[1:]
    de = ws_gate.shape[1]
    n_tiles = t // ROW_TILE
    eidx, gate_t, rank, cnt = route(h2, w_router, router_bias)
    n_blocks = -(-t * TOP_K // MOE_BLOCK) + N_EXPERTS
    counts = cnt[:, 0].astype(jnp.int32)
    padded = (counts + MOE_BLOCK - 1) // MOE_BLOCK * MOE_BLOCK
    pad_end = jnp.cumsum(padded).astype(jnp.int32)
    pad_start = pad_end - padded
    e_ids = jnp.arange(N_EXPERTS, dtype=jnp.int32)
    slot = rank + jnp.sum(jnp.where(eidx[:, :, None] == e_ids, pad_start, 0), axis=-1)
    slots = jnp.transpose(slot.reshape(TOP_K, n_tiles, ROW_TILE), (1, 0, 2))
    gate = jnp.transpose(gate_t)
    block_start = jnp.arange(n_blocks, dtype=jnp.int32) * MOE_BLOCK
    block_e = jnp.minimum(jnp.sum(pad_end[None, :] <= block_start[:, None], axis=1), N_EXPERTS - 1).astype(jnp.int32)
    n_used = (pad_end[-1] // MOE_BLOCK).astype(jnp.int32).reshape(1)

    idx_scratch = [pltpu.SMEM((2, TOP_K, ROW_TILE), jnp.int32)]
    token_ids = jnp.broadcast_to(jnp.arange(t, dtype=jnp.int32), (TOP_K, t))
    slot_tok = jnp.zeros((n_blocks * MOE_BLOCK,), jnp.int32).at[slot.reshape(-1)].set(
        token_ids.reshape(-1), unique_indices=True)
    slot_tok = slot_tok.reshape(n_blocks, MOE_BLOCK // LANES, LANES)

    y_sorted = pl.pallas_call(
        _expert_kernel,
        grid_spec=pltpu.PrefetchScalarGridSpec(
            num_scalar_prefetch=2,
            grid=(n_blocks,),
            in_specs=[pl.BlockSpec(memory_space=pl.ANY),
                      pl.BlockSpec(memory_space=pl.ANY),
                      pl.BlockSpec((None, d, de), lambda i, be, nu: (be[i], 0, 0)),
                      pl.BlockSpec((None, d, de), lambda i, be, nu: (be[i], 0, 0)),
                      pl.BlockSpec((None, de, d), lambda i, be, nu: (be[i], 0, 0))],
            out_specs=pl.BlockSpec((MOE_BLOCK,) + row_shape, lambda i, be, nu: (i, 0, 0)),
            scratch_shapes=[pltpu.VMEM((d, de), BF16), pltpu.VMEM((d, de), BF16), pltpu.VMEM((de, d), BF16),
                            pltpu.SMEM((2, MOE_BLOCK // LANES, LANES), jnp.int32),
                            pltpu.VMEM((2, MOE_BLOCK) + row_shape, jnp.uint32),
                            pltpu.SemaphoreType.DMA((2,)), pltpu.SemaphoreType.DMA((2,))],
        ),
        out_shape=jax.ShapeDtypeStruct((n_blocks * MOE_BLOCK,) + row_shape, jnp.uint32),
        compiler_params=_cparams("arbitrary"),
        name="routed_experts",
    )(block_e, n_used, slot_tok, h2_packed, w_gate, w_up, w_down)

    seg = functools.partial(_seg_of_tile, tiles_per_row=tiles_per_row, n_batch=n_batch)
    return pl.pallas_call(
        _combine_kernel,
        grid=(n_tiles,),
        in_specs=[pl.BlockSpec(memory_space=pl.ANY),
                  pl.BlockSpec(memory_space=pl.ANY),
                  pl.BlockSpec((ROW_TILE, d), lambda i: (i, 0)),
                  pl.BlockSpec((ROW_TILE, TOP_K), lambda i: (i, 0)),
                  pl.BlockSpec((ROW_TILE, d), lambda i: (i, 0)),
                  pl.BlockSpec((None, None, 1, d), lambda i: (seg(i), k_gate, 0, 0)),
                  pl.BlockSpec((d, de), lambda i: (0, 0)),
                  pl.BlockSpec((d, de), lambda i: (0, 0)),
                  pl.BlockSpec((de, d), lambda i: (0, 0))],
        out_specs=pl.BlockSpec((ROW_TILE, d), lambda i: (i, 0)),
        out_shape=jax.ShapeDtypeStruct((t, d), F32),
        scratch_shapes=idx_scratch + [pltpu.VMEM((TOP_K * ROW_TILE,) + row_shape, jnp.uint32),
                                      pltpu.SemaphoreType.DMA((2,)), pltpu.SemaphoreType.DMA(())],
        compiler_params=_cparams("arbitrary"),
        name="moe_combine",
    )(slots, y_sorted, h2, gate, xs, mod, ws_gate.astype(BF16), ws_up.astype(BF16), ws_down.astype(BF16))


def kernel(x, c, ctx, c_ctx, w_ada, b_ada, norm1_g, norm2_g, w_in_ab, s5_lambda_re, s5_lambda_im, s5_log_dt, s5_b_re, s5_b_im, s5_c_re, s5_c_im, s5_d, s5_w_glu, hgrn_lb, w_in_cd, ret_decay_logit, gdn_conv_w, gdn_a_log, gdn_dt_bias, gdn_norm_g, w_out, w_router, router_bias, w_exp_gate, w_exp_up, w_exp_down, w_sh_gate, w_sh_up, w_sh_down, final_norm_g):
    n_batch, n_lat, d = x.shape
    n_ctx = ctx.shape[1]
    n_tok = n_ctx + n_lat
    depth = w_ada.shape[0]
    assert n_ctx % ROW_TILE == 0 and n_lat % ROW_TILE == 0
    tiles_per_row, ctx_tiles = n_tok // ROW_TILE, n_ctx // ROW_TILE
    assert ctx_tiles == 1
    t = n_batch * n_tok

    xs = jnp.concatenate([ctx, x], axis=1).reshape(t, d)
    lb_all = jnp.cumsum(jax.nn.softmax(hgrn_lb.astype(F32), axis=0), axis=0)
    cond = jnp.concatenate([c, c_ctx[None]], axis=0)
    cond = jnp.pad(jax.nn.silu(cond), ((0, 8 - (n_batch + 1)), (0, 0)))
    ones_g = jnp.ones((D_HEAD,), F32)

    for i in range(depth):
        j = i // 2
        mod = matmul(cond, w_ada[i], F32, 8, 1024, "ada_mod")[:n_batch + 1] + b_ada[i]
        mod = mod.reshape(n_batch + 1, 6, 1, d)
        h = norm_mod(xs, norm1_g[i], mod, 0, 1, tiles_per_row, n_batch)
        if i % 2 == 0:
            z = matmul(h, w_in_ab[j].astype(BF16), F32, 512, 1024, "in_proj")
            a_width = s5_d.shape[1]
            b_heads = (z.shape[1] - a_width) // 5 // B_HEAD
            y_a = mixer_s5(z, n_batch, n_tok, n_ctx, s5_lambda_re[j], s5_lambda_im[j], s5_log_dt[j],
                           s5_b_re[j], s5_b_im[j], s5_c_re[j], s5_c_im[j], s5_d[j], s5_w_glu[j])
            o_f, o_b = mixer_gla(z, n_batch, n_tok, n_ctx, a_width, lb_all[j], b_heads, B_HEAD)
            gcb = (a_width + 4 * b_heads * B_HEAD) // (b_heads * B_HEAD)
            y_b = post_norm_gate(o_f, o_b, z, gcb, B_HEAD, jnp.ones((B_HEAD,), F32))
            m1, m2 = y_a, y_b
        else:
            cd_cols = w_in_cd.shape[2]
            cd_pad = -(-cd_cols // 1024) * 1024
            w_cd = jnp.pad(w_in_cd[j].astype(BF16), ((0, 0), (0, cd_pad - cd_cols)))
            z = matmul(h, w_cd, F32, 512, 1024, "in_proj")
            c_heads = ret_decay_logit.shape[2]
            d_heads = gdn_a_log.shape[2]
            o_f, o_b = mixer_retention(z, n_batch, n_tok, n_ctx, ret_decay_logit[j], c_heads)
            y_c = post_norm_gate(o_f, o_b, z, (2 * c_heads * C_QK + c_heads * C_V) // (c_heads * C_V), C_V,
                                 jnp.ones((C_V,), F32))
            col0 = 2 * c_heads * C_QK + 2 * c_heads * C_V
            o_f, o_b = mixer_gdn(z, n_batch, n_tok, n_ctx, col0, gdn_conv_w[j], gdn_a_log[j], gdn_dt_bias[j],
                                 d_heads, D_HEAD)
            y_d = post_norm_gate(o_f, o_b, z, (col0 + 3 * d_heads * D_HEAD) // (d_heads * D_HEAD), D_HEAD,
                                 gdn_norm_g[j])
            m1, m2 = y_c, y_d
        xs = out_proj(m1, m2, w_out[i].astype(BF16), xs, mod, 2, tiles_per_row, n_batch)
        h2, h2_packed = norm_mod(xs, norm2_g[i], mod, 3, 4, tiles_per_row, n_batch, packed=True)
        xs = moe_block(h2, h2_packed, xs, mod, 5, tiles_per_row, n_batch, w_router[i], router_bias[i],
                       w_exp_gate[i], w_exp_up[i], w_exp_down[i], w_sh_gate[i], w_sh_up[i], w_sh_down[i])
    out = final_norm(xs, final_norm_g, n_batch, tiles_per_row, ctx_tiles)
    return out.reshape(n_batch, n_lat, d)
```

```python
import functools

import numpy as np
import jax
import jax.numpy as jnp
from jax import lax
from jax.experimental import pallas as pl
from jax.experimental.pallas import tpu as pltpu

F32 = jnp.float32
BF16 = jnp.bfloat16

EPS = 1e-6
GRID_W = 64
CHUNK = 64
S5_L = 32
S5_H = 16
S5_P = 64
B_HEAD = 128
C_QK = 128
C_V = 256
D_HEAD = 128
ROPE_BASE = 10000.0
N_EXPERTS = 64
N_GROUPS = 8
TOPK_GROUPS = 4
TOP_K = 8
ROUTED_SCALE = 2.5
MOE_BLOCK = 256
ROW_TILE = 256
VMEM_LIMIT_V7X = 56 * 1024 * 1024
EXP_CLAMP = 80.0
GLA_SUB = 16


def _cparams(*sem):
    return pltpu.CompilerParams(dimension_semantics=sem, vmem_limit_bytes=VMEM_LIMIT_V7X)


def _silu(x):
    return x * jax.nn.sigmoid(x)


def _dot(a, b):
    return jnp.dot(a.astype(BF16), b.astype(BF16), preferred_element_type=F32)


def _dot_nt(a, b):
    return lax.dot_general(a.astype(BF16), b.astype(BF16), (((1,), (1,)), ((), ())), preferred_element_type=F32)


def _dot_tn(a, b):
    return lax.dot_general(a.astype(BF16), b.astype(BF16), (((0,), (0,)), ((), ())), preferred_element_type=F32)


def _split3(x):
    x1 = x.astype(BF16)
    r = x - x1.astype(F32)
    x2 = r.astype(BF16)
    x3 = (r - x2.astype(F32)).astype(BF16)
    return x1, x2, x3


def _tri_dot(tri, x):
    return sum(jnp.dot(tri, p, preferred_element_type=F32) for p in _split3(x))


def _dot_tri(x, tri):
    return sum(jnp.dot(p, tri, preferred_element_type=F32) for p in _split3(x))


def _tri_mask(n, lower):
    r = lax.broadcasted_iota(jnp.int32, (n, n), 0)
    c = lax.broadcasted_iota(jnp.int32, (n, n), 1)
    return (r >= c) if lower else (r <= c)


def _seg_of_tile(i, tiles_per_row, n_batch):
    return jnp.where(i % tiles_per_row == 0, n_batch, i // tiles_per_row)


def _pack_halves(y):
    half = y.shape[1] // 2
    lo = lax.bitcast_convert_type(y[:, :half].astype(BF16).astype(F32), jnp.uint32) >> 16
    hi = lax.bitcast_convert_type(y[:, half:].astype(BF16).astype(F32), jnp.uint32) & jnp.uint32(0xFFFF0000)
    return hi | lo


def _unpack_halves(w):
    lo = lax.bitcast_convert_type(w << 16, F32)
    hi = lax.bitcast_convert_type(w & jnp.uint32(0xFFFF0000), F32)
    return lo, hi


LANES = 128


def _store_rows(ref, packed):
    pieces = jnp.stack([packed[:, s * LANES:(s + 1) * LANES] for s in range(ref.shape[1])], axis=0)
    ref[...] = pltpu.einshape("stl->tsl", pieces)


def _load_rows(ref, row0, rows):
    x = pltpu.einshape("tsl->stl", ref[row0:row0 + rows])
    return jnp.concatenate([x[s] for s in range(ref.shape[1])], axis=1)


def _norm_mod_kernel(x_ref, g_ref, shift_ref, scale_ref, o_ref, *packed_ref):
    x = x_ref[...]
    y = x * lax.rsqrt(jnp.mean(x * x, axis=-1, keepdims=True) + EPS) * g_ref[...]
    y = y * (1.0 + scale_ref[...]) + shift_ref[...]
    o_ref[...] = y.astype(o_ref.dtype)
    if packed_ref:
        _store_rows(packed_ref[0], _pack_halves(y))


def norm_mod(xs, g, mod, k_shift, k_scale, tiles_per_row, n_batch, packed=False):
    t, d = xs.shape
    seg = functools.partial(_seg_of_tile, tiles_per_row=tiles_per_row, n_batch=n_batch)
    out_specs = [pl.BlockSpec((ROW_TILE, d), lambda i: (i, 0))]
    out_shape = [jax.ShapeDtypeStruct((t, d), BF16)]
    if packed:
        out_specs.append(pl.BlockSpec((ROW_TILE, d // 2 // LANES, LANES), lambda i: (i, 0, 0)))
        out_shape.append(jax.ShapeDtypeStruct((t, d // 2 // LANES, LANES), jnp.uint32))
    res = pl.pallas_call(
        _norm_mod_kernel,
        grid=(t // ROW_TILE,),
        in_specs=[
            pl.BlockSpec((ROW_TILE, d), lambda i: (i, 0)),
            pl.BlockSpec((1, d), lambda i: (0, 0)),
            pl.BlockSpec((None, None, 1, d), lambda i: (seg(i), k_shift, 0, 0)),
            pl.BlockSpec((None, None, 1, d), lambda i: (seg(i), k_scale, 0, 0)),
        ],
        out_specs=out_specs,
        out_shape=out_shape,
        compiler_params=_cparams("parallel"),
        name="norm_mod",
    )(xs, g.reshape(1, d), mod, mod)
    return res if packed else res[0]


def _final_norm_kernel(x_ref, g_ref, o_ref):
    x = x_ref[...]
    o_ref[...] = x * lax.rsqrt(jnp.mean(x * x, axis=-1, keepdims=True) + EPS) * g_ref[...]


def final_norm(xs, g, n_batch, tiles_per_row, ctx_tiles):
    t, d = xs.shape
    lat_tiles = tiles_per_row - ctx_tiles
    return pl.pallas_call(
        _final_norm_kernel,
        grid=(n_batch, lat_tiles),
        in_specs=[
            pl.BlockSpec((ROW_TILE, d), lambda b, i: (b * tiles_per_row + ctx_tiles + i, 0)),
            pl.BlockSpec((1, d), lambda b, i: (0, 0)),
        ],
        out_specs=pl.BlockSpec((ROW_TILE, d), lambda b, i: (b * lat_tiles + i, 0)),
        out_shape=jax.ShapeDtypeStruct((n_batch * lat_tiles * ROW_TILE, d), F32),
        compiler_params=_cparams("parallel", "parallel"),
        name="final_norm",
    )(xs, g.reshape(1, d))


def _mm_kernel(a_ref, w_ref, o_ref):
    o_ref[...] = jnp.dot(a_ref[...].astype(BF16), w_ref[...].astype(BF16),
                         preferred_element_type=F32).astype(o_ref.dtype)


def matmul(a, w, out_dtype, tm, tn, name):
    m, k = a.shape
    n = w.shape[1]
    return pl.pallas_call(
        _mm_kernel,
        grid=(n // tn, m // tm),
        in_specs=[pl.BlockSpec((tm, k), lambda j, i: (i, 0)),
                  pl.BlockSpec((k, tn), lambda j, i: (0, j))],
        out_specs=pl.BlockSpec((tm, tn), lambda j, i: (i, j)),
        out_shape=jax.ShapeDtypeStruct((m, n), out_dtype),
        compiler_params=_cparams("parallel", "parallel"),
        name=name,
    )(a, w)


def _out_proj_kernel(a1_ref, a2_ref, w1_ref, w2_ref, res_ref, gate_ref, o_ref):
    y = jnp.dot(a1_ref[...], w1_ref[...], preferred_element_type=F32)
    y += jnp.dot(a2_ref[...], w2_ref[...], preferred_element_type=F32)
    o_ref[...] = res_ref[...] + gate_ref[...] * y


def out_proj(a1, a2, w, xs, mod, k_gate, tiles_per_row, n_batch, tn=1024):
    t, d = xs.shape
    k1, k2 = a1.shape[1], a2.shape[1]
    assert k1 == k2
    seg = functools.partial(_seg_of_tile, tiles_per_row=tiles_per_row, n_batch=n_batch)
    return pl.pallas_call(
        _out_proj_kernel,
        grid=(d // tn, t // ROW_TILE),
        in_specs=[
            pl.BlockSpec((ROW_TILE, k1), lambda j, i: (i, 0)),
            pl.BlockSpec((ROW_TILE, k2), lambda j, i: (i, 0)),
            pl.BlockSpec((k1, tn), lambda j, i: (0, j)),
            pl.BlockSpec((k2, tn), lambda j, i: (1, j)),
            pl.BlockSpec((ROW_TILE, tn), lambda j, i: (i, j)),
            pl.BlockSpec((None, None, 1, tn), lambda j, i: (seg(i), k_gate, 0, j)),
        ],
        out_specs=pl.BlockSpec((ROW_TILE, tn), lambda j, i: (i, j)),
        out_shape=jax.ShapeDtypeStruct((t, d), F32),
        compiler_params=_cparams("parallel", "parallel"),
        name="out_proj",
    )(a1, a2, w, w, xs, mod)


def _post_kernel(of_ref, ob_ref, gate_ref, ng_ref, o_ref, *, head_dim):
    o = of_ref[...].astype(F32) + ob_ref[...].astype(F32)
    g = gate_ref[...].astype(F32)
    width = o.shape[1]
    for h in range(width // head_dim):
        sl = slice(h * head_dim, (h + 1) * head_dim)
        oh = o[:, sl]
        y = oh * lax.rsqrt(jnp.mean(oh * oh, axis=-1, keepdims=True) + EPS) * ng_ref[...]
        o_ref[:, sl] = (y * _silu(g[:, sl])).astype(o_ref.dtype)


def post_norm_gate(o_f, o_b, z, gate_col_block, head_dim, norm_g):
    t, width = o_f.shape
    return pl.pallas_call(
        functools.partial(_post_kernel, head_dim=head_dim),
        grid=(t // ROW_TILE,),
        in_specs=[
            pl.BlockSpec((ROW_TILE, width), lambda i: (i, 0)),
            pl.BlockSpec((ROW_TILE, width), lambda i: (i, 0)),
            pl.BlockSpec((ROW_TILE, width), lambda i: (i, gate_col_block)),
            pl.BlockSpec((1, head_dim), lambda i: (0, 0)),
        ],
        out_specs=pl.BlockSpec((ROW_TILE, width), lambda i: (i, 0)),
        out_shape=jax.ShapeDtypeStruct((t, width), BF16),
        compiler_params=_cparams("parallel"),
        name="post_norm_gate",
    )(o_f, o_b, z, norm_g.reshape(1, head_dim).astype(F32))


def _bwd_chunk(j, ctx_chunks, n_chunks):
    return jnp.where(j < ctx_chunks, ctx_chunks - 1 - j, n_chunks + ctx_chunks - 1 - j)


def _s5_direction_tables(lam_re, lam_im, log_dt, b_re, b_im, c_re, c_im, reverse):
    hi = lax.Precision.HIGHEST
    ln = S5_L
    lam_re, lam_im, b_re, b_im, c_re, c_im = (p.astype(F32) for p in (lam_re, lam_im, b_re, b_im, c_re, c_im))
    dt = jnp.exp(log_dt.astype(F32))[:, None]
    mag = jnp.exp(lam_re * dt)
    ab_re, ab_im = mag * jnp.cos(lam_im * dt), mag * jnp.sin(lam_im * dt)
    den = lam_re * lam_re + lam_im * lam_im
    fr = ((ab_re - 1) * lam_re + ab_im * lam_im) / den
    fi = (ab_im * lam_re - (ab_re - 1) * lam_im) / den
    bb_re = fr[..., None] * b_re - fi[..., None] * b_im
    bb_im = fr[..., None] * b_im + fi[..., None] * b_re
    tau = jnp.arange(ln + 1, dtype=F32)[:, None, None]
    pw = jnp.exp(tau * (lam_re * dt))
    pr, pi = pw * jnp.cos(tau * (lam_im * dt)), pw * jnp.sin(tau * (lam_im * dt))
    abr = pr[..., None] * bb_re - pi[..., None] * bb_im
    abi = pr[..., None] * bb_im + pi[..., None] * bb_re
    kern = (jnp.einsum('ghp,tgpk->tghk', c_re, abr[:ln], precision=hi)
            - jnp.einsum('ghp,tgpk->tghk', c_im, abi[:ln], precision=hi))
    pos = jnp.arange(ln)
    lag = (pos[None, :] - pos[:, None]) if not reverse else (pos[:, None] - pos[None, :])
    toe = jnp.where((lag >= 0)[:, :, None, None, None], kern[jnp.clip(lag, 0, ln - 1)], 0.0)
    g = lam_re.shape[0]
    intra = jnp.transpose(toe, (2, 0, 4, 1, 3)).reshape(g, ln * S5_H, ln * S5_H)
    pw_in = (ln - 1 - pos) if not reverse else pos
    inj = jnp.concatenate([abr[pw_in], abi[pw_in]], axis=2)
    inject = jnp.transpose(inj, (1, 0, 3, 2)).reshape(g, ln * S5_H, 2 * S5_P)
    pw_out = (pos + 1) if not reverse else (ln - pos)
    w_re = c_re[None] * pr[pw_out][:, :, None, :] - c_im[None] * pi[pw_out][:, :, None, :]
    w_im = -(c_re[None] * pi[pw_out][:, :, None, :] + c_im[None] * pr[pw_out][:, :, None, :])
    readout = jnp.transpose(jnp.concatenate([w_re, w_im], axis=3), (1, 3, 0, 2)).reshape(g, 2 * S5_P, ln * S5_H)
    decay = jnp.stack([pr[ln], pi[ln]])
    return intra, inject, readout, decay


def _s5_in_kernel(u_ref, w_ref, yi_ref, s_ref):
    r = jnp.dot(u_ref[...], w_ref[...], preferred_element_type=F32)
    n_intra = yi_ref.shape[-1]
    yi_ref[...] = r[:, :n_intra]
    s_ref[...] = r[:, n_intra:]


def _s5_scan_kernel(a_ref, sf_ref, sb_ref, of_ref, ob_ref, st_ref):
    @pl.when(pl.program_id(1) == 0)
    def _():
        st_ref[...] = jnp.zeros_like(st_ref)

    steps = sf_ref.shape[0]
    for d, (s_ref, o_ref) in enumerate(((sf_ref, of_ref), (sb_ref, ob_ref))):
        ar, ai = a_ref[2 * d], a_ref[2 * d + 1]
        sr, si = st_ref[2 * d], st_ref[2 * d + 1]
        for q in range(steps):
            r = q if d == 0 else steps - 1 - q
            o_ref[r, 0] = sr
            o_ref[r, 1] = si
            sr, si = (ar * sr - ai * si + s_ref[r, 0], ar * si + ai * sr + s_ref[r, 1])
        st_ref[2 * d] = sr
        st_ref[2 * d + 1] = si


def _s5_out_kernel(yi_ref, st_ref, w_ref, u_ref, d_ref, o_ref):
    y = yi_ref[...] + jnp.dot(st_ref[...].astype(BF16), w_ref[...], preferred_element_type=F32)
    o_ref[...] = (y + d_ref[...] * u_ref[...].astype(F32)).astype(o_ref.dtype)


def _glu_kernel(y_ref, w_ref, o_ref):
    y = jax.nn.gelu(y_ref[...].astype(F32))
    o_ref[...] = (y * jax.nn.sigmoid(jnp.dot(y.astype(BF16), w_ref[...], preferred_element_type=F32))
                  ).astype(o_ref.dtype)


def mixer_s5(z, n_batch, n_tok, n_ctx, lam_re, lam_im, log_dt, b_re, b_im, c_re, c_im, d_skip, w_glu):
    t = z.shape[0]
    g = lam_re.shape[1]
    width = g * S5_H
    lh = S5_L * S5_H
    rows = t // S5_L
    tabs = [_s5_direction_tables(lam_re[d], lam_im[d], log_dt[d], b_re[d], b_im[d], c_re[d], c_im[d], d == 1)
            for d in range(2)]
    w_in = jnp.concatenate([tabs[0][0] + tabs[1][0], tabs[0][1], tabs[1][1]], axis=2).astype(BF16)
    w_st = jnp.concatenate([tabs[0][2], tabs[1][2]], axis=1).astype(BF16)
    decay = jnp.concatenate([tabs[0][3], tabs[1][3]], axis=0).reshape(4, g * S5_P // 128, 128)
    u = z[:, :width].reshape(rows, S5_L, g, S5_H)
    u_g = jnp.transpose(u, (2, 0, 1, 3)).reshape(g, rows, lh).astype(BF16)

    n_st = 4 * S5_P
    yi, s_in = pl.pallas_call(
        _s5_in_kernel,
        grid=(g,),
        in_specs=[pl.BlockSpec((None, rows, lh), lambda i: (i, 0, 0)),
                  pl.BlockSpec((None, lh, lh + n_st), lambda i: (i, 0, 0))],
        out_specs=[pl.BlockSpec((None, rows, lh), lambda i: (i, 0, 0)),
                   pl.BlockSpec((None, rows, n_st), lambda i: (i, 0, 0))],
        out_shape=[jax.ShapeDtypeStruct((g, rows, lh), F32), jax.ShapeDtypeStruct((g, rows, n_st), F32)],
        compiler_params=_cparams("parallel"),
        name="s5_in",
    )(u_g, w_in)

    gp = g * S5_P // 128
    s_scan = jnp.transpose(s_in.reshape(g, rows, 4, S5_P), (1, 2, 0, 3)).reshape(rows, 4, gp, 128)
    blocks_per_row = n_tok // S5_L
    cb = n_ctx // S5_L
    assert blocks_per_row % cb == 0
    nblk = blocks_per_row // cb
    bwd = lambda j: jnp.where(j == 0, 0, nblk - j)
    st_f, st_b = pl.pallas_call(
        _s5_scan_kernel,
        grid=(n_batch, nblk),
        in_specs=[pl.BlockSpec((4, gp, 128), lambda b, j: (0, 0, 0)),
                  pl.BlockSpec((cb, 2, gp, 128), lambda b, j: (b * nblk + j, 0, 0, 0)),
                  pl.BlockSpec((cb, 2, gp, 128), lambda b, j: (b * nblk + bwd(j), 1, 0, 0))],
        out_specs=[pl.BlockSpec((cb, 2, gp, 128), lambda b, j: (b * nblk + j, 0, 0, 0)),
                   pl.BlockSpec((cb, 2, gp, 128), lambda b, j: (b * nblk + bwd(j), 0, 0, 0))],
        out_shape=[jax.ShapeDtypeStruct((rows, 2, gp, 128), F32)] * 2,
        scratch_shapes=[pltpu.VMEM((4, gp, 128), F32)],
        compiler_params=_cparams("parallel", "arbitrary"),
        name="s5_scan",
    )(decay, s_scan, s_scan)
    st = jnp.concatenate([st_f, st_b], axis=1).reshape(rows, 4, g, S5_P)
    st_g = jnp.transpose(st, (2, 0, 1, 3)).reshape(g, rows, n_st)

    d_vec = jnp.tile(d_skip.astype(F32).reshape(g, 1, S5_H), (1, S5_L, 1)).reshape(g, 1, lh)
    y_g = pl.pallas_call(
        _s5_out_kernel,
        grid=(g,),
        in_specs=[pl.BlockSpec((None, rows, lh), lambda i: (i, 0, 0)),
                  pl.BlockSpec((None, rows, n_st), lambda i: (i, 0, 0)),
                  pl.BlockSpec((None, n_st, lh), lambda i: (i, 0, 0)),
                  pl.BlockSpec((None, rows, lh), lambda i: (i, 0, 0)),
                  pl.BlockSpec((None, 1, lh), lambda i: (i, 0, 0))],
        out_specs=pl.BlockSpec((None, rows, lh), lambda i: (i, 0, 0)),
        out_shape=jax.ShapeDtypeStruct((g, rows, lh), BF16),
        compiler_params=_cparams("parallel"),
        name="s5_out",
    )(yi, st_g, w_st, u_g, d_vec)
    y = jnp.transpose(y_g.reshape(g, rows, S5_L, S5_H), (1, 2, 0, 3)).reshape(t, width)

    return pl.pallas_call(
        _glu_kernel,
        grid=(t // ROW_TILE,),
        in_specs=[pl.BlockSpec((ROW_TILE, width), lambda i: (i, 0)),
                  pl.BlockSpec((width, width), lambda i: (0, 0))],
        out_specs=pl.BlockSpec((ROW_TILE, width), lambda i: (i, 0)),
        out_shape=jax.ShapeDtypeStruct((t, width), BF16),
        compiler_params=_cparams("parallel"),
        name="s5_glu",
    )(y, w_glu.astype(BF16))


def _gla_kernel(lb_ref, qf_ref, ff_ref, vf_ref, qb_ref, fb_ref, vb_ref, of_ref, ob_ref, st_ref, *, n_heads, hd):
    @pl.when(pl.program_id(1) == 0)
    def _():
        st_ref[...] = jnp.zeros_like(st_ref)

    c = qf_ref.shape[0]
    items = []
    for d, (q_ref, f_ref, v_ref, o_ref) in enumerate(((qf_ref, ff_ref, vf_ref, of_ref),
                                                       (qb_ref, fb_ref, vb_ref, ob_ref))):
        lower = d == 0
        mask = _tri_mask(c, lower)
        tri = mask.astype(BF16)
        lb = lb_ref[d]
        sig = jax.nn.sigmoid(f_ref[...])
        kk = (1.0 - lb) * (1.0 - sig)
        logf = jnp.log(lb + (1.0 - lb) * sig)
        b = _tri_dot(tri, logf)
        b_end = b[c - 1:c, :] if lower else b[0:1, :]
        qs = _silu(q_ref[...])
        qt = (qs * jnp.exp(b)).astype(BF16)
        q_blk, k_blk = [], []
        for blk in range(c // GLA_SUB):
            r0, r1 = blk * GLA_SUB, (blk + 1) * GLA_SUB
            if lower:
                beta = b[r0 - 1:r0, :] if blk > 0 else 0.0
            else:
                beta = b[r1:r1 + 1, :] if r1 < c else 0.0
            q_blk.append((qs[r0:r1, :] * jnp.exp(b[r0:r1, :] - beta)).astype(BF16))
            k_blk.append((kk * jnp.exp(jnp.minimum(beta - b, EXP_CLAMP))).astype(BF16))
        kend = (kk * jnp.exp(b_end - b)).astype(BF16)
        dec = jnp.exp(b_end)
        v = v_ref[...].astype(BF16)
        for h in range(n_heads):
            sl = slice(h * hd, (h + 1) * hd)
            items.append((d, h, sl, o_ref, mask, qt[:, sl], [(qb[:, sl], kb[:, sl]) for qb, kb in zip(q_blk, k_blk)],
                          kend[:, sl], v[:, sl], dec[:, sl], st_ref[d, h]))
    att = [jnp.concatenate([_dot_nt(qb, kb) for qb, kb in it[6]], axis=0) for it in items]
    inter = [_dot_nt(it[5], it[10]) for it in items]
    upd = [_dot_tn(it[8], it[7]) for it in items]
    intra = [_dot(jnp.where(it[4], a, 0.0), it[8]) for it, a in zip(items, att)]
    for it, oi, os, up in zip(items, intra, inter, upd):
        d, h, sl, o_ref = it[:4]
        o_ref[:, sl] = (oi + os).astype(o_ref.dtype)
        st_ref[d, h] = it[9] * it[10] + up


def mixer_gla(z, n_batch, n_tok, n_ctx, col0, lb, n_heads, hd):
    width = n_heads * hd
    z3 = z.reshape(n_batch, n_tok, z.shape[1])
    nc, cc = n_tok // CHUNK, n_ctx // CHUNK
    cb0 = col0 // width
    bwd = functools.partial(_bwd_chunk, ctx_chunks=cc, n_chunks=nc)
    blk = (None, CHUNK, width)
    fw = lambda k: pl.BlockSpec(blk, lambda b, j: (b, j, cb0 + k))
    bw = lambda k: pl.BlockSpec(blk, lambda b, j: (b, bwd(j), cb0 + k))
    o_f, o_b = pl.pallas_call(
        functools.partial(_gla_kernel, n_heads=n_heads, hd=hd),
        grid=(n_batch, nc),
        in_specs=[pl.BlockSpec((2, 1, width), lambda b, j: (0, 0, 0)),
                  fw(0), fw(1), fw(3), bw(0), bw(2), bw(3)],
        out_specs=[pl.BlockSpec(blk, lambda b, j: (b, j, 0)),
                   pl.BlockSpec(blk, lambda b, j: (b, bwd(j), 0))],
        out_shape=[jax.ShapeDtypeStruct((n_batch, n_tok, width), BF16)] * 2,
        scratch_shapes=[pltpu.VMEM((2, n_heads, hd, hd), F32)],
        compiler_params=_cparams("parallel", "arbitrary"),
        name="gla_scan",
    )(lb.reshape(2, 1, width).astype(F32), z3, z3, z3, z3, z3, z3)
    return o_f.reshape(-1, width), o_b.reshape(-1, width)


def _rotary_tables(n_tok, n_ctx):
    n_lat = n_tok - n_ctx
    rows = n_lat // GRID_W
    row = jnp.repeat(jnp.arange(rows, dtype=F32), GRID_W)
    col = jnp.tile(jnp.arange(GRID_W, dtype=F32), rows)
    n_freq = C_QK // 4
    inv = ROPE_BASE ** (-jnp.arange(n_freq, dtype=F32) / n_freq)
    ang = jnp.concatenate([row[:, None] * inv, col[:, None] * inv], axis=-1)
    ang = jnp.concatenate([jnp.zeros((n_ctx, C_QK // 2), F32), ang], axis=0)
    cos, sin = jnp.cos(ang), jnp.sin(ang)
    return jnp.concatenate([cos, cos], axis=-1), jnp.concatenate([-sin, sin], axis=-1)


def _ret_kernel(cdec_ref, dmat_ref, qdec_ref, kdec_ref,
                qf_ref, kf_ref, vf_ref, cf_ref, sf_ref, qb_ref, kb_ref, vb_ref, cb_ref, sb_ref,
                of_ref, ob_ref, st_ref, *, n_heads):
    @pl.when(pl.program_id(1) == 0)
    def _():
        st_ref[...] = jnp.zeros_like(st_ref)

    half = C_QK // 2
    items = []
    for d, (q_ref, k_ref, v_ref, cos_ref, sin_ref, o_ref) in enumerate(
            ((qf_ref, kf_ref, vf_ref, cf_ref, sf_ref, of_ref), (qb_ref, kb_ref, vb_ref, cb_ref, sb_ref, ob_ref))):
        cos, sin = cos_ref[...], sin_ref[...]
        for h in range(n_heads):
            qs = slice(h * C_QK, (h + 1) * C_QK)
            vs = slice(h * C_V, (h + 1) * C_V)
            qh, kh = q_ref[:, qs].astype(F32), k_ref[:, qs].astype(F32)
            qh = (qh * cos + pltpu.roll(qh, half, axis=1) * sin) * (C_QK ** -0.5)
            kh = kh * cos + pltpu.roll(kh, half, axis=1) * sin
            items.append((d, h, vs, o_ref, qh.astype(BF16), kh.astype(BF16), (qh * qdec_ref[d, h]).astype(BF16),
                          (kh * kdec_ref[d, h]).astype(BF16), v_ref[:, vs].astype(BF16), st_ref[d, h]))
    att = [_dot_nt(it[4], it[5]) for it in items]
    inter = [_dot(it[6], it[9]) for it in items]
    upd = [_dot_tn(it[7], it[8]) for it in items]
    intra = [_dot(a * dmat_ref[it[0], it[1]], it[8]) for it, a in zip(items, att)]
    for it, oi, os, up in zip(items, intra, inter, upd):
        d, h, vs, o_ref = it[:4]
        o_ref[:, vs] = (oi + os).astype(o_ref.dtype)
        st_ref[d, h] = cdec_ref[d, h] * it[9] + up


def mixer_retention(z, n_batch, n_tok, n_ctx, decay_logit, n_heads):
    qw, vw = n_heads * C_QK, n_heads * C_V
    z3 = z.reshape(n_batch, n_tok, z.shape[1])
    nc, cc = n_tok // CHUNK, n_ctx // CHUNK
    bwd = functools.partial(_bwd_chunk, ctx_chunks=cc, n_chunks=nc)
    log_gamma = jax.nn.log_sigmoid(decay_logit.astype(F32))[:, :, None, None]
    idx = jnp.arange(CHUNK, dtype=F32)
    diff = idx[:, None] - idx[None, :]
    dmat_f = jnp.where(diff >= 0, jnp.exp(jnp.maximum(diff, 0.0) * log_gamma[0]), 0.0)
    dmat_b = jnp.where(diff <= 0, jnp.exp(jnp.maximum(-diff, 0.0) * log_gamma[1]), 0.0)
    dmat = jnp.stack([dmat_f, dmat_b])
    ones = jnp.ones((1, 1, 1, C_QK), F32)
    pos_f, pos_b = idx[None, None, :, None], (CHUNK - 1 - idx)[None, None, :, None]
    lg = log_gamma
    qdec = jnp.concatenate([jnp.exp((pos_f + 1) * lg[0:1]), jnp.exp((pos_b + 1) * lg[1:2])]) * ones
    kdec = jnp.concatenate([jnp.exp((CHUNK - 1 - pos_f) * lg[0:1]), jnp.exp((CHUNK - 1 - pos_b) * lg[1:2])]) * ones
    cdec = jnp.exp(CHUNK * log_gamma[:, :, 0, 0])
    cos2, sin2 = _rotary_tables(n_tok, n_ctx)
    full = lambda shape: pl.BlockSpec(shape, lambda b, j: (0,) * len(shape))
    tab = lambda order: pl.BlockSpec((CHUNK, C_QK), lambda b, j: (order(j), 0))
    ident = lambda j: j
    vcb = 2 * qw // vw
    assert vcb * vw == 2 * qw
    def specs(order):
        return [pl.BlockSpec((None, CHUNK, qw), lambda b, j: (b, order(j), 0)),
                pl.BlockSpec((None, CHUNK, qw), lambda b, j: (b, order(j), 1)),
                pl.BlockSpec((None, CHUNK, vw), lambda b, j: (b, order(j), vcb)),
                tab(order), tab(order)]
    o_f, o_b = pl.pallas_call(
        functools.partial(_ret_kernel, n_heads=n_heads),
        grid=(n_batch, nc),
        in_specs=[pl.BlockSpec(memory_space=pltpu.SMEM), full((2, n_heads, CHUNK, CHUNK)),
                  full((2, n_heads, CHUNK, C_QK)), full((2, n_heads, CHUNK, C_QK))] + specs(ident) + specs(bwd),
        out_specs=[pl.BlockSpec((None, CHUNK, vw), lambda b, j: (b, j, 0)),
                   pl.BlockSpec((None, CHUNK, vw), lambda b, j: (b, bwd(j), 0))],
        out_shape=[jax.ShapeDtypeStruct((n_batch, n_tok, vw), BF16)] * 2,
        scratch_shapes=[pltpu.VMEM((2, n_heads, C_QK, C_V), F32)],
        compiler_params=_cparams("parallel", "arbitrary"),
        name="retention_scan",
    )(cdec, dmat, qdec, kdec, z3, z3, z3, cos2, sin2, z3, z3, z3, cos2, sin2)
    return o_f.reshape(-1, vw), o_b.reshape(-1, vw)


def _gdn_conv_kernel(prev_ref, cur_ref, next_ref, w_ref, o_ref, *, tiles_per_row, ctx_tiles, n_heads, hd):
    i = pl.program_id(0)
    r = i % tiles_per_row
    first = jnp.logical_or(r == 0, r == ctx_tiles)
    last = jnp.logical_or(r == ctx_tiles - 1, r == tiles_per_row - 1)
    x = cur_ref[...].astype(F32)
    rows = x.shape[0]
    rid = lax.broadcasted_iota(jnp.int32, x.shape, 0)
    hp = prev_ref.shape[0]
    x_prev = jnp.where(first, 0.0, prev_ref[hp - 1:hp, :].astype(F32))
    x_next = jnp.where(last, 0.0, next_ref[0:1, :].astype(F32))
    left = jnp.where(rid == 0, x_prev, pltpu.roll(x, 1, axis=0))
    right = jnp.where(rid == rows - 1, x_next, pltpu.roll(x, rows - 1, axis=0))
    w = w_ref[...]
    y = _silu(left * w[0:1, :] + x * w[1:2, :] + right * w[2:3, :])
    width = n_heads * hd
    for h in range(3 * n_heads):
        sl = slice(h * hd, (h + 1) * hd)
        yh = y[:, sl]
        if h < 2 * n_heads:
            yh = yh * lax.rsqrt(jnp.sum(yh * yh, axis=-1, keepdims=True) + EPS)
            if h < n_heads:
                yh = yh * (hd ** -0.5)
        o_ref[:, sl] = yh.astype(o_ref.dtype)


def _gdn_kernel(qf_ref, kf_ref, vf_ref, cf_ref, rf_ref, qb_ref, kb_ref, vb_ref, cb_ref, rb_ref,
                of_ref, ob_ref, st_ref, *, n_heads, hd):
    @pl.when(pl.program_id(1) == 0)
    def _():
        st_ref[...] = jnp.zeros_like(st_ref)

    c = qf_ref.shape[0]
    eye = (lax.broadcasted_iota(jnp.int32, (c, c), 0) == lax.broadcasted_iota(jnp.int32, (c, c), 1)).astype(F32)
    items = []
    for d, (q_ref, k_ref, v_ref, col_ref, row_ref, o_ref) in enumerate(
            ((qf_ref, kf_ref, vf_ref, cf_ref, rf_ref, of_ref), (qb_ref, kb_ref, vb_ref, cb_ref, rb_ref, ob_ref))):
        lower = d == 0
        incl = _tri_mask(c, lower)
        strict = jnp.logical_and(incl, eye == 0.0)
        col = col_ref[...]
        row = row_ref[...]
        b_cols = _tri_dot(incl.astype(BF16), col)
        b_rows = _dot_tri(row, _tri_mask(c, not lower).astype(BF16))
        for h in range(n_heads):
            sl = slice(h * hd, (h + 1) * hd)
            ib, ig = d * n_heads + h, (2 + d) * n_heads + h
            be = col[:, ib:ib + 1]
            bc = b_cols[:, ig:ig + 1]
            br = b_rows[ig:ig + 1, :]
            b_end = bc[c - 1:c, :] if lower else bc[0:1, :]
            gam_i = jnp.exp(jnp.where(incl, bc - br, -1e30))
            gam_s = jnp.where(strict, gam_i, 0.0)
            qh, kh, vh = q_ref[:, sl], k_ref[:, sl], v_ref[:, sl]
            eb = jnp.exp(bc)
            rhs = jnp.concatenate([be * vh.astype(F32), (be * eb) * kh.astype(F32)], axis=1).astype(BF16)
            items.append(dict(d=d, h=h, sl=sl, o_ref=o_ref, be=be, gam_i=gam_i, gam_s=gam_s, q=qh, k=kh, rhs=rhs,
                              qe=(qh.astype(F32) * eb).astype(BF16), dec=jnp.exp(b_end),
                              ke=(kh.astype(F32) * jnp.exp(b_end - bc)).astype(BF16), s=st_ref[d, h]))
    kk = [_dot_nt(it["k"], it["k"]) for it in items]
    qk = [_dot_nt(it["q"], it["k"]) for it in items]
    qs = [_dot(it["qe"], it["s"]) for it in items]
    ns = [it["be"] * a * it["gam_s"] for it, a in zip(items, kk)]
    invs = [eye - n for n in ns]
    ps = [_dot(n, n) for n in ns]
    levels = int(np.log2(c)) - 1
    for lvl in range(levels):
        prods = [_dot(inv, p) for inv, p in zip(invs, ps)]
        if lvl < levels - 1:
            ps = [_dot(p, p) for p in ps]
        invs = [inv + pr for inv, pr in zip(invs, prods)]
    sols = [_dot(inv, it["rhs"]) for inv, it in zip(invs, items)]
    sks = [_dot(sol[:, hd:], it["s"]) for sol, it in zip(sols, items)]
    us = [sol[:, :hd] - sk for sol, sk in zip(sols, sks)]
    intra = [_dot(a * it["gam_i"], u) for a, it, u in zip(qk, items, us)]
    upd = [_dot_tn(it["ke"], u) for it, u in zip(items, us)]
    for it, oi, os, up in zip(items, intra, qs, upd):
        it["o_ref"][:, it["sl"]] = (oi + os).astype(it["o_ref"].dtype)
        st_ref[it["d"], it["h"]] = it["dec"] * it["s"] + up


def mixer_gdn(z, n_batch, n_tok, n_ctx, col0, conv_w, a_log, dt_bias, n_heads, hd):
    t, zw = z.shape
    width = n_heads * hd
    tiles_per_row, ctx_tiles = n_tok // ROW_TILE, n_ctx // ROW_TILE
    cb0 = col0 // (3 * width)
    assert cb0 * 3 * width == col0
    halo = 8
    hpt = ROW_TILE // halo
    n_halo = t // halo
    qkv = pl.pallas_call(
        functools.partial(_gdn_conv_kernel, tiles_per_row=tiles_per_row, ctx_tiles=ctx_tiles, n_heads=n_heads, hd=hd),
        grid=(t // ROW_TILE,),
        in_specs=[pl.BlockSpec((halo, 3 * width), lambda i: (jnp.maximum(i * hpt - 1, 0), cb0)),
                  pl.BlockSpec((ROW_TILE, 3 * width), lambda i: (i, cb0)),
                  pl.BlockSpec((halo, 3 * width), lambda i: (jnp.minimum((i + 1) * hpt, n_halo - 1), cb0)),
                  pl.BlockSpec((3, 3 * width), lambda i: (0, 0))],
        out_specs=pl.BlockSpec((ROW_TILE, 3 * width), lambda i: (i, 0)),
        out_shape=jax.ShapeDtypeStruct((t, 3 * width), BF16),
        compiler_params=_cparams("parallel"),
        name="gdn_conv",
    )(z, z, z, conv_w.astype(F32))

    sc0 = col0 + 4 * width
    small = z[:, sc0:sc0 + 4 * n_heads].astype(F32)
    a_log, dt_bias = a_log.astype(F32), dt_bias.astype(F32)
    be = jax.nn.sigmoid(small[:, :2 * n_heads])
    la_f = -jnp.exp(a_log[0]) * jax.nn.softplus(small[:, 2 * n_heads:3 * n_heads] + dt_bias[0])
    la_b = -jnp.exp(a_log[1]) * jax.nn.softplus(small[:, 3 * n_heads:] + dt_bias[1])
    cols = jnp.concatenate([be, la_f, la_b], axis=1)
    nc, cc = n_tok // CHUNK, n_ctx // CHUNK
    cols3 = cols.reshape(n_batch, n_tok, 4 * n_heads)
    rows4 = jnp.transpose(cols.reshape(n_batch, nc, CHUNK, 4 * n_heads), (0, 1, 3, 2))
    qkv3 = qkv.reshape(n_batch, n_tok, 3 * width)
    bwd = functools.partial(_bwd_chunk, ctx_chunks=cc, n_chunks=nc)
    ident = lambda j: j
    def specs(order):
        return [pl.BlockSpec((None, CHUNK, width), lambda b, j: (b, order(j), 0)),
                pl.BlockSpec((None, CHUNK, width), lambda b, j: (b, order(j), 1)),
                pl.BlockSpec((None, CHUNK, width), lambda b, j: (b, order(j), 2)),
                pl.BlockSpec((None, CHUNK, 4 * n_heads), lambda b, j: (b, order(j), 0)),
                pl.BlockSpec((None, None, 4 * n_heads, CHUNK), lambda b, j: (b, order(j), 0, 0))]
    o_f, o_b = pl.pallas_call(
        functools.partial(_gdn_kernel, n_heads=n_heads, hd=hd),
        grid=(n_batch, nc),
        in_specs=specs(ident) + specs(bwd),
        out_specs=[pl.BlockSpec((None, CHUNK, width), lambda b, j: (b, j, 0)),
                   pl.BlockSpec((None, CHUNK, width), lambda b, j: (b, bwd(j), 0))],
        out_shape=[jax.ShapeDtypeStruct((n_batch, n_tok, width), BF16)] * 2,
        scratch_shapes=[pltpu.VMEM((2, n_heads, hd, hd), F32)],
        compiler_params=_cparams("parallel", "arbitrary"),
        name="gdn_scan",
    )(qkv3, qkv3, qkv3, cols3, rows4, qkv3, qkv3, qkv3, cols3, rows4)
    return o_f.reshape(-1, width), o_b.reshape(-1, width)


def _first_max(x, idx, sentinel):
    m = jnp.max(x, axis=0, keepdims=True)
    return m, jnp.min(jnp.where(x == m, idx, sentinel), axis=0, keepdims=True)


def _router_kernel(h_ref, wt_ref, bias_ref, e_ref, g_ref, rank_ref, cnt_ref, carry_ref):
    @pl.when(pl.program_id(0) == 0)
    def _():
        carry_ref[...] = jnp.zeros_like(carry_ref)

    cols = h_ref.shape[0]
    gsz = N_EXPERTS // N_GROUPS
    neg = -jnp.inf
    logits = lax.dot_general(wt_ref[...], h_ref[...], (((1,), (1,)), ((), ())), preferred_element_type=F32)
    scores = jax.nn.sigmoid(logits)
    sel = scores + bias_ref[...]
    i_g = lax.broadcasted_iota(jnp.int32, (gsz, cols), 0)
    blocks, g_scores = [], []
    for g in range(N_GROUPS):
        blk = sel[g * gsz:(g + 1) * gsz, :]
        m1, first = _first_max(blk, i_g, gsz)
        m2 = jnp.max(jnp.where(i_g == first, neg, blk), axis=0, keepdims=True)
        blocks.append(blk)
        g_scores.append(m1 + m2)
    keep = [jnp.zeros((1, cols), jnp.bool_) for _ in range(N_GROUPS)]
    for _ in range(TOPK_GROUPS):
        best = functools.reduce(jnp.maximum, g_scores)
        found = jnp.zeros((1, cols), jnp.bool_)
        for g in range(N_GROUPS):
            pick = jnp.logical_and(g_scores[g] == best, jnp.logical_not(found))
            found = jnp.logical_or(found, pick)
            keep[g] = jnp.logical_or(keep[g], pick)
            g_scores[g] = jnp.where(pick, neg, g_scores[g])
    cur = jnp.concatenate([jnp.where(keep[g], blocks[g], neg) for g in range(N_GROUPS)], axis=0)
    i_e = lax.broadcasted_iota(jnp.int32, (N_EXPERTS, cols), 0)
    picks, gates = [], []
    for _ in range(TOP_K):
        _, idx = _first_max(cur, i_e, N_EXPERTS)
        pick = i_e == idx
        picks.append((idx, pick))
        gates.append(jnp.sum(jnp.where(pick, scores, 0.0), axis=0, keepdims=True))
        cur = jnp.where(pick, neg, cur)
    total = functools.reduce(jnp.add, gates)
    chosen = functools.reduce(jnp.logical_or, [p for _, p in picks]).astype(F32)
    r = lax.broadcasted_iota(jnp.int32, (cols, cols), 0)
    c = lax.broadcasted_iota(jnp.int32, (cols, cols), 1)
    before = jnp.dot(chosen.astype(BF16), (r < c).astype(BF16), preferred_element_type=F32) + carry_ref[...]
    for k, (idx, pick) in enumerate(picks):
        e_ref[k:k + 1, :] = idx
        g_ref[k:k + 1, :] = gates[k] / total * ROUTED_SCALE
        rank_ref[k:k + 1, :] = jnp.sum(jnp.where(pick, before, 0.0), axis=0, keepdims=True).astype(jnp.int32)
    carry_ref[...] += jnp.sum(chosen, axis=1, keepdims=True)
    cnt_ref[...] = carry_ref[...]


def _slot_fetch(slots_hbm, idx_smem, sem_idx, tile, buf):
    return pltpu.make_async_copy(slots_hbm.at[tile], idx_smem.at[buf], sem_idx.at[buf])


def _expert_kernel(be_ref, nu_ref, tok_hbm, hp_hbm, wg_ref, wu_ref, wd_ref, o_ref,
                   wg_s, wu_s, wd_s, idx_smem, x_buf, sem_idx, sem_rows):
    i = pl.program_id(0)
    n_live = nu_ref[0]
    live = i < n_live
    new_expert = jnp.logical_or(i == 0, be_ref[i] != be_ref[jnp.maximum(i - 1, 0)])
    rows = x_buf.shape[1]
    idx_rows = idx_smem.shape[1]

    def idx_fetch(blk):
        return pltpu.make_async_copy(tok_hbm.at[blk], idx_smem.at[blk % 2], sem_idx.at[blk % 2])

    def gather_start(blk):
        for b in range(2):
            @pl.when(blk % 2 == b)
            def _():
                for q in range(idx_rows):
                    def body(r, carry):
                        pltpu.make_async_copy(hp_hbm.at[pl.ds(idx_smem[b, q, r], 1)],
                                              x_buf.at[b, pl.ds(q * LANES + r, 1)],
                                              sem_rows.at[b]).start(priority=q % 2)
                        return carry
                    lax.fori_loop(0, LANES, body, 0, unroll=16)

    @pl.when(i == 0)
    def _():
        idx_fetch(0).start()
        idx_fetch(0).wait()
        gather_start(0)

        @pl.when(n_live > 1)
        def _():
            idx_fetch(1).start()

    @pl.when(i + 1 < n_live)
    def _():
        idx_fetch(i + 1).wait()
        gather_start(i + 1)

        @pl.when(i + 2 < n_live)
        def _():
            idx_fetch(i + 2).start()

    @pl.when(jnp.logical_and(live, new_expert))
    def _():
        wg_s[...] = wg_ref[...].astype(BF16)
        wu_s[...] = wu_ref[...].astype(BF16)
        wd_s[...] = wd_ref[...].astype(BF16)

    @pl.when(live)
    def _():
        b = i % 2
        pltpu.make_async_copy(hp_hbm.at[pl.ds(0, rows)], x_buf.at[b], sem_rows.at[b]).wait()
        lo, hi = _unpack_halves(_load_rows(x_buf.at[b], 0, rows))
        lo, hi = lo.astype(BF16), hi.astype(BF16)
        half = lo.shape[1]
        g = (jnp.dot(lo, wg_s[:half, :], preferred_element_type=F32)
             + jnp.dot(hi, wg_s[half:, :], preferred_element_type=F32))
        u = (jnp.dot(lo, wu_s[:half, :], preferred_element_type=F32)
             + jnp.dot(hi, wu_s[half:, :], preferred_element_type=F32))
        a = (_silu(g) * u).astype(BF16)
        _store_rows(o_ref, _pack_halves(jnp.dot(a, wd_s[...], preferred_element_type=F32)))

    @pl.when(jnp.logical_not(live))
    def _():
        o_ref[...] = jnp.zeros_like(o_ref)


def _combine_kernel(slots_hbm, y_hbm, h_ref, gate_ref, xs_ref, mg_ref, wg_ref, wu_ref, wd_ref, o_ref,
                    idx_smem, rows_buf, sem_idx, sem_rows):
    i = pl.program_id(0)
    rows = h_ref.shape[0]

    @pl.when(i == 0)
    def _():
        _slot_fetch(slots_hbm, idx_smem, sem_idx, 0, 0).start()

    buf = i % 2
    _slot_fetch(slots_hbm, idx_smem, sem_idx, i, buf).wait()

    @pl.when(i + 1 < pl.num_programs(0))
    def _():
        _slot_fetch(slots_hbm, idx_smem, sem_idx, i + 1, 1 - buf).start()

    for b in range(2):
        @pl.when(buf == b)
        def _():
            def body(r, carry):
                for k in range(TOP_K):
                    pltpu.make_async_copy(y_hbm.at[pl.ds(idx_smem[b, k, r], 1)],
                                          rows_buf.at[pl.ds(k * rows + r, 1)], sem_rows).start(priority=k % 2)
                return carry

            lax.fori_loop(0, rows, body, 0, unroll=8)

    h = h_ref[...]
    g = jnp.dot(h, wg_ref[...], preferred_element_type=F32)
    u = jnp.dot(h, wu_ref[...], preferred_element_type=F32)
    a = (_silu(g) * u).astype(BF16)
    y = jnp.dot(a, wd_ref[...], preferred_element_type=F32)

    pltpu.make_async_copy(y_hbm.at[pl.ds(0, rows * TOP_K)], rows_buf, sem_rows).wait()
    half = y.shape[1] // 2
    acc_lo, acc_hi = y[:, :half], y[:, half:]
    gate = gate_ref[...]
    for k in range(TOP_K):
        lo, hi = _unpack_halves(_load_rows(rows_buf, k * rows, rows))
        acc_lo = acc_lo + gate[:, k:k + 1] * lo
        acc_hi = acc_hi + gate[:, k:k + 1] * hi
    mg = mg_ref[...]
    o_ref[:, :half] = xs_ref[:, :half] + mg[:, :half] * acc_lo
    o_ref[:, half:] = xs_ref[:, half:] + mg[:, half:] * acc_hi


def route(h2, w_router, router_bias):
    t, d = h2.shape
    n_tiles = t // ROW_TILE
    n_exp = w_router.shape[1]
    return pl.pallas_call(
        _router_kernel,
        grid=(n_tiles,),
        in_specs=[pl.BlockSpec((ROW_TILE, d), lambda i: (i, 0)),
                  pl.BlockSpec((n_exp, d), lambda i: (0, 0)),
                  pl.BlockSpec((n_exp, 1), lambda i: (0, 0))],
        out_specs=[pl.BlockSpec((TOP_K, ROW_TILE), lambda i: (0, i)),
                   pl.BlockSpec((TOP_K, ROW_TILE), lambda i: (0, i)),
                   pl.BlockSpec((TOP_K, ROW_TILE), lambda i: (0, i)),
                   pl.BlockSpec((n_exp, 1), lambda i: (0, 0))],
        out_shape=[jax.ShapeDtypeStruct((TOP_K, t), jnp.int32), jax.ShapeDtypeStruct((TOP_K, t), F32),
                   jax.ShapeDtypeStruct((TOP_K, t), jnp.int32), jax.ShapeDtypeStruct((n_exp, 1), F32)],
        scratch_shapes=[pltpu.VMEM((n_exp, 1), F32)],
        compiler_params=_cparams("arbitrary"),
        name="router_topk",
    )(h2, jnp.transpose(w_router).astype(BF16), router_bias.astype(F32).reshape(n_exp, 1))


def moe_block(h2, h2_packed, xs, mod, k_gate, tiles_per_row, n_batch,
              w_router, router_bias, w_gate, w_up, w_down, ws_gate, ws_up, ws_down):
    t, d = h2.shape
    row_shape = h2_packed.shape[1:]
    de = ws_gate.shape[1]
    n_tiles = t // ROW_TILE
    eidx, gate_t, rank, cnt = route(h2, w_router, router_bias)
    n_blocks = -(-t * TOP_K // MOE_BLOCK) + N_EXPERTS
    counts = cnt[:, 0].astype(jnp.int32)
    padded = (counts + MOE_BLOCK - 1) // MOE_BLOCK * MOE_BLOCK
    pad_end = jnp.cumsum(padded).astype(jnp.int32)
    pad_start = pad_end - padded
    e_ids = jnp.arange(N_EXPERTS, dtype=jnp.int32)
    slot = rank + jnp.sum(jnp.where(eidx[:, :, None] == e_ids, pad_start, 0), axis=-1)
    slots = jnp.transpose(slot.reshape(TOP_K, n_tiles, ROW_TILE), (1, 0, 2))
    gate = jnp.transpose(gate_t)
    block_start = jnp.arange(n_blocks, dtype=jnp.int32) * MOE_BLOCK
    block_e = jnp.minimum(jnp.sum(pad_end[None, :] <= block_start[:, None], axis=1), N_EXPERTS - 1).astype(jnp.int32)
    n_used = (pad_end[-1] // MOE_BLOCK).astype(jnp.int32).reshape(1)

    idx_scratch = [pltpu.SMEM((2, TOP_K, ROW_TILE), jnp.int32)]
    token_ids = jnp.broadcast_to(jnp.arange(t, dtype=jnp.int32), (TOP_K, t))
    slot_tok = jnp.zeros((n_blocks * MOE_BLOCK,), jnp.int32).at[slot.reshape(-1)].set(
        token_ids.reshape(-1), unique_indices=True)
    slot_tok = slot_tok.reshape(n_blocks, MOE_BLOCK // LANES, LANES)

    y_sorted = pl.pallas_call(
        _expert_kernel,
        grid_spec=pltpu.PrefetchScalarGridSpec(
            num_scalar_prefetch=2,
            grid=(n_blocks,),
            in_specs=[pl.BlockSpec(memory_space=pl.ANY),
                      pl.BlockSpec(memory_space=pl.ANY),
                      pl.BlockSpec((None, d, de), lambda i, be, nu: (be[i], 0, 0)),
                      pl.BlockSpec((None, d, de), lambda i, be, nu: (be[i], 0, 0)),
                      pl.BlockSpec((None, de, d), lambda i, be, nu: (be[i], 0, 0))],
            out_specs=pl.BlockSpec((MOE_BLOCK,) + row_shape, lambda i, be, nu: (i, 0, 0)),
            scratch_shapes=[pltpu.VMEM((d, de), BF16), pltpu.VMEM((d, de), BF16), pltpu.VMEM((de, d), BF16),
                            pltpu.SMEM((2, MOE_BLOCK // LANES, LANES), jnp.int32),
                            pltpu.VMEM((2, MOE_BLOCK) + row_shape, jnp.uint32),
                            pltpu.SemaphoreType.DMA((2,)), pltpu.SemaphoreType.DMA((2,))],
        ),
        out_shape=jax.ShapeDtypeStruct((n_blocks * MOE_BLOCK,) + row_shape, jnp.uint32),
        compiler_params=_cparams("arbitrary"),
        name="routed_experts",
    )(block_e, n_used, slot_tok, h2_packed, w_gate, w_up, w_down)

    seg = functools.partial(_seg_of_tile, tiles_per_row=tiles_per_row, n_batch=n_batch)
    return pl.pallas_call(
        _combine_kernel,
        grid=(n_tiles,),
        in_specs=[pl.BlockSpec(memory_space=pl.ANY),
                  pl.BlockSpec(memory_space=pl.ANY),
                  pl.BlockSpec((ROW_TILE, d), lambda i: (i, 0)),
                  pl.BlockSpec((ROW_TILE, TOP_K), lambda i: (i, 0)),
                  pl.BlockSpec((ROW_TILE, d), lambda i: (i, 0)),
                  pl.BlockSpec((None, None, 1, d), lambda i: (seg(i), k_gate, 0, 0)),
                  pl.BlockSpec((d, de), lambda i: (0, 0)),
                  pl.BlockSpec((d, de), lambda i: (0, 0)),
                  pl.BlockSpec((de, d), lambda i: (0, 0))],
        out_specs=pl.BlockSpec((ROW_TILE, d), lambda i: (i, 0)),
        out_shape=jax.ShapeDtypeStruct((t, d), F32),
        scratch_shapes=idx_scratch + [pltpu.VMEM((TOP_K * ROW_TILE,) + row_shape, jnp.uint32),
                                      pltpu.SemaphoreType.DMA((2,)), pltpu.SemaphoreType.DMA(())],
        compiler_params=_cparams("arbitrary"),
        name="moe_combine",
    )(slots, y_sorted, h2, gate, xs, mod, ws_gate.astype(BF16), ws_up.astype(BF16), ws_down.astype(BF16))


def kernel(x, c, ctx, c_ctx, w_ada, b_ada, norm1_g, norm2_g, w_in_ab, s5_lambda_re, s5_lambda_im, s5_log_dt, s5_b_re, s5_b_im, s5_c_re, s5_c_im, s5_d, s5_w_glu, hgrn_lb, w_in_cd, ret_decay_logit, gdn_conv_w, gdn_a_log, gdn_dt_bias, gdn_norm_g, w_out, w_router, router_bias, w_exp_gate, w_exp_up, w_exp_down, w_sh_gate, w_sh_up, w_sh_down, final_norm_g):
    n_batch, n_lat, d = x.shape
    n_ctx = ctx.shape[1]
    n_tok = n_ctx + n_lat
    depth = w_ada.shape[0]
    assert n_ctx % ROW_TILE == 0 and n_lat % ROW_TILE == 0
    tiles_per_row, ctx_tiles = n_tok // ROW_TILE, n_ctx // ROW_TILE
    assert ctx_tiles == 1
    t = n_batch * n_tok

    xs = jnp.concatenate([ctx, x], axis=1).reshape(t, d)
    lb_all = jnp.cumsum(jax.nn.softmax(hgrn_lb.astype(F32), axis=0), axis=0)
    cond = jnp.concatenate([c, c_ctx[None]], axis=0)
    cond = jnp.pad(jax.nn.silu(cond), ((0, 8 - (n_batch + 1)), (0, 0)))
    ones_g = jnp.ones((D_HEAD,), F32)

    for i in range(depth):
        j = i // 2
        mod = matmul(cond, w_ada[i], F32, 8, 1024, "ada_mod")[:n_batch + 1] + b_ada[i]
        mod = mod.reshape(n_batch + 1, 6, 1, d)
        h = norm_mod(xs, norm1_g[i], mod, 0, 1, tiles_per_row, n_batch)
        if i % 2 == 0:
            z = matmul(h, w_in_ab[j].astype(BF16), F32, 512, 1024, "in_proj")
            a_width = s5_d.shape[1]
            b_heads = (z.shape[1] - a_width) // 5 // B_HEAD
            y_a = mixer_s5(z, n_batch, n_tok, n_ctx, s5_lambda_re[j], s5_lambda_im[j], s5_log_dt[j],
                           s5_b_re[j], s5_b_im[j], s5_c_re[j], s5_c_im[j], s5_d[j], s5_w_glu[j])
            o_f, o_b = mixer_gla(z, n_batch, n_tok, n_ctx, a_width, lb_all[j], b_heads, B_HEAD)
            gcb = (a_width + 4 * b_heads * B_HEAD) // (b_heads * B_HEAD)
            y_b = post_norm_gate(o_f, o_b, z, gcb, B_HEAD, jnp.ones((B_HEAD,), F32))
            m1, m2 = y_a, y_b
        else:
            cd_cols = w_in_cd.shape[2]
            cd_pad = -(-cd_cols // 1024) * 1024
            w_cd = jnp.pad(w_in_cd[j].astype(BF16), ((0, 0), (0, cd_pad - cd_cols)))
            z = matmul(h, w_cd, F32, 512, 1024, "in_proj")
            c_heads = ret_decay_logit.shape[2]
            d_heads = gdn_a_log.shape[2]
            o_f, o_b = mixer_retention(z, n_batch, n_tok, n_ctx, ret_decay_logit[j], c_heads)
            y_c = post_norm_gate(o_f, o_b, z, (2 * c_heads * C_QK + c_heads * C_V) // (c_heads * C_V), C_V,
                                 jnp.ones((C_V,), F32))
            col0 = 2 * c_heads * C_QK + 2 * c_heads * C_V
            o_f, o_b = mixer_gdn(z, n_batch, n_tok, n_ctx, col0, gdn_conv_w[j], gdn_a_log[j], gdn_dt_bias[j],
                                 d_heads, D_HEAD)
            y_d = post_norm_gate(o_f, o_b, z, (col0 + 3 * d_heads * D_HEAD) // (d_heads * D_HEAD), D_HEAD,
                                 gdn_norm_g[j])
            m1, m2 = y_c, y_d
        xs = out_proj(m1, m2, w_out[i].astype(BF16), xs, mod, 2, tiles_per_row, n_batch)
        h2, h2_packed = norm_mod(xs, norm2_g[i], mod, 3, 4, tiles_per_row, n_batch, packed=True)
        xs = moe_block(h2, h2_packed, xs, mod, 5, tiles_per_row, n_batch, w_router[i], router_bias[i],
                       w_exp_gate[i], w_exp_up[i], w_exp_down[i], w_sh_gate[i], w_sh_up[i], w_sh_down[i])
    out = final_norm(xs, final_norm_g, n_batch, tiles_per_row, ctx_tiles)
    return out.reshape(n_batch, n_lat, d)
```

```python
import functools

import numpy as np
import jax
import jax.numpy as jnp
from jax import lax
from jax.experimental import pallas as pl
from jax.experimental.pallas import tpu as pltpu

F32 = jnp.float32
BF16 = jnp.bfloat16

EPS = 1e-6
GRID_W = 64
CHUNK = 64
S5_L = 32
S5_H = 16
S5_P = 64
B_HEAD = 128
C_QK = 128
C_V = 256
D_HEAD = 128
ROPE_BASE = 10000.0
N_EXPERTS = 64
N_GROUPS = 8
TOPK_GROUPS = 4
TOP_K = 8
ROUTED_SCALE = 2.5
MOE_BLOCK = 256
ROW_TILE = 256
VMEM_LIMIT_V7X = 56 * 1024 * 1024
EXP_CLAMP = 80.0
GLA_SUB = 16


def _cparams(*sem):
    return pltpu.CompilerParams(dimension_semantics=sem, vmem_limit_bytes=VMEM_LIMIT_V7X)


def _silu(x):
    return x * jax.nn.sigmoid(x)


def _dot(a, b):
    return jnp.dot(a.astype(BF16), b.astype(BF16), preferred_element_type=F32)


def _dot_nt(a, b):
    return lax.dot_general(a.astype(BF16), b.astype(BF16), (((1,), (1,)), ((), ())), preferred_element_type=F32)


def _dot_tn(a, b):
    return lax.dot_general(a.astype(BF16), b.astype(BF16), (((0,), (0,)), ((), ())), preferred_element_type=F32)


def _split3(x):
    x1 = x.astype(BF16)
    r = x - x1.astype(F32)
    x2 = r.astype(BF16)
    x3 = (r - x2.astype(F32)).astype(BF16)
    return x1, x2, x3


def _tri_dot(tri, x):
    return sum(jnp.dot(tri, p, preferred_element_type=F32) for p in _split3(x))


def _dot_tri(x, tri):
    return sum(jnp.dot(p, tri, preferred_element_type=F32) for p in _split3(x))


def _tri_mask(n, lower):
    r = lax.broadcasted_iota(jnp.int32, (n, n), 0)
    c = lax.broadcasted_iota(jnp.int32, (n, n), 1)
    return (r >= c) if lower else (r <= c)


def _seg_of_tile(i, tiles_per_row, n_batch):
    return jnp.where(i % tiles_per_row == 0, n_batch, i // tiles_per_row)


def _pack_halves(y):
    half = y.shape[1] // 2
    lo = lax.bitcast_convert_type(y[:, :half].astype(BF16).astype(F32), jnp.uint32) >> 16
    hi = lax.bitcast_convert_type(y[:, half:].astype(BF16).astype(F32), jnp.uint32) & jnp.uint32(0xFFFF0000)
    return hi | lo


def _unpack_halves(w):
    lo = lax.bitcast_convert_type(w << 16, F32)
    hi = lax.bitcast_convert_type(w & jnp.uint32(0xFFFF0000), F32)
    return lo, hi


LANES = 128


def _store_rows(ref, packed):
    pieces = jnp.stack([packed[:, s * LANES:(s + 1) * LANES] for s in range(ref.shape[1])], axis=0)
    ref[...] = pltpu.einshape("stl->tsl", pieces)


def _load_rows(ref, row0, rows):
    x = pltpu.einshape("tsl->stl", ref[row0:row0 + rows])
    return jnp.concatenate([x[s] for s in range(ref.shape[1])], axis=1)


def _norm_mod_kernel(x_ref, g_ref, shift_ref, scale_ref, o_ref, *packed_ref):
    x = x_ref[...]
    y = x * lax.rsqrt(jnp.mean(x * x, axis=-1, keepdims=True) + EPS) * g_ref[...]
    y = y * (1.0 + scale_ref[...]) + shift_ref[...]
    o_ref[...] = y.astype(o_ref.dtype)
    if packed_ref:
        _store_rows(packed_ref[0], _pack_halves(y))


def norm_mod(xs, g, mod, k_shift, k_scale, tiles_per_row, n_batch, packed=False):
    t, d = xs.shape
    seg = functools.partial(_seg_of_tile, tiles_per_row=tiles_per_row, n_batch=n_batch)
    out_specs = [pl.BlockSpec((ROW_TILE, d), lambda i: (i, 0))]
    out_shape = [jax.ShapeDtypeStruct((t, d), BF16)]
    if packed:
        out_specs.append(pl.BlockSpec((ROW_TILE, d // 2 // LANES, LANES), lambda i: (i, 0, 0)))
        out_shape.append(jax.ShapeDtypeStruct((t, d // 2 // LANES, LANES), jnp.uint32))
    res = pl.pallas_call(
        _norm_mod_kernel,
        grid=(t // ROW_TILE,),
        in_specs=[
            pl.BlockSpec((ROW_TILE, d), lambda i: (i, 0)),
            pl.BlockSpec((1, d), lambda i: (0, 0)),
            pl.BlockSpec((None, None, 1, d), lambda i: (seg(i), k_shift, 0, 0)),
            pl.BlockSpec((None, None, 1, d), lambda i: (seg(i), k_scale, 0, 0)),
        ],
        out_specs=out_specs,
        out_shape=out_shape,
        compiler_params=_cparams("parallel"),
        name="norm_mod",
    )(xs, g.reshape(1, d), mod, mod)
    return res if packed else res[0]


def _final_norm_kernel(x_ref, g_ref, o_ref):
    x = x_ref[...]
    o_ref[...] = x * lax.rsqrt(jnp.mean(x * x, axis=-1, keepdims=True) + EPS) * g_ref[...]


def final_norm(xs, g, n_batch, tiles_per_row, ctx_tiles):
    t, d = xs.shape
    lat_tiles = tiles_per_row - ctx_tiles
    return pl.pallas_call(
        _final_norm_kernel,
        grid=(n_batch, lat_tiles),
        in_specs=[
            pl.BlockSpec((ROW_TILE, d), lambda b, i: (b * tiles_per_row + ctx_tiles + i, 0)),
            pl.BlockSpec((1, d), lambda b, i: (0, 0)),
        ],
        out_specs=pl.BlockSpec((ROW_TILE, d), lambda b, i: (b * lat_tiles + i, 0)),
        out_shape=jax.ShapeDtypeStruct((n_batch * lat_tiles * ROW_TILE, d), F32),
        compiler_params=_cparams("parallel", "parallel"),
        name="final_norm",
    )(xs, g.reshape(1, d))


def _mm_kernel(a_ref, w_ref, o_ref):
    o_ref[...] = jnp.dot(a_ref[...].astype(BF16), w_ref[...].astype(BF16),
                         preferred_element_type=F32).astype(o_ref.dtype)


def matmul(a, w, out_dtype, tm, tn, name):
    m, k = a.shape
    n = w.shape[1]
    return pl.pallas_call(
        _mm_kernel,
        grid=(n // tn, m // tm),
        in_specs=[pl.BlockSpec((tm, k), lambda j, i: (i, 0)),
                  pl.BlockSpec((k, tn), lambda j, i: (0, j))],
        out_specs=pl.BlockSpec((tm, tn), lambda j, i: (i, j)),
        out_shape=jax.ShapeDtypeStruct((m, n), out_dtype),
        compiler_params=_cparams("parallel", "parallel"),
        name=name,
    )(a, w)


def _out_proj_kernel(a1_ref, a2_ref, w1_ref, w2_ref, res_ref, gate_ref, o_ref):
    y = jnp.dot(a1_ref[...], w1_ref[...], preferred_element_type=F32)
    y += jnp.dot(a2_ref[...], w2_ref[...], preferred_element_type=F32)
    o_ref[...] = res_ref[...] + gate_ref[...] * y


def out_proj(a1, a2, w, xs, mod, k_gate, tiles_per_row, n_batch, tn=2048):
    t, d = xs.shape
    k1, k2 = a1.shape[1], a2.shape[1]
    assert k1 == k2
    seg = functools.partial(_seg_of_tile, tiles_per_row=tiles_per_row, n_batch=n_batch)
    return pl.pallas_call(
        _out_proj_kernel,
        grid=(d // tn, t // ROW_TILE),
        in_specs=[
            pl.BlockSpec((ROW_TILE, k1), lambda j, i: (i, 0)),
            pl.BlockSpec((ROW_TILE, k2), lambda j, i: (i, 0)),
            pl.BlockSpec((k1, tn), lambda j, i: (0, j)),
            pl.BlockSpec((k2, tn), lambda j, i: (1, j)),
            pl.BlockSpec((ROW_TILE, tn), lambda j, i: (i, j)),
            pl.BlockSpec((None, None, 1, tn), lambda j, i: (seg(i), k_gate, 0, j)),
        ],
        out_specs=pl.BlockSpec((ROW_TILE, tn), lambda j, i: (i, j)),
        out_shape=jax.ShapeDtypeStruct((t, d), F32),
        compiler_params=_cparams("parallel", "parallel"),
        name="out_proj",
    )(a1, a2, w, w, xs, mod)


def _post_kernel(of_ref, ob_ref, gate_ref, ng_ref, o_ref, *, head_dim):
    o = of_ref[...].astype(F32) + ob_ref[...].astype(F32)
    g = gate_ref[...].astype(F32)
    width = o.shape[1]
    for h in range(width // head_dim):
        sl = slice(h * head_dim, (h + 1) * head_dim)
        oh = o[:, sl]
        y = oh * lax.rsqrt(jnp.mean(oh * oh, axis=-1, keepdims=True) + EPS) * ng_ref[...]
        o_ref[:, sl] = (y * _silu(g[:, sl])).astype(o_ref.dtype)


def post_norm_gate(o_f, o_b, z, gate_col_block, head_dim, norm_g):
    t, width = o_f.shape
    return pl.pallas_call(
        functools.partial(_post_kernel, head_dim=head_dim),
        grid=(t // ROW_TILE,),
        in_specs=[
            pl.BlockSpec((ROW_TILE, width), lambda i: (i, 0)),
            pl.BlockSpec((ROW_TILE, width), lambda i: (i, 0)),
            pl.BlockSpec((ROW_TILE, width), lambda i: (i, gate_col_block)),
            pl.BlockSpec((1, head_dim), lambda i: (0, 0)),
        ],
        out_specs=pl.BlockSpec((ROW_TILE, width), lambda i: (i, 0)),
        out_shape=jax.ShapeDtypeStruct((t, width), BF16),
        compiler_params=_cparams("parallel"),
        name="post_norm_gate",
    )(o_f, o_b, z, norm_g.reshape(1, head_dim).astype(F32))


def _bwd_chunk(j, ctx_chunks, n_chunks):
    return jnp.where(j < ctx_chunks, ctx_chunks - 1 - j, n_chunks + ctx_chunks - 1 - j)


def _s5_direction_tables(lam_re, lam_im, log_dt, b_re, b_im, c_re, c_im, reverse):
    hi = lax.Precision.HIGHEST
    ln = S5_L
    lam_re, lam_im, b_re, b_im, c_re, c_im = (p.astype(F32) for p in (lam_re, lam_im, b_re, b_im, c_re, c_im))
    dt = jnp.exp(log_dt.astype(F32))[:, None]
    mag = jnp.exp(lam_re * dt)
    ab_re, ab_im = mag * jnp.cos(lam_im * dt), mag * jnp.sin(lam_im * dt)
    den = lam_re * lam_re + lam_im * lam_im
    fr = ((ab_re - 1) * lam_re + ab_im * lam_im) / den
    fi = (ab_im * lam_re - (ab_re - 1) * lam_im) / den
    bb_re = fr[..., None] * b_re - fi[..., None] * b_im
    bb_im = fr[..., None] * b_im + fi[..., None] * b_re
    tau = jnp.arange(ln + 1, dtype=F32)[:, None, None]
    pw = jnp.exp(tau * (lam_re * dt))
    pr, pi = pw * jnp.cos(tau * (lam_im * dt)), pw * jnp.sin(tau * (lam_im * dt))
    abr = pr[..., None] * bb_re - pi[..., None] * bb_im
    abi = pr[..., None] * bb_im + pi[..., None] * bb_re
    kern = (jnp.einsum('ghp,tgpk->tghk', c_re, abr[:ln], precision=hi)
            - jnp.einsum('ghp,tgpk->tghk', c_im, abi[:ln], precision=hi))
    pos = jnp.arange(ln)
    lag = (pos[None, :] - pos[:, None]) if not reverse else (pos[:, None] - pos[None, :])
    toe = jnp.where((lag >= 0)[:, :, None, None, None], kern[jnp.clip(lag, 0, ln - 1)], 0.0)
    g = lam_re.shape[0]
    intra = jnp.transpose(toe, (2, 0, 4, 1, 3)).reshape(g, ln * S5_H, ln * S5_H)
    pw_in = (ln - 1 - pos) if not reverse else pos
    inj = jnp.concatenate([abr[pw_in], abi[pw_in]], axis=2)
    inject = jnp.transpose(inj, (1, 0, 3, 2)).reshape(g, ln * S5_H, 2 * S5_P)
    pw_out = (pos + 1) if not reverse else (ln - pos)
    w_re = c_re[None] * pr[pw_out][:, :, None, :] - c_im[None] * pi[pw_out][:, :, None, :]
    w_im = -(c_re[None] * pi[pw_out][:, :, None, :] + c_im[None] * pr[pw_out][:, :, None, :])
    readout = jnp.transpose(jnp.concatenate([w_re, w_im], axis=3), (1, 3, 0, 2)).reshape(g, 2 * S5_P, ln * S5_H)
    decay = jnp.stack([pr[ln], pi[ln]])
    return intra, inject, readout, decay


def _s5_in_kernel(u_ref, w_ref, yi_ref, s_ref):
    r = jnp.dot(u_ref[...], w_ref[...], preferred_element_type=F32)
    n_intra = yi_ref.shape[-1]
    yi_ref[...] = r[:, :n_intra]
    s_ref[...] = r[:, n_intra:]


def _s5_scan_kernel(a_ref, sf_ref, sb_ref, of_ref, ob_ref, st_ref):
    @pl.when(pl.program_id(1) == 0)
    def _():
        st_ref[...] = jnp.zeros_like(st_ref)

    steps = sf_ref.shape[0]
    for d, (s_ref, o_ref) in enumerate(((sf_ref, of_ref), (sb_ref, ob_ref))):
        ar, ai = a_ref[2 * d], a_ref[2 * d + 1]
        sr, si = st_ref[2 * d], st_ref[2 * d + 1]
        for q in range(steps):
            r = q if d == 0 else steps - 1 - q
            o_ref[r, 0] = sr
            o_ref[r, 1] = si
            sr, si = (ar * sr - ai * si + s_ref[r, 0], ar * si + ai * sr + s_ref[r, 1])
        st_ref[2 * d] = sr
        st_ref[2 * d + 1] = si


def _s5_out_kernel(yi_ref, st_ref, w_ref, u_ref, d_ref, o_ref):
    y = yi_ref[...] + jnp.dot(st_ref[...].astype(BF16), w_ref[...], preferred_element_type=F32)
    o_ref[...] = (y + d_ref[...] * u_ref[...].astype(F32)).astype(o_ref.dtype)


def _glu_kernel(y_ref, w_ref, o_ref):
    y = jax.nn.gelu(y_ref[...].astype(F32))
    o_ref[...] = (y * jax.nn.sigmoid(jnp.dot(y.astype(BF16), w_ref[...], preferred_element_type=F32))
                  ).astype(o_ref.dtype)


def mixer_s5(z, n_batch, n_tok, n_ctx, lam_re, lam_im, log_dt, b_re, b_im, c_re, c_im, d_skip, w_glu):
    t = z.shape[0]
    g = lam_re.shape[1]
    width = g * S5_H
    lh = S5_L * S5_H
    rows = t // S5_L
    tabs = [_s5_direction_tables(lam_re[d], lam_im[d], log_dt[d], b_re[d], b_im[d], c_re[d], c_im[d], d == 1)
            for d in range(2)]
    w_in = jnp.concatenate([tabs[0][0] + tabs[1][0], tabs[0][1], tabs[1][1]], axis=2).astype(BF16)
    w_st = jnp.concatenate([tabs[0][2], tabs[1][2]], axis=1).astype(BF16)
    decay = jnp.concatenate([tabs[0][3], tabs[1][3]], axis=0).reshape(4, g * S5_P // 128, 128)
    u = z[:, :width].reshape(rows, S5_L, g, S5_H)
    u_g = jnp.transpose(u, (2, 0, 1, 3)).reshape(g, rows, lh).astype(BF16)

    n_st = 4 * S5_P
    yi, s_in = pl.pallas_call(
        _s5_in_kernel,
        grid=(g,),
        in_specs=[pl.BlockSpec((None, rows, lh), lambda i: (i, 0, 0)),
                  pl.BlockSpec((None, lh, lh + n_st), lambda i: (i, 0, 0))],
        out_specs=[pl.BlockSpec((None, rows, lh), lambda i: (i, 0, 0)),
                   pl.BlockSpec((None, rows, n_st), lambda i: (i, 0, 0))],
        out_shape=[jax.ShapeDtypeStruct((g, rows, lh), F32), jax.ShapeDtypeStruct((g, rows, n_st), F32)],
        compiler_params=_cparams("parallel"),
        name="s5_in",
    )(u_g, w_in)

    gp = g * S5_P // 128
    s_scan = jnp.transpose(s_in.reshape(g, rows, 4, S5_P), (1, 2, 0, 3)).reshape(rows, 4, gp, 128)
    blocks_per_row = n_tok // S5_L
    cb = n_ctx // S5_L
    assert blocks_per_row % cb == 0
    nblk = blocks_per_row // cb
    bwd = lambda j: jnp.where(j == 0, 0, nblk - j)
    st_f, st_b = pl.pallas_call(
        _s5_scan_kernel,
        grid=(n_batch, nblk),
        in_specs=[pl.BlockSpec((4, gp, 128), lambda b, j: (0, 0, 0)),
                  pl.BlockSpec((cb, 2, gp, 128), lambda b, j: (b * nblk + j, 0, 0, 0)),
                  pl.BlockSpec((cb, 2, gp, 128), lambda b, j: (b * nblk + bwd(j), 1, 0, 0))],
        out_specs=[pl.BlockSpec((cb, 2, gp, 128), lambda b, j: (b * nblk + j, 0, 0, 0)),
                   pl.BlockSpec((cb, 2, gp, 128), lambda b, j: (b * nblk + bwd(j), 0, 0, 0))],
        out_shape=[jax.ShapeDtypeStruct((rows, 2, gp, 128), F32)] * 2,
        scratch_shapes=[pltpu.VMEM((4, gp, 128), F32)],
        compiler_params=_cparams("parallel", "arbitrary"),
        name="s5_scan",
    )(decay, s_scan, s_scan)
    st = jnp.concatenate([st_f, st_b], axis=1).reshape(rows, 4, g, S5_P)
    st_g = jnp.transpose(st, (2, 0, 1, 3)).reshape(g, rows, n_st)

    d_vec = jnp.tile(d_skip.astype(F32).reshape(g, 1, S5_H), (1, S5_L, 1)).reshape(g, 1, lh)
    y_g = pl.pallas_call(
        _s5_out_kernel,
        grid=(g,),
        in_specs=[pl.BlockSpec((None, rows, lh), lambda i: (i, 0, 0)),
                  pl.BlockSpec((None, rows, n_st), lambda i: (i, 0, 0)),
                  pl.BlockSpec((None, n_st, lh), lambda i: (i, 0, 0)),
                  pl.BlockSpec((None, rows, lh), lambda i: (i, 0, 0)),
                  pl.BlockSpec((None, 1, lh), lambda i: (i, 0, 0))],
        out_specs=pl.BlockSpec((None, rows, lh), lambda i: (i, 0, 0)),
        out_shape=jax.ShapeDtypeStruct((g, rows, lh), BF16),
        compiler_params=_cparams("parallel"),
        name="s5_out",
    )(yi, st_g, w_st, u_g, d_vec)
    y = jnp.transpose(y_g.reshape(g, rows, S5_L, S5_H), (1, 2, 0, 3)).reshape(t, width)

    return pl.pallas_call(
        _glu_kernel,
        grid=(t // ROW_TILE,),
        in_specs=[pl.BlockSpec((ROW_TILE, width), lambda i: (i, 0)),
                  pl.BlockSpec((width, width), lambda i: (0, 0))],
        out_specs=pl.BlockSpec((ROW_TILE, width), lambda i: (i, 0)),
        out_shape=jax.ShapeDtypeStruct((t, width), BF16),
        compiler_params=_cparams("parallel"),
        name="s5_glu",
    )(y, w_glu.astype(BF16))


def _gla_kernel(lb_ref, qf_ref, ff_ref, vf_ref, qb_ref, fb_ref, vb_ref, of_ref, ob_ref, st_ref, *, n_heads, hd):
    @pl.when(pl.program_id(1) == 0)
    def _():
        st_ref[...] = jnp.zeros_like(st_ref)

    c = qf_ref.shape[0]
    items = []
    for d, (q_ref, f_ref, v_ref, o_ref) in enumerate(((qf_ref, ff_ref, vf_ref, of_ref),
                                                       (qb_ref, fb_ref, vb_ref, ob_ref))):
        lower = d == 0
        mask = _tri_mask(c, lower)
        tri = mask.astype(BF16)
        lb = lb_ref[d]
        sig = jax.nn.sigmoid(f_ref[...])
        kk = (1.0 - lb) * (1.0 - sig)
        logf = jnp.log(lb + (1.0 - lb) * sig)
        b = _tri_dot(tri, logf)
        b_end = b[c - 1:c, :] if lower else b[0:1, :]
        qs = _silu(q_ref[...])
        qt = (qs * jnp.exp(b)).astype(BF16)
        q_blk, k_blk = [], []
        for blk in range(c // GLA_SUB):
            r0, r1 = blk * GLA_SUB, (blk + 1) * GLA_SUB
            if lower:
                beta = b[r0 - 1:r0, :] if blk > 0 else 0.0
            else:
                beta = b[r1:r1 + 1, :] if r1 < c else 0.0
            q_blk.append((qs[r0:r1, :] * jnp.exp(b[r0:r1, :] - beta)).astype(BF16))
            k_blk.append((kk * jnp.exp(jnp.minimum(beta - b, EXP_CLAMP))).astype(BF16))
        kend = (kk * jnp.exp(b_end - b)).astype(BF16)
        dec = jnp.exp(b_end)
        v = v_ref[...].astype(BF16)
        for h in range(n_heads):
            sl = slice(h * hd, (h + 1) * hd)
            items.append((d, h, sl, o_ref, mask, qt[:, sl], [(qb[:, sl], kb[:, sl]) for qb, kb in zip(q_blk, k_blk)],
                          kend[:, sl], v[:, sl], dec[:, sl], st_ref[d, h]))
    att = [jnp.concatenate([_dot_nt(qb, kb) for qb, kb in it[6]], axis=0) for it in items]
    inter = [_dot_nt(it[5], it[10]) for it in items]
    upd = [_dot_tn(it[8], it[7]) for it in items]
    intra = [_dot(jnp.where(it[4], a, 0.0), it[8]) for it, a in zip(items, att)]
    for it, oi, os, up in zip(items, intra, inter, upd):
        d, h, sl, o_ref = it[:4]
        o_ref[:, sl] = (oi + os).astype(o_ref.dtype)
        st_ref[d, h] = it[9] * it[10] + up


def mixer_gla(z, n_batch, n_tok, n_ctx, col0, lb, n_heads, hd):
    width = n_heads * hd
    z3 = z.reshape(n_batch, n_tok, z.shape[1])
    nc, cc = n_tok // CHUNK, n_ctx // CHUNK
    cb0 = col0 // width
    bwd = functools.partial(_bwd_chunk, ctx_chunks=cc, n_chunks=nc)
    blk = (None, CHUNK, width)
    fw = lambda k: pl.BlockSpec(blk, lambda b, j: (b, j, cb0 + k))
    bw = lambda k: pl.BlockSpec(blk, lambda b, j: (b, bwd(j), cb0 + k))
    o_f, o_b = pl.pallas_call(
        functools.partial(_gla_kernel, n_heads=n_heads, hd=hd),
        grid=(n_batch, nc),
        in_specs=[pl.BlockSpec((2, 1, width), lambda b, j: (0, 0, 0)),
                  fw(0), fw(1), fw(3), bw(0), bw(2), bw(3)],
        out_specs=[pl.BlockSpec(blk, lambda b, j: (b, j, 0)),
                   pl.BlockSpec(blk, lambda b, j: (b, bwd(j), 0))],
        out_shape=[jax.ShapeDtypeStruct((n_batch, n_tok, width), BF16)] * 2,
        scratch_shapes=[pltpu.VMEM((2, n_heads, hd, hd), F32)],
        compiler_params=_cparams("parallel", "arbitrary"),
        name="gla_scan",
    )(lb.reshape(2, 1, width).astype(F32), z3, z3, z3, z3, z3, z3)
    return o_f.reshape(-1, width), o_b.reshape(-1, width)


def _rotary_tables(n_tok, n_ctx):
    n_lat = n_tok - n_ctx
    rows = n_lat // GRID_W
    row = jnp.repeat(jnp.arange(rows, dtype=F32), GRID_W)
    col = jnp.tile(jnp.arange(GRID_W, dtype=F32), rows)
    n_freq = C_QK // 4
    inv = ROPE_BASE ** (-jnp.arange(n_freq, dtype=F32) / n_freq)
    ang = jnp.concatenate([row[:, None] * inv, col[:, None] * inv], axis=-1)
    ang = jnp.concatenate([jnp.zeros((n_ctx, C_QK // 2), F32), ang], axis=0)
    cos, sin = jnp.cos(ang), jnp.sin(ang)
    return jnp.concatenate([cos, cos], axis=-1), jnp.concatenate([-sin, sin], axis=-1)


def _ret_kernel(cdec_ref, dmat_ref, qdec_ref, kdec_ref,
                qf_ref, kf_ref, vf_ref, cf_ref, sf_ref, qb_ref, kb_ref, vb_ref, cb_ref, sb_ref,
                of_ref, ob_ref, st_ref, *, n_heads):
    @pl.when(pl.program_id(1) == 0)
    def _():
        st_ref[...] = jnp.zeros_like(st_ref)

    half = C_QK // 2
    items = []
    for d, (q_ref, k_ref, v_ref, cos_ref, sin_ref, o_ref) in enumerate(
            ((qf_ref, kf_ref, vf_ref, cf_ref, sf_ref, of_ref), (qb_ref, kb_ref, vb_ref, cb_ref, sb_ref, ob_ref))):
        cos, sin = cos_ref[...], sin_ref[...]
        for h in range(n_heads):
            qs = slice(h * C_QK, (h + 1) * C_QK)
            vs = slice(h * C_V, (h + 1) * C_V)
            qh, kh = q_ref[:, qs].astype(F32), k_ref[:, qs].astype(F32)
            qh = (qh * cos + pltpu.roll(qh, half, axis=1) * sin) * (C_QK ** -0.5)
            kh = kh * cos + pltpu.roll(kh, half, axis=1) * sin
            items.append((d, h, vs, o_ref, qh.astype(BF16), kh.astype(BF16), (qh * qdec_ref[d, h]).astype(BF16),
                          (kh * kdec_ref[d, h]).astype(BF16), v_ref[:, vs].astype(BF16), st_ref[d, h]))
    att = [_dot_nt(it[4], it[5]) for it in items]
    inter = [_dot(it[6], it[9]) for it in items]
    upd = [_dot_tn(it[7], it[8]) for it in items]
    intra = [_dot(a * dmat_ref[it[0], it[1]], it[8]) for it, a in zip(items, att)]
    for it, oi, os, up in zip(items, intra, inter, upd):
        d, h, vs, o_ref = it[:4]
        o_ref[:, vs] = (oi + os).astype(o_ref.dtype)
        st_ref[d, h] = cdec_ref[d, h] * it[9] + up


def mixer_retention(z, n_batch, n_tok, n_ctx, decay_logit, n_heads):
    qw, vw = n_heads * C_QK, n_heads * C_V
    z3 = z.reshape(n_batch, n_tok, z.shape[1])
    nc, cc = n_tok // CHUNK, n_ctx // CHUNK
    bwd = functools.partial(_bwd_chunk, ctx_chunks=cc, n_chunks=nc)
    log_gamma = jax.nn.log_sigmoid(decay_logit.astype(F32))[:, :, None, None]
    idx = jnp.arange(CHUNK, dtype=F32)
    diff = idx[:, None] - idx[None, :]
    dmat_f = jnp.where(diff >= 0, jnp.exp(jnp.maximum(diff, 0.0) * log_gamma[0]), 0.0)
    dmat_b = jnp.where(diff <= 0, jnp.exp(jnp.maximum(-diff, 0.0) * log_gamma[1]), 0.0)
    dmat = jnp.stack([dmat_f, dmat_b])
    ones = jnp.ones((1, 1, 1, C_QK), F32)
    pos_f, pos_b = idx[None, None, :, None], (CHUNK - 1 - idx)[None, None, :, None]
    lg = log_gamma
    qdec = jnp.concatenate([jnp.exp((pos_f + 1) * lg[0:1]), jnp.exp((pos_b + 1) * lg[1:2])]) * ones
    kdec = jnp.concatenate([jnp.exp((CHUNK - 1 - pos_f) * lg[0:1]), jnp.exp((CHUNK - 1 - pos_b) * lg[1:2])]) * ones
    cdec = jnp.exp(CHUNK * log_gamma[:, :, 0, 0])
    cos2, sin2 = _rotary_tables(n_tok, n_ctx)
    full = lambda shape: pl.BlockSpec(shape, lambda b, j: (0,) * len(shape))
    tab = lambda order: pl.BlockSpec((CHUNK, C_QK), lambda b, j: (order(j), 0))
    ident = lambda j: j
    vcb = 2 * qw // vw
    assert vcb * vw == 2 * qw
    def specs(order):
        return [pl.BlockSpec((None, CHUNK, qw), lambda b, j: (b, order(j), 0)),
                pl.BlockSpec((None, CHUNK, qw), lambda b, j: (b, order(j), 1)),
                pl.BlockSpec((None, CHUNK, vw), lambda b, j: (b, order(j), vcb)),
                tab(order), tab(order)]
    o_f, o_b = pl.pallas_call(
        functools.partial(_ret_kernel, n_heads=n_heads),
        grid=(n_batch, nc),
        in_specs=[pl.BlockSpec(memory_space=pltpu.SMEM), full((2, n_heads, CHUNK, CHUNK)),
                  full((2, n_heads, CHUNK, C_QK)), full((2, n_heads, CHUNK, C_QK))] + specs(ident) + specs(bwd),
        out_specs=[pl.BlockSpec((None, CHUNK, vw), lambda b, j: (b, j, 0)),
                   pl.BlockSpec((None, CHUNK, vw), lambda b, j: (b, bwd(j), 0))],
        out_shape=[jax.ShapeDtypeStruct((n_batch, n_tok, vw), BF16)] * 2,
        scratch_shapes=[pltpu.VMEM((2, n_heads, C_QK, C_V), F32)],
        compiler_params=_cparams("parallel", "arbitrary"),
        name="retention_scan",
    )(cdec, dmat, qdec, kdec, z3, z3, z3, cos2, sin2, z3, z3, z3, cos2, sin2)
    return o_f.reshape(-1, vw), o_b.reshape(-1, vw)


def _gdn_conv_kernel(prev_ref, cur_ref, next_ref, w_ref, o_ref, *, tiles_per_row, ctx_tiles, n_heads, hd):
    i = pl.program_id(0)
    r = i % tiles_per_row
    first = jnp.logical_or(r == 0, r == ctx_tiles)
    last = jnp.logical_or(r == ctx_tiles - 1, r == tiles_per_row - 1)
    x = cur_ref[...].astype(F32)
    rows = x.shape[0]
    rid = lax.broadcasted_iota(jnp.int32, x.shape, 0)
    hp = prev_ref.shape[0]
    x_prev = jnp.where(first, 0.0, prev_ref[hp - 1:hp, :].astype(F32))
    x_next = jnp.where(last, 0.0, next_ref[0:1, :].astype(F32))
    left = jnp.where(rid == 0, x_prev, pltpu.roll(x, 1, axis=0))
    right = jnp.where(rid == rows - 1, x_next, pltpu.roll(x, rows - 1, axis=0))
    w = w_ref[...]
    y = _silu(left * w[0:1, :] + x * w[1:2, :] + right * w[2:3, :])
    width = n_heads * hd
    for h in range(3 * n_heads):
        sl = slice(h * hd, (h + 1) * hd)
        yh = y[:, sl]
        if h < 2 * n_heads:
            yh = yh * lax.rsqrt(jnp.sum(yh * yh, axis=-1, keepdims=True) + EPS)
            if h < n_heads:
                yh = yh * (hd ** -0.5)
        o_ref[:, sl] = yh.astype(o_ref.dtype)


def _gdn_kernel(qf_ref, kf_ref, vf_ref, cf_ref, rf_ref, qb_ref, kb_ref, vb_ref, cb_ref, rb_ref,
                of_ref, ob_ref, st_ref, *, n_heads, hd):
    @pl.when(pl.program_id(1) == 0)
    def _():
        st_ref[...] = jnp.zeros_like(st_ref)

    c = qf_ref.shape[0]
    eye = (lax.broadcasted_iota(jnp.int32, (c, c), 0) == lax.broadcasted_iota(jnp.int32, (c, c), 1)).astype(F32)
    items = []
    for d, (q_ref, k_ref, v_ref, col_ref, row_ref, o_ref) in enumerate(
            ((qf_ref, kf_ref, vf_ref, cf_ref, rf_ref, of_ref), (qb_ref, kb_ref, vb_ref, cb_ref, rb_ref, ob_ref))):
        lower = d == 0
        incl = _tri_mask(c, lower)
        strict = jnp.logical_and(incl, eye == 0.0)
        col = col_ref[...]
        row = row_ref[...]
        b_cols = _tri_dot(incl.astype(BF16), col)
        b_rows = _dot_tri(row, _tri_mask(c, not lower).astype(BF16))
        for h in range(n_heads):
            sl = slice(h * hd, (h + 1) * hd)
            ib, ig = d * n_heads + h, (2 + d) * n_heads + h
            be = col[:, ib:ib + 1]
            bc = b_cols[:, ig:ig + 1]
            br = b_rows[ig:ig + 1, :]
            b_end = bc[c - 1:c, :] if lower else bc[0:1, :]
            gam_i = jnp.exp(jnp.where(incl, bc - br, -1e30))
            gam_s = jnp.where(strict, gam_i, 0.0)
            qh, kh, vh = q_ref[:, sl], k_ref[:, sl], v_ref[:, sl]
            eb = jnp.exp(bc)
            rhs = jnp.concatenate([be * vh.astype(F32), (be * eb) * kh.astype(F32)], axis=1).astype(BF16)
            items.append(dict(d=d, h=h, sl=sl, o_ref=o_ref, be=be, gam_i=gam_i, gam_s=gam_s, q=qh, k=kh, rhs=rhs,
                              qe=(qh.astype(F32) * eb).astype(BF16), dec=jnp.exp(b_end),
                              ke=(kh.astype(F32) * jnp.exp(b_end - bc)).astype(BF16), s=st_ref[d, h]))
    kk = [_dot_nt(it["k"], it["k"]) for it in items]
    qk = [_dot_nt(it["q"], it["k"]) for it in items]
    qs = [_dot(it["qe"], it["s"]) for it in items]
    ns = [it["be"] * a * it["gam_s"] for it, a in zip(items, kk)]
    invs = [eye - n for n in ns]
    ps = [_dot(n, n) for n in ns]
    levels = int(np.log2(c)) - 1
    for lvl in range(levels):
        prods = [_dot(inv, p) for inv, p in zip(invs, ps)]
        if lvl < levels - 1:
            ps = [_dot(p, p) for p in ps]
        invs = [inv + pr for inv, pr in zip(invs, prods)]
    sols = [_dot(inv, it["rhs"]) for inv, it in zip(invs, items)]
    sks = [_dot(sol[:, hd:], it["s"]) for sol, it in zip(sols, items)]
    us = [sol[:, :hd] - sk for sol, sk in zip(sols, sks)]
    intra = [_dot(a * it["gam_i"], u) for a, it, u in zip(qk, items, us)]
    upd = [_dot_tn(it["ke"], u) for it, u in zip(items, us)]
    for it, oi, os, up in zip(items, intra, qs, upd):
        it["o_ref"][:, it["sl"]] = (oi + os).astype(it["o_ref"].dtype)
        st_ref[it["d"], it["h"]] = it["dec"] * it["s"] + up


def mixer_gdn(z, n_batch, n_tok, n_ctx, col0, conv_w, a_log, dt_bias, n_heads, hd):
    t, zw = z.shape
    width = n_heads * hd
    tiles_per_row, ctx_tiles = n_tok // ROW_TILE, n_ctx // ROW_TILE
    cb0 = col0 // (3 * width)
    assert cb0 * 3 * width == col0
    halo = 8
    hpt = ROW_TILE // halo
    n_halo = t // halo
    qkv = pl.pallas_call(
        functools.partial(_gdn_conv_kernel, tiles_per_row=tiles_per_row, ctx_tiles=ctx_tiles, n_heads=n_heads, hd=hd),
        grid=(t // ROW_TILE,),
        in_specs=[pl.BlockSpec((halo, 3 * width), lambda i: (jnp.maximum(i * hpt - 1, 0), cb0)),
                  pl.BlockSpec((ROW_TILE, 3 * width), lambda i: (i, cb0)),
                  pl.BlockSpec((halo, 3 * width), lambda i: (jnp.minimum((i + 1) * hpt, n_halo - 1), cb0)),
                  pl.BlockSpec((3, 3 * width), lambda i: (0, 0))],
        out_specs=pl.BlockSpec((ROW_TILE, 3 * width), lambda i: (i, 0)),
        out_shape=jax.ShapeDtypeStruct((t, 3 * width), BF16),
        compiler_params=_cparams("parallel"),
        name="gdn_conv",
    )(z, z, z, conv_w.astype(F32))

    sc0 = col0 + 4 * width
    small = z[:, sc0:sc0 + 4 * n_heads].astype(F32)
    a_log, dt_bias = a_log.astype(F32), dt_bias.astype(F32)
    be = jax.nn.sigmoid(small[:, :2 * n_heads])
    la_f = -jnp.exp(a_log[0]) * jax.nn.softplus(small[:, 2 * n_heads:3 * n_heads] + dt_bias[0])
    la_b = -jnp.exp(a_log[1]) * jax.nn.softplus(small[:, 3 * n_heads:] + dt_bias[1])
    cols = jnp.concatenate([be, la_f, la_b], axis=1)
    nc, cc = n_tok // CHUNK, n_ctx // CHUNK
    cols3 = cols.reshape(n_batch, n_tok, 4 * n_heads)
    rows4 = jnp.transpose(cols.reshape(n_batch, nc, CHUNK, 4 * n_heads), (0, 1, 3, 2))
    qkv3 = qkv.reshape(n_batch, n_tok, 3 * width)
    bwd = functools.partial(_bwd_chunk, ctx_chunks=cc, n_chunks=nc)
    ident = lambda j: j
    def specs(order):
        return [pl.BlockSpec((None, CHUNK, width), lambda b, j: (b, order(j), 0)),
                pl.BlockSpec((None, CHUNK, width), lambda b, j: (b, order(j), 1)),
                pl.BlockSpec((None, CHUNK, width), lambda b, j: (b, order(j), 2)),
                pl.BlockSpec((None, CHUNK, 4 * n_heads), lambda b, j: (b, order(j), 0)),
                pl.BlockSpec((None, None, 4 * n_heads, CHUNK), lambda b, j: (b, order(j), 0, 0))]
    o_f, o_b = pl.pallas_call(
        functools.partial(_gdn_kernel, n_heads=n_heads, hd=hd),
        grid=(n_batch, nc),
        in_specs=specs(ident) + specs(bwd),
        out_specs=[pl.BlockSpec((None, CHUNK, width), lambda b, j: (b, j, 0)),
                   pl.BlockSpec((None, CHUNK, width), lambda b, j: (b, bwd(j), 0))],
        out_shape=[jax.ShapeDtypeStruct((n_batch, n_tok, width), BF16)] * 2,
        scratch_shapes=[pltpu.VMEM((2, n_heads, hd, hd), F32)],
        compiler_params=_cparams("parallel", "arbitrary"),
        name="gdn_scan",
    )(qkv3, qkv3, qkv3, cols3, rows4, qkv3, qkv3, qkv3, cols3, rows4)
    return o_f.reshape(-1, width), o_b.reshape(-1, width)


def _first_max(x, idx, sentinel):
    m = jnp.max(x, axis=0, keepdims=True)
    return m, jnp.min(jnp.where(x == m, idx, sentinel), axis=0, keepdims=True)


def _router_kernel(h_ref, wt_ref, bias_ref, e_ref, g_ref, rank_ref, cnt_ref, carry_ref):
    @pl.when(pl.program_id(0) == 0)
    def _():
        carry_ref[...] = jnp.zeros_like(carry_ref)

    cols = h_ref.shape[0]
    gsz = N_EXPERTS // N_GROUPS
    neg = -jnp.inf
    logits = lax.dot_general(wt_ref[...], h_ref[...], (((1,), (1,)), ((), ())), preferred_element_type=F32)
    scores = jax.nn.sigmoid(logits)
    sel = scores + bias_ref[...]
    i_g = lax.broadcasted_iota(jnp.int32, (gsz, cols), 0)
    blocks, g_scores = [], []
    for g in range(N_GROUPS):
        blk = sel[g * gsz:(g + 1) * gsz, :]
        m1, first = _first_max(blk, i_g, gsz)
        m2 = jnp.max(jnp.where(i_g == first, neg, blk), axis=0, keepdims=True)
        blocks.append(blk)
        g_scores.append(m1 + m2)
    keep = [jnp.zeros((1, cols), jnp.bool_) for _ in range(N_GROUPS)]
    for _ in range(TOPK_GROUPS):
        best = functools.reduce(jnp.maximum, g_scores)
        found = jnp.zeros((1, cols), jnp.bool_)
        for g in range(N_GROUPS):
            pick = jnp.logical_and(g_scores[g] == best, jnp.logical_not(found))
            found = jnp.logical_or(found, pick)
            keep[g] = jnp.logical_or(keep[g], pick)
            g_scores[g] = jnp.where(pick, neg, g_scores[g])
    cur = jnp.concatenate([jnp.where(keep[g], blocks[g], neg) for g in range(N_GROUPS)], axis=0)
    i_e = lax.broadcasted_iota(jnp.int32, (N_EXPERTS, cols), 0)
    picks, gates = [], []
    for _ in range(TOP_K):
        _, idx = _first_max(cur, i_e, N_EXPERTS)
        pick = i_e == idx
        picks.append((idx, pick))
        gates.append(jnp.sum(jnp.where(pick, scores, 0.0), axis=0, keepdims=True))
        cur = jnp.where(pick, neg, cur)
    total = functools.reduce(jnp.add, gates)
    chosen = functools.reduce(jnp.logical_or, [p for _, p in picks]).astype(F32)
    r = lax.broadcasted_iota(jnp.int32, (cols, cols), 0)
    c = lax.broadcasted_iota(jnp.int32, (cols, cols), 1)
    before = jnp.dot(chosen.astype(BF16), (r < c).astype(BF16), preferred_element_type=F32) + carry_ref[...]
    for k, (idx, pick) in enumerate(picks):
        e_ref[k:k + 1, :] = idx
        g_ref[k:k + 1, :] = gates[k] / total * ROUTED_SCALE
        rank_ref[k:k + 1, :] = jnp.sum(jnp.where(pick, before, 0.0), axis=0, keepdims=True).astype(jnp.int32)
    carry_ref[...] += jnp.sum(chosen, axis=1, keepdims=True)
    cnt_ref[...] = carry_ref[...]


def _slot_fetch(slots_hbm, idx_smem, sem_idx, tile, buf):
    return pltpu.make_async_copy(slots_hbm.at[tile], idx_smem.at[buf], sem_idx.at[buf])


def _expert_kernel(be_ref, nu_ref, tok_hbm, hp_hbm, wg_ref, wu_ref, wd_ref, o_ref,
                   wg_s, wu_s, wd_s, idx_smem, x_buf, sem_idx, sem_rows):
    i = pl.program_id(0)
    n_live = nu_ref[0]
    live = i < n_live
    new_expert = jnp.logical_or(i == 0, be_ref[i] != be_ref[jnp.maximum(i - 1, 0)])
    rows = x_buf.shape[1]
    idx_rows = idx_smem.shape[1]

    def idx_fetch(blk):
        return pltpu.make_async_copy(tok_hbm.at[blk], idx_smem.at[blk % 2], sem_idx.at[blk % 2])

    def gather_start(blk):
        for b in range(2):
            @pl.when(blk % 2 == b)
            def _():
                for q in range(idx_rows):
                    def body(r, carry):
                        pltpu.make_async_copy(hp_hbm.at[pl.ds(idx_smem[b, q, r], 1)],
                                              x_buf.at[b, pl.ds(q * LANES + r, 1)],
                                              sem_rows.at[b]).start(priority=1)
                        return carry
                    lax.fori_loop(0, LANES, body, 0, unroll=16)

    @pl.when(i == 0)
    def _():
        idx_fetch(0).start()
        idx_fetch(0).wait()
        gather_start(0)

        @pl.when(n_live > 1)
        def _():
            idx_fetch(1).start()

    @pl.when(i + 1 < n_live)
    def _():
        idx_fetch(i + 1).wait()
        gather_start(i + 1)

        @pl.when(i + 2 < n_live)
        def _():
            idx_fetch(i + 2).start()

    @pl.when(jnp.logical_and(live, new_expert))
    def _():
        wg_s[...] = wg_ref[...].astype(BF16)
        wu_s[...] = wu_ref[...].astype(BF16)
        wd_s[...] = wd_ref[...].astype(BF16)

    @pl.when(live)
    def _():
        b = i % 2
        pltpu.make_async_copy(hp_hbm.at[pl.ds(0, rows)], x_buf.at[b], sem_rows.at[b]).wait()
        lo, hi = _unpack_halves(_load_rows(x_buf.at[b], 0, rows))
        lo, hi = lo.astype(BF16), hi.astype(BF16)
        half = lo.shape[1]
        g = (jnp.dot(lo, wg_s[:half, :], preferred_element_type=F32)
             + jnp.dot(hi, wg_s[half:, :], preferred_element_type=F32))
        u = (jnp.dot(lo, wu_s[:half, :], preferred_element_type=F32)
             + jnp.dot(hi, wu_s[half:, :], preferred_element_type=F32))
        a = (_silu(g) * u).astype(BF16)
        _store_rows(o_ref, _pack_halves(jnp.dot(a, wd_s[...], preferred_element_type=F32)))

    @pl.when(jnp.logical_not(live))
    def _():
        o_ref[...] = jnp.zeros_like(o_ref)


def _combine_kernel(slots_hbm, y_hbm, h_ref, gate_ref, xs_ref, mg_ref, wg_ref, wu_ref, wd_ref, o_ref,
                    idx_smem, rows_buf, sem_idx, sem_rows):
    i = pl.program_id(0)
    rows = h_ref.shape[0]

    @pl.when(i == 0)
    def _():
        _slot_fetch(slots_hbm, idx_smem, sem_idx, 0, 0).start()

    buf = i % 2
    _slot_fetch(slots_hbm, idx_smem, sem_idx, i, buf).wait()

    @pl.when(i + 1 < pl.num_programs(0))
    def _():
        _slot_fetch(slots_hbm, idx_smem, sem_idx, i + 1, 1 - buf).start()

    for b in range(2):
        @pl.when(buf == b)
        def _():
            def body(r, carry):
                for k in range(TOP_K):
                    pltpu.make_async_copy(y_hbm.at[pl.ds(idx_smem[b, k, r], 1)],
                                          rows_buf.at[pl.ds(k * rows + r, 1)], sem_rows).start(priority=k % 2)
                return carry

            lax.fori_loop(0, rows, body, 0, unroll=8)

    h = h_ref[...]
    g = jnp.dot(h, wg_ref[...], preferred_element_type=F32)
    u = jnp.dot(h, wu_ref[...], preferred_element_type=F32)
    a = (_silu(g) * u).astype(BF16)
    y = jnp.dot(a, wd_ref[...], preferred_element_type=F32)

    pltpu.make_async_copy(y_hbm.at[pl.ds(0, rows * TOP_K)], rows_buf, sem_rows).wait()
    half = y.shape[1] // 2
    acc_lo, acc_hi = y[:, :half], y[:, half:]
    gate = gate_ref[...]
    for k in range(TOP_K):
        lo, hi = _unpack_halves(_load_rows(rows_buf, k * rows, rows))
        acc_lo = acc_lo + gate[:, k:k + 1] * lo
        acc_hi = acc_hi + gate[:, k:k + 1] * hi
    mg = mg_ref[...]
    o_ref[:, :half] = xs_ref[:, :half] + mg[:, :half] * acc_lo
    o_ref[:, half:] = xs_ref[:, half:] + mg[:, half:] * acc_hi


def route(h2, w_router, router_bias):
    t, d = h2.shape
    n_tiles = t // ROW_TILE
    n_exp = w_router.shape[1]
    return pl.pallas_call(
        _router_kernel,
        grid=(n_tiles,),
        in_specs=[pl.BlockSpec((ROW_TILE, d), lambda i: (i, 0)),
                  pl.BlockSpec((n_exp, d), lambda i: (0, 0)),
                  pl.BlockSpec((n_exp, 1), lambda i: (0, 0))],
        out_specs=[pl.BlockSpec((TOP_K, ROW_TILE), lambda i: (0, i)),
                   pl.BlockSpec((TOP_K, ROW_TILE), lambda i: (0, i)),
                   pl.BlockSpec((TOP_K, ROW_TILE), lambda i: (0, i)),
                   pl.BlockSpec((n_exp, 1), lambda i: (0, 0))],
        out_shape=[jax.ShapeDtypeStruct((TOP_K, t), jnp.int32), jax.ShapeDtypeStruct((TOP_K, t), F32),
                   jax.ShapeDtypeStruct((TOP_K, t), jnp.int32), jax.ShapeDtypeStruct((n_exp, 1), F32)],
        scratch_shapes=[pltpu.VMEM((n_exp, 1), F32)],
        compiler_params=_cparams("arbitrary"),
        name="router_topk",
    )(h2, jnp.transpose(w_router).astype(BF16), router_bias.astype(F32).reshape(n_exp, 1))


def moe_block(h2, h2_packed, xs, mod, k_gate, tiles_per_row, n_batch,
              w_router, router_bias, w_gate, w_up, w_down, ws_gate, ws_up, ws_down):
    t, d = h2.shape
    row_shape = h2_packed.shape[1:]
    de = ws_gate.shape[1]
    n_tiles = t // ROW_TILE
    eidx, gate_t, rank, cnt = route(h2, w_router, router_bias)
    n_blocks = -(-t * TOP_K // MOE_BLOCK) + N_EXPERTS
    counts = cnt[:, 0].astype(jnp.int32)
    padded = (counts + MOE_BLOCK - 1) // MOE_BLOCK * MOE_BLOCK
    pad_end = jnp.cumsum(padded).astype(jnp.int32)
    pad_start = pad_end - padded
    e_ids = jnp.arange(N_EXPERTS, dtype=jnp.int32)
    slot = rank + jnp.sum(jnp.where(eidx[:, :, None] == e_ids, pad_start, 0), axis=-1)
    slots = jnp.transpose(slot.reshape(TOP_K, n_tiles, ROW_TILE), (1, 0, 2))
    gate = jnp.transpose(gate_t)
    block_start = jnp.arange(n_blocks, dtype=jnp.int32) * MOE_BLOCK
    block_e = jnp.minimum(jnp.sum(pad_end[None, :] <= block_start[:, None], axis=1), N_EXPERTS - 1).astype(jnp.int32)
    n_used = (pad_end[-1] // MOE_BLOCK).astype(jnp.int32).reshape(1)

    idx_scratch = [pltpu.SMEM((2, TOP_K, ROW_TILE), jnp.int32)]
    token_ids = jnp.broadcast_to(jnp.arange(t, dtype=jnp.int32), (TOP_K, t))
    slot_tok = jnp.zeros((n_blocks * MOE_BLOCK,), jnp.int32).at[slot.reshape(-1)].set(
        token_ids.reshape(-1), unique_indices=True)
    slot_tok = slot_tok.reshape(n_blocks, MOE_BLOCK // LANES, LANES)

    y_sorted = pl.pallas_call(
        _expert_kernel,
        grid_spec=pltpu.PrefetchScalarGridSpec(
            num_scalar_prefetch=2,
            grid=(n_blocks,),
            in_specs=[pl.BlockSpec(memory_space=pl.ANY),
                      pl.BlockSpec(memory_space=pl.ANY),
                      pl.BlockSpec((None, d, de), lambda i, be, nu: (be[i], 0, 0)),
                      pl.BlockSpec((None, d, de), lambda i, be, nu: (be[i], 0, 0)),
                      pl.BlockSpec((None, de, d), lambda i, be, nu: (be[i], 0, 0))],
            out_specs=pl.BlockSpec((MOE_BLOCK,) + row_shape, lambda i, be, nu: (i, 0, 0)),
            scratch_shapes=[pltpu.VMEM((d, de), BF16), pltpu.VMEM((d, de), BF16), pltpu.VMEM((de, d), BF16),
                            pltpu.SMEM((2, MOE_BLOCK // LANES, LANES), jnp.int32),
                            pltpu.VMEM((2, MOE_BLOCK) + row_shape, jnp.uint32),
                            pltpu.SemaphoreType.DMA((2,)), pltpu.SemaphoreType.DMA((2,))],
        ),
        out_shape=jax.ShapeDtypeStruct((n_blocks * MOE_BLOCK,) + row_shape, jnp.uint32),
        compiler_params=_cparams("arbitrary"),
        name="routed_experts",
    )(block_e, n_used, slot_tok, h2_packed, w_gate, w_up, w_down)

    seg = functools.partial(_seg_of_tile, tiles_per_row=tiles_per_row, n_batch=n_batch)
    return pl.pallas_call(
        _combine_kernel,
        grid=(n_tiles,),
        in_specs=[pl.BlockSpec(memory_space=pl.ANY),
                  pl.BlockSpec(memory_space=pl.ANY),
                  pl.BlockSpec((ROW_TILE, d), lambda i: (i, 0)),
                  pl.BlockSpec((ROW_TILE, TOP_K), lambda i: (i, 0)),
                  pl.BlockSpec((ROW_TILE, d), lambda i: (i, 0)),
                  pl.BlockSpec((None, None, 1, d), lambda i: (seg(i), k_gate, 0, 0)),
                  pl.BlockSpec((d, de), lambda i: (0, 0)),
                  pl.BlockSpec((d, de), lambda i: (0, 0)),
                  pl.BlockSpec((de, d), lambda i: (0, 0))],
        out_specs=pl.BlockSpec((ROW_TILE, d), lambda i: (i, 0)),
        out_shape=jax.ShapeDtypeStruct((t, d), F32),
        scratch_shapes=idx_scratch + [pltpu.VMEM((TOP_K * ROW_TILE,) + row_shape, jnp.uint32),
                                      pltpu.SemaphoreType.DMA((2,)), pltpu.SemaphoreType.DMA(())],
        compiler_params=_cparams("arbitrary"),
        name="moe_combine",
    )(slots, y_sorted, h2, gate, xs, mod, ws_gate.astype(BF16), ws_up.astype(BF16), ws_down.astype(BF16))


def kernel(x, c, ctx, c_ctx, w_ada, b_ada, norm1_g, norm2_g, w_in_ab, s5_lambda_re, s5_lambda_im, s5_log_dt, s5_b_re, s5_b_im, s5_c_re, s5_c_im, s5_d, s5_w_glu, hgrn_lb, w_in_cd, ret_decay_logit, gdn_conv_w, gdn_a_log, gdn_dt_bias, gdn_norm_g, w_out, w_router, router_bias, w_exp_gate, w_exp_up, w_exp_down, w_sh_gate, w_sh_up, w_sh_down, final_norm_g):
    n_batch, n_lat, d = x.shape
    n_ctx = ctx.shape[1]
    n_tok = n_ctx + n_lat
    depth = w_ada.shape[0]
    assert n_ctx % ROW_TILE == 0 and n_lat % ROW_TILE == 0
    tiles_per_row, ctx_tiles = n_tok // ROW_TILE, n_ctx // ROW_TILE
    assert ctx_tiles == 1
    t = n_batch * n_tok

    xs = jnp.concatenate([ctx, x], axis=1).reshape(t, d)
    lb_all = jnp.cumsum(jax.nn.softmax(hgrn_lb.astype(F32), axis=0), axis=0)
    cond = jnp.concatenate([c, c_ctx[None]], axis=0)
    cond = jnp.pad(jax.nn.silu(cond), ((0, 8 - (n_batch + 1)), (0, 0)))
    ones_g = jnp.ones((D_HEAD,), F32)

    for i in range(depth):
        j = i // 2
        mod = matmul(cond, w_ada[i], F32, 8, 1024, "ada_mod")[:n_batch + 1] + b_ada[i]
        mod = mod.reshape(n_batch + 1, 6, 1, d)
        h = norm_mod(xs, norm1_g[i], mod, 0, 1, tiles_per_row, n_batch)
        if i % 2 == 0:
            z = matmul(h, w_in_ab[j].astype(BF16), F32, 512, 1024, "in_proj")
            a_width = s5_d.shape[1]
            b_heads = (z.shape[1] - a_width) // 5 // B_HEAD
            y_a = mixer_s5(z, n_batch, n_tok, n_ctx, s5_lambda_re[j], s5_lambda_im[j], s5_log_dt[j],
                           s5_b_re[j], s5_b_im[j], s5_c_re[j], s5_c_im[j], s5_d[j], s5_w_glu[j])
            o_f, o_b = mixer_gla(z, n_batch, n_tok, n_ctx, a_width, lb_all[j], b_heads, B_HEAD)
            gcb = (a_width + 4 * b_heads * B_HEAD) // (b_heads * B_HEAD)
            y_b = post_norm_gate(o_f, o_b, z, gcb, B_HEAD, jnp.ones((B_HEAD,), F32))
            m1, m2 = y_a, y_b
        else:
            cd_cols = w_in_cd.shape[2]
            cd_pad = -(-cd_cols // 1024) * 1024
            w_cd = jnp.pad(w_in_cd[j].astype(BF16), ((0, 0), (0, cd_pad - cd_cols)))
            z = matmul(h, w_cd, F32, 512, 1024, "in_proj")
            c_heads = ret_decay_logit.shape[2]
            d_heads = gdn_a_log.shape[2]
            o_f, o_b = mixer_retention(z, n_batch, n_tok, n_ctx, ret_decay_logit[j], c_heads)
            y_c = post_norm_gate(o_f, o_b, z, (2 * c_heads * C_QK + c_heads * C_V) // (c_heads * C_V), C_V,
                                 jnp.ones((C_V,), F32))
            col0 = 2 * c_heads * C_QK + 2 * c_heads * C_V
            o_f, o_b = mixer_gdn(z, n_batch, n_tok, n_ctx, col0, gdn_conv_w[j], gdn_a_log[j], gdn_dt_bias[j],
                                 d_heads, D_HEAD)
            y_d = post_norm_gate(o_f, o_b, z, (col0 + 3 * d_heads * D_HEAD) // (d_heads * D_HEAD), D_HEAD,
                                 gdn_norm_g[j])
            m1, m2 = y_c, y_d
        xs = out_proj(m1, m2, w_out[i].astype(BF16), xs, mod, 2, tiles_per_row, n_batch)
        h2, h2_packed = norm_mod(xs, norm2_g[i], mod, 3, 4, tiles_per_row, n_batch, packed=True)
        xs = moe_block(h2, h2_packed, xs, mod, 5, tiles_per_row, n_batch, w_router[i], router_bias[i],
                       w_exp_gate[i], w_exp_up[i], w_exp_down[i], w_sh_gate[i], w_sh_up[i], w_sh_down[i])
    out = final_norm(xs, final_norm_g, n_batch, tiles_per_row, ctx_tiles)
    return out.reshape(n_batch, n_lat, d)
```
